```python
import math
import jax, jax.numpy as jnp
from jax import lax
import numpy as np

D_MODEL = 1024
BATCH = 8
SEQ = 8192
DEPTH = 4

GRID_W = 64
CTX_LEN = 256
N_MIXERS = 3
EPS = 1e-6
HEAD_DIM = 128
N_HEADS = D_MODEL // HEAD_DIM
N_KV_HEADS = N_HEADS // 4
Q_GROUP = N_HEADS // N_KV_HEADS
ATTN_Q_W = N_HEADS * HEAD_DIM
ATTN_KV_W = N_KV_HEADS * HEAD_DIM
AXIS_DIM = HEAD_DIM // 2
ROPE_THETA = 10000.0
Q_BLOCK = 128
CHUNK = 128
GMLP_WIDTH = D_MODEL
GMLP_GROUPS = 8
GMLP_GROUP_DIM = GMLP_WIDTH // GMLP_GROUPS
HYENA_WIDTH = D_MODEL
HYENA_ORDER = 2
FILTER_EMB = 33
N_BANDS = (FILTER_EMB - 1) // 2
FILTER_HIDDEN = 64
DECAY_TARGET = 1e-2
FAST_DECAY_PCT = 0.3
SLOW_DECAY_PCT = 1.5
MAX_DECAY = math.log(DECAY_TARGET) / FAST_DECAY_PCT
MIN_DECAY = math.log(DECAY_TARGET) / SLOW_DECAY_PCT
N_EXPERTS = 32
TOP_K = 4
D_FF = D_MODEL
SWIGLU_ALPHA = 1.702
SWIGLU_LIMIT = 7.0
EXPERT_BLOCK = 256

kernel_name = 'hybrid_gqa_gmlp_hyena_moe_dit'


def rmsnorm(x, g):
    xf = x.astype(jnp.float32)
    y = xf * lax.rsqrt(jnp.mean(xf * xf, axis=-1, keepdims=True) + EPS)
    return y.astype(x.dtype) * g


def axial_rope(n_tok):
    rows = n_tok // GRID_W
    row = jnp.repeat(jnp.arange(rows, dtype=jnp.float32), GRID_W)
    col = jnp.tile(jnp.arange(GRID_W, dtype=jnp.float32), rows)
    inv = ROPE_THETA ** (-jnp.arange(0, AXIS_DIM, 2, dtype=jnp.float32) / AXIS_DIM)
    ang_r = row[:, None] * inv
    ang_c = col[:, None] * inv
    ang = jnp.concatenate([ang_r, ang_r, ang_c, ang_c], axis=-1)
    return jnp.cos(ang), jnp.sin(ang)


def apply_rope(x, cos, sin):
    xr = x.reshape(*x.shape[:-1], 2, 2, AXIS_DIM // 2)
    rot = jnp.stack([-xr[..., 1, :], xr[..., 0, :]], axis=-2).reshape(x.shape)
    return x * cos[None, :, None, :].astype(x.dtype) + rot * sin[None, :, None, :].astype(x.dtype)


def gqa_attend(q, k, v):
    s = jnp.einsum('bqhgd,bkhd->bhgqk', q, k).astype(jnp.float32) * (HEAD_DIM ** -0.5)
    p = jax.nn.softmax(s, axis=-1).astype(v.dtype)
    return jnp.einsum('bhgqk,bkhd->bqhgd', p, v)


def attention_mixer(h_lat, h_ctx, w_in, q_gain, k_gain, cos, sin, ctx_queries):
    B, S, _ = h_lat.shape
    C = h_ctx.shape[1]

    def heads_kv(p_kv):
        b, l, _ = p_kv.shape
        k = rmsnorm(p_kv[..., :ATTN_KV_W].reshape(b, l, N_KV_HEADS, HEAD_DIM), k_gain)
        v = p_kv[..., ATTN_KV_W:].reshape(b, l, N_KV_HEADS, HEAD_DIM)
        return k, v

    def heads_q(p_q):
        b, l, _ = p_q.shape
        return rmsnorm(p_q.reshape(b, l, N_HEADS, HEAD_DIM), q_gain)

    p_lat = h_lat @ w_in
    q_lat = apply_rope(heads_q(p_lat[..., :ATTN_Q_W]), cos, sin)
    k_lat, v_lat = heads_kv(p_lat[..., ATTN_Q_W:])
    k_lat = apply_rope(k_lat, cos, sin)
    if ctx_queries:
        p_ctx = h_ctx @ w_in
        q_ctx = heads_q(p_ctx[..., :ATTN_Q_W])
        k_ctx, v_ctx = heads_kv(p_ctx[..., ATTN_Q_W:])
    else:
        k_ctx, v_ctx = heads_kv(h_ctx @ w_in[:, ATTN_Q_W:])
    k_all = jnp.concatenate([k_lat, k_ctx], axis=1)
    v_all = jnp.concatenate([v_lat, v_ctx], axis=1)
    n_blk = S // Q_BLOCK
    q_blocks = q_lat.reshape(B, n_blk, Q_BLOCK, N_KV_HEADS, Q_GROUP, HEAD_DIM).transpose(1, 0, 2, 3, 4, 5)
    o = lax.map(lambda qb: gqa_attend(qb, k_all, v_all), q_blocks)
    y_lat = o.transpose(1, 0, 2, 3, 4, 5).reshape(B, S, ATTN_Q_W)
    y_ctx = None
    if ctx_queries:
        y_ctx = gqa_attend(q_ctx.reshape(B, C, N_KV_HEADS, Q_GROUP, HEAD_DIM), k_ctx, v_ctx).reshape(B, C, ATTN_Q_W)
    return y_lat, y_ctx


def gmlp_mixer(h, w_in, v_gain, w_s, b_s):
    B, L, _ = h.shape
    z = jax.nn.gelu(h @ w_in, approximate=False)
    u, v = jnp.split(z, 2, axis=-1)
    v = rmsnorm(v, v_gain).reshape(B, L // CHUNK, CHUNK, GMLP_GROUPS, GMLP_GROUP_DIM)
    mixed = jnp.einsum('gpq,bnqgc->bnpgc', w_s, v) + b_s.T[None, None, :, :, None]
    return u * mixed.reshape(B, L, GMLP_WIDTH)


def short_conv(p, w, b):
    pp = jnp.pad(p, ((0, 0), (1, 1), (0, 0)))
    return pp[:, :-2] * w[0] + pp[:, 1:-1] * w[1] + pp[:, 2:] * w[2] + b


def hyena_filter(L, w1, b1, w2, b2, w3, b3, w4, b4, freq):
    t = jnp.linspace(0.0, 1.0, L, dtype=jnp.float32)[:, None]
    w = 2.0 * math.pi * jnp.arange(L, dtype=jnp.float32)[:, None] / L
    f = jnp.linspace(1e-4, N_BANDS - 1, N_BANDS, dtype=jnp.float32)[None, :]
    z = jnp.concatenate([t, jnp.cos(f * w), -jnp.sin(f * w)], axis=-1)
    a = jnp.sin(freq * (z @ w1 + b1))
    a = jnp.sin(freq * (a @ w2 + b2))
    a = jnp.sin(freq * (a @ w3 + b3))
    k = (a @ w4 + b4).astype(jnp.float32)
    deltas = jnp.abs(jnp.linspace(MIN_DECAY, MAX_DECAY, HYENA_WIDTH, dtype=jnp.float32))
    window = jnp.exp(-t * deltas[None, :])
    k_fwd = k[:, :HYENA_WIDTH] * window
    k_bwd = k[:, HYENA_WIDTH:] * window
    zero = jnp.zeros((1, HYENA_WIDTH), jnp.float32)
    return jnp.concatenate([k_fwd, zero, k_bwd[:0:-1]], axis=0)


def hyena_mixer(h, w_in, conv_w, conv_b, f_w1, f_b1, f_w2, f_b2, f_w3, f_b3, f_w4, f_b4, freq, d_skip):
    B, L, _ = h.shape
    p = short_conv(h @ w_in, conv_w, conv_b)
    x0, x1, v = jnp.split(p, HYENA_ORDER + 1, axis=-1)
    v = v * x1
    filt = hyena_filter(L, f_w1, f_b1, f_w2, f_b2, f_w3, f_b3, f_w4, f_b4, freq)
    vf = jnp.fft.rfft(v.astype(jnp.float32), n=2 * L, axis=1)
    kf = jnp.fft.rfft(filt, n=2 * L, axis=0)
    y = jnp.fft.irfft(vf * kf[None], n=2 * L, axis=1)[:, :L].astype(v.dtype)
    y = y + v * d_skip
    return y * x0


def expert_ffn(xb, w1, b1, w2, b2):
    hh = xb @ w1 + b1
    glu, lin = jnp.split(hh, 2, axis=-1)
    glu = jnp.minimum(glu, SWIGLU_LIMIT)
    lin = jnp.clip(lin, -SWIGLU_LIMIT, SWIGLU_LIMIT)
    return (glu * jax.nn.sigmoid(SWIGLU_ALPHA * glu) * (lin + 1.0)) @ w2 + b2


def moe(h, router_w, router_b, w1, b1, w2, b2):
    n_tok, d = h.shape
    logits = (h @ router_w + router_b).astype(jnp.float32)
    top_val, top_idx = lax.top_k(logits, TOP_K)
    gates = jax.nn.softmax(top_val, axis=-1).astype(h.dtype)
    n_assign = n_tok * TOP_K
    flat_e = top_idx.reshape(-1)
    order = jnp.argsort(flat_e)
    sorted_e = flat_e[order]
    counts = jnp.bincount(flat_e, length=N_EXPERTS)
    padded = (counts + EXPERT_BLOCK - 1) // EXPERT_BLOCK * EXPERT_BLOCK
    pad_end = jnp.cumsum(padded)
    pad_start = pad_end - padded
    start = jnp.cumsum(counts) - counts
    dest = pad_start[sorted_e] + (jnp.arange(n_assign) - start[sorted_e])
    n_blocks = -(-n_assign // EXPERT_BLOCK) + N_EXPERTS
    rows = n_blocks * EXPERT_BLOCK
    x_disp = jnp.zeros((rows, d), h.dtype).at[dest].set(h[order // TOP_K])
    block_e = jnp.minimum(jnp.searchsorted(pad_end, jnp.arange(n_blocks) * EXPERT_BLOCK, side='right'), N_EXPERTS - 1)

    def run_block(args):
        xb, e = args
        return expert_ffn(xb, w1[e], b1[e], w2[e], b2[e])

    y_disp = lax.map(run_block, (x_disp.reshape(n_blocks, EXPERT_BLOCK, d), block_e)).reshape(rows, d)
    y_assign = jnp.zeros((n_assign, d), h.dtype).at[order].set(y_disp[dest])
    return jnp.einsum('nkd,nk->nd', y_assign.reshape(n_tok, TOP_K, d), gates)


def setup_inputs(seed: int = 0) -> dict:
    key = jax.random.key(seed)
    ks = iter(jax.random.split(key, 40))

    def nrm(shape, scale):
        return jax.random.normal(next(ks), shape, jnp.float32) * scale

    n_a = len(range(0, DEPTH, N_MIXERS))
    n_b = len(range(1, DEPTH, N_MIXERS))
    n_c = len(range(2, DEPTH, N_MIXERS))
    D = D_MODEL
    return {
        'x': nrm((BATCH, SEQ, D), 1.0),
        'c': nrm((BATCH, D), 1.0),
        'ctx': nrm((BATCH, CTX_LEN, D), 1.0),
        'c_ctx': nrm((D,), 1.0),
        'ada_w': nrm((DEPTH, D, 6 * D), 0.5 * D ** -0.5),
        'ada_b': nrm((DEPTH, 6 * D), 0.02),
        'norm1_g': 1.0 + nrm((DEPTH, D), 0.02),
        'norm2_g': 1.0 + nrm((DEPTH, D), 0.02),
        'mix_w_out': nrm((DEPTH, D, D), D ** -0.5),
        'router_w': nrm((DEPTH, D, N_EXPERTS), D ** -0.5),
        'router_b': nrm((DEPTH, N_EXPERTS), 0.01),
        'exp_w1': nrm((DEPTH, N_EXPERTS, D, 2 * D_FF), D ** -0.5),
        'exp_b1': nrm((DEPTH, N_EXPERTS, 2 * D_FF), 0.01),
        'exp_w2': nrm((DEPTH, N_EXPERTS, D_FF, D), D_FF ** -0.5),
        'exp_b2': nrm((DEPTH, N_EXPERTS, D), 0.01),
        'attn_w_in': nrm((n_a, D, ATTN_Q_W + 2 * ATTN_KV_W), D ** -0.5),
        'attn_q_gain': 1.0 + nrm((n_a, HEAD_DIM), 0.02),
        'attn_k_gain': 1.0 + nrm((n_a, HEAD_DIM), 0.02),
        'gmlp_w_in': nrm((n_b, D, 2 * GMLP_WIDTH), D ** -0.5),
        'gmlp_v_gain': 1.0 + nrm((n_b, GMLP_WIDTH), 0.02),
        'gmlp_w_s': nrm((n_b, GMLP_GROUPS, CHUNK, CHUNK), CHUNK ** -0.5),
        'gmlp_b_s': 1.0 + nrm((n_b, GMLP_GROUPS, CHUNK), 0.02),
        'hyena_w_in': nrm((n_c, D, (HYENA_ORDER + 1) * HYENA_WIDTH), D ** -0.5),
        'hyena_conv_w': nrm((n_c, 3, (HYENA_ORDER + 1) * HYENA_WIDTH), 0.5),
        'hyena_conv_b': nrm((n_c, (HYENA_ORDER + 1) * HYENA_WIDTH), 0.01),
        'hyena_f_w1': nrm((n_c, FILTER_EMB, FILTER_HIDDEN), FILTER_EMB ** -0.5),
        'hyena_f_b1': nrm((n_c, FILTER_HIDDEN), 0.02),
        'hyena_f_w2': nrm((n_c, FILTER_HIDDEN, FILTER_HIDDEN), FILTER_HIDDEN ** -0.5),
        'hyena_f_b2': nrm((n_c, FILTER_HIDDEN), 0.02),
        'hyena_f_w3': nrm((n_c, FILTER_HIDDEN, FILTER_HIDDEN), FILTER_HIDDEN ** -0.5),
        'hyena_f_b3': nrm((n_c, FILTER_HIDDEN), 0.02),
        'hyena_f_w4': nrm((n_c, FILTER_HIDDEN, 2 * HYENA_WIDTH), 0.02 * FILTER_HIDDEN ** -0.5),
        'hyena_f_b4': nrm((n_c, 2 * HYENA_WIDTH), 0.002),
        'hyena_freq': 1.0 + nrm((n_c, FILTER_HIDDEN), 0.02),
        'hyena_d': nrm((n_c, HYENA_WIDTH), 0.5),
        'final_g': 1.0 + nrm((D,), 0.02),
    }


def reference(x, c, ctx, c_ctx, ada_w, ada_b, norm1_g, norm2_g, mix_w_out, router_w, router_b,
              exp_w1, exp_b1, exp_w2, exp_b2, attn_w_in, attn_q_gain, attn_k_gain,
              gmlp_w_in, gmlp_v_gain, gmlp_w_s, gmlp_b_s,
              hyena_w_in, hyena_conv_w, hyena_conv_b, hyena_f_w1, hyena_f_b1, hyena_f_w2, hyena_f_b2,
              hyena_f_w3, hyena_f_b3, hyena_f_w4, hyena_f_b4, hyena_freq, hyena_d, final_g):
    B, S, D = x.shape
    C = ctx.shape[1]
    cos, sin = axial_rope(S)
    x_lat, x_ctx = x, ctx
    for i in range(DEPTH):
        kind = i % N_MIXERS
        j = i // N_MIXERS
        last = i == DEPTH - 1
        mod_lat = jax.nn.silu(c) @ ada_w[i] + ada_b[i]
        mod_ctx = jax.nn.silu(c_ctx) @ ada_w[i] + ada_b[i]
        sh1, sc1, g1, sh2, sc2, g2 = jnp.split(mod_lat[:, None, :], 6, axis=-1)
        csh1, csc1, cg1, csh2, csc2, cg2 = jnp.split(mod_ctx, 6, axis=-1)
        h_lat = rmsnorm(x_lat, norm1_g[i]) * (1.0 + sc1) + sh1
        h_ctx = rmsnorm(x_ctx, norm1_g[i]) * (1.0 + csc1) + csh1
        if kind == 0:
            y_lat, y_ctx = attention_mixer(h_lat, h_ctx, attn_w_in[j], attn_q_gain[j], attn_k_gain[j],
                                           cos, sin, not last)
        elif kind == 1:
            y_lat = gmlp_mixer(h_lat, gmlp_w_in[j], gmlp_v_gain[j], gmlp_w_s[j], gmlp_b_s[j])
            y_ctx = None if last else gmlp_mixer(h_ctx, gmlp_w_in[j], gmlp_v_gain[j], gmlp_w_s[j], gmlp_b_s[j])
        else:
            hy = (hyena_w_in[j], hyena_conv_w[j], hyena_conv_b[j], hyena_f_w1[j], hyena_f_b1[j],
                  hyena_f_w2[j], hyena_f_b2[j], hyena_f_w3[j], hyena_f_b3[j], hyena_f_w4[j], hyena_f_b4[j],
                  hyena_freq[j], hyena_d[j])
            y_lat = hyena_mixer(h_lat, *hy)
            y_ctx = None if last else hyena_mixer(h_ctx, *hy)
        x_lat = x_lat + g1 * (y_lat @ mix_w_out[i])
        h2_lat = rmsnorm(x_lat, norm2_g[i]) * (1.0 + sc2) + sh2
        if last:
            out = moe(h2_lat.reshape(B * S, D), router_w[i], router_b[i], exp_w1[i], exp_b1[i], exp_w2[i], exp_b2[i])
            x_lat = x_lat + g2 * out.reshape(B, S, D)
        else:
            x_ctx = x_ctx + cg1 * (y_ctx @ mix_w_out[i])
            h2_ctx = rmsnorm(x_ctx, norm2_g[i]) * (1.0 + csc2) + csh2
            tokens = jnp.concatenate([h2_lat.reshape(B * S, D), h2_ctx.reshape(B * C, D)], axis=0)
            out = moe(tokens, router_w[i], router_b[i], exp_w1[i], exp_b1[i], exp_w2[i], exp_b2[i])
            x_lat = x_lat + g2 * out[:B * S].reshape(B, S, D)
            x_ctx = x_ctx + cg2 * out[B * S:].reshape(B, C, D)
    return rmsnorm(x_lat, final_g)
```

```python
import functools
import math

import jax
import jax.numpy as jnp
import numpy as np
from jax import lax
from jax.experimental import pallas as pl
from jax.experimental.pallas import tpu as pltpu

F32 = jnp.float32
BF16 = jnp.bfloat16
HIGHEST = lax.Precision.HIGHEST

EPS = 1e-6
GRID_W = 64
ROPE_THETA = 10000.0
HEAD_DIM = 128
N_KV_HEADS = 2
Q_GROUP = 4
CHUNK = 128
GMLP_GROUPS = 8
N_EXPERTS = 32
TOP_K = 4
SWIGLU_ALPHA = 1.702
SWIGLU_LIMIT = 7.0
N_BANDS = 16
DECAY_TARGET = 1e-2
MAX_DECAY = math.log(DECAY_TARGET) / 0.3
MIN_DECAY = math.log(DECAY_TARGET) / 1.5

LANES = 128
TM = 256
MOD_ROWS = 16
EXPERT_BLOCK = 256
FFT_N2 = 128
FFT_CB = 128
VMEM_LIMIT = 56 * 1024 * 1024


def _cparams(sem):
    return pltpu.CompilerParams(dimension_semantics=sem, vmem_limit_bytes=VMEM_LIMIT)


def _norm_mod(x, g, scale, shift):
    y = x * lax.rsqrt(jnp.mean(x * x, axis=-1, keepdims=True) + EPS)
    return y * g * (1.0 + scale) + shift


def _resident(shape, index_map):
    return pl.BlockSpec(shape, index_map, pipeline_mode=pl.Buffered(1))


def _mod_kernel(c_ref, w_ref, b_ref, o_ref):
    c = c_ref[...]
    s = c * jax.nn.sigmoid(c)
    o_ref[...] = jnp.dot(s, w_ref[...], precision=HIGHEST, preferred_element_type=F32) + b_ref[...]


def _modulation(cc, ada_w, ada_b):
    depth, d, d6 = ada_w.shape
    tn = 1536
    return pl.pallas_call(
        _mod_kernel,
        grid=(depth, d6 // tn),
        in_specs=[
            pl.BlockSpec((MOD_ROWS, d), lambda l, j: (0, 0)),
            pl.BlockSpec((None, d, tn), lambda l, j: (l, 0, j)),
            pl.BlockSpec((None, 1, tn), lambda l, j: (l, 0, j)),
        ],
        out_specs=pl.BlockSpec((None, MOD_ROWS, tn), lambda l, j: (l, 0, j)),
        out_shape=jax.ShapeDtypeStruct((depth, MOD_ROWS, d6), F32),
        compiler_params=_cparams(("arbitrary", "arbitrary")),
        name="adaln_mod",
    )(cc, ada_w, ada_b.reshape(depth, 1, d6))


def _inproj_kernel(x_ref, mod_ref, g_ref, w_ref, o_ref, *, d):
    mod = mod_ref[...]
    h = _norm_mod(x_ref[...], g_ref[...], mod[:, d:2 * d], mod[:, 0:d]).astype(BF16)
    o_ref[...] = jnp.dot(h, w_ref[...], preferred_element_type=F32).astype(o_ref.dtype)


def _inproj(x, mod, g, w, n_lat_tiles):
    b, t, d = x.shape
    n_out = w.shape[1]
    return pl.pallas_call(
        functools.partial(_inproj_kernel, d=d),
        grid=(b, t // TM),
        in_specs=[
            pl.BlockSpec((None, TM, d), lambda i, j: (i, j, 0)),
            pl.BlockSpec((None, 1, 6 * d), lambda i, j: (jnp.where(j >= n_lat_tiles, b, i), 0, 0)),
            pl.BlockSpec((1, d), lambda i, j: (0, 0)),
            pl.BlockSpec((d, n_out), lambda i, j: (0, 0)),
        ],
        out_specs=pl.BlockSpec((None, TM, n_out), lambda i, j: (i, j, 0)),
        out_shape=jax.ShapeDtypeStruct((b, t, n_out), BF16),
        compiler_params=_cparams(("parallel", "parallel")),
        name="inproj",
    )(x, mod, g.reshape(1, d), w)


def _attn_proj_kernel(x_ref, mod_ref, g_ref, w_ref, qg_ref, kg_ref, cos_ref, sa_ref, sb_ref,
                      q_ref, k_ref, v_ref, *, d):
    mod = mod_ref[...]
    h = _norm_mod(x_ref[...], g_ref[...], mod[:, d:2 * d], mod[:, 0:d]).astype(BF16)
    p = jnp.dot(h, w_ref[...], preferred_element_type=F32)
    cos, sin_a, sin_b = cos_ref[...], sa_ref[...], sb_ref[...]
    n_q = Q_GROUP * N_KV_HEADS

    def head(ph, gain):
        y = ph * lax.rsqrt(jnp.mean(ph * ph, axis=-1, keepdims=True) + EPS) * gain
        return y * cos + pltpu.roll(y, LANES - 32, 1) * sin_a + pltpu.roll(y, 32, 1) * sin_b

    qg = qg_ref[...] * (HEAD_DIM ** -0.5)
    kg = kg_ref[...]
    for i in range(n_q):
        q_ref[:, i * HEAD_DIM:(i + 1) * HEAD_DIM] = head(p[:, i * HEAD_DIM:(i + 1) * HEAD_DIM], qg).astype(BF16)
    k0 = n_q * HEAD_DIM
    for i in range(N_KV_HEADS):
        k_ref[:, i * HEAD_DIM:(i + 1) * HEAD_DIM] = head(
            p[:, k0 + i * HEAD_DIM:k0 + (i + 1) * HEAD_DIM], kg).astype(BF16)
    v0 = k0 + N_KV_HEADS * HEAD_DIM
    v_ref[...] = p[:, v0:v0 + N_KV_HEADS * HEAD_DIM].astype(BF16)


def _attn_proj(x, mod, g, w, q_gain, k_gain, rope, n_lat_tiles):
    b, t, d = x.shape
    n_out = w.shape[1]
    qw = Q_GROUP * N_KV_HEADS * HEAD_DIM
    kw = N_KV_HEADS * HEAD_DIM
    row = lambda i, j: (i, j, 0)
    tab = pl.BlockSpec((TM, HEAD_DIM), lambda i, j: (j, 0))
    return pl.pallas_call(
        functools.partial(_attn_proj_kernel, d=d),
        grid=(b, t // TM),
        in_specs=[
            pl.BlockSpec((None, TM, d), row),
            pl.BlockSpec((None, 1, 6 * d), lambda i, j: (jnp.where(j >= n_lat_tiles, b, i), 0, 0)),
            pl.BlockSpec((1, d), lambda i, j: (0, 0)),
            pl.BlockSpec((d, n_out), lambda i, j: (0, 0)),
            pl.BlockSpec((1, HEAD_DIM), lambda i, j: (0, 0)),
            pl.BlockSpec((1, HEAD_DIM), lambda i, j: (0, 0)),
            tab, tab, tab,
        ],
        out_specs=[
            pl.BlockSpec((None, TM, qw), row),
            pl.BlockSpec((None, TM, kw), row),
            pl.BlockSpec((None, TM, kw), row),
        ],
        out_shape=[
            jax.ShapeDtypeStruct((b, t, qw), BF16),
            jax.ShapeDtypeStruct((b, t, kw), BF16),
            jax.ShapeDtypeStruct((b, t, kw), BF16),
        ],
        compiler_params=_cparams(("parallel", "parallel")),
        name="attn_proj",
    )(x, mod, g.reshape(1, d), w, q_gain.reshape(1, HEAD_DIM), k_gain.reshape(1, HEAD_DIM), *rope)


def _attn_kernel(q_ref, k_ref, v_ref, o_ref, m_scr, l_scr, acc_scr, *, n_lat, n_ctx, tk):
    t = pl.program_id(2)
    is_ctx = t * TM >= n_lat
    q = jnp.concatenate([q_ref[:, g * HEAD_DIM:(g + 1) * HEAD_DIM] for g in range(Q_GROUP)], axis=0)
    m_scr[...] = jnp.full(m_scr.shape, -jnp.inf, F32)
    l_scr[...] = jnp.zeros(l_scr.shape, F32)
    acc_scr[...] = jnp.zeros(acc_scr.shape, F32)

    def chunk(start, size):
        k = k_ref[pl.ds(start, size), :]
        v = v_ref[pl.ds(start, size), :]
        s = lax.dot_general(q, k, (((1,), (1,)), ((), ())), preferred_element_type=F32)
        m_prev = m_scr[...]
        m_new = jnp.maximum(m_prev, jnp.max(s, axis=-1, keepdims=True))
        alpha = jnp.exp(m_prev - m_new)
        p = jnp.exp(s - m_new)
        l_scr[...] = alpha * l_scr[...] + jnp.sum(p, axis=-1, keepdims=True)
        acc_scr[...] = alpha * acc_scr[...] + jnp.dot(p.astype(BF16), v, preferred_element_type=F32)
        m_scr[...] = m_new

    def body(i, carry):
        chunk(pl.multiple_of(i * tk, tk), tk)
        return carry

    lax.fori_loop(0, jnp.where(is_ctx, 0, n_lat // tk), body, 0)
    chunk(n_lat, n_ctx)
    out = acc_scr[...] / l_scr[...]
    for g in range(Q_GROUP):
        o_ref[:, g * HEAD_DIM:(g + 1) * HEAD_DIM] = out[g * TM:(g + 1) * TM].astype(o_ref.dtype)


def _attention(q, k, v, n_lat):
    b, t, _ = q.shape
    n_ctx = t - n_lat
    gw = Q_GROUP * HEAD_DIM
    tk = 512 if n_lat % 512 == 0 else TM
    m = Q_GROUP * TM
    return pl.pallas_call(
        functools.partial(_attn_kernel, n_lat=n_lat, n_ctx=n_ctx, tk=tk),
        grid=(b, N_KV_HEADS, t // TM),
        in_specs=[
            pl.BlockSpec((None, TM, gw), lambda i, h, j: (i, j, h)),
            pl.BlockSpec((None, t, HEAD_DIM), lambda i, h, j: (i, 0, h)),
            pl.BlockSpec((None, t, HEAD_DIM), lambda i, h, j: (i, 0, h)),
        ],
        out_specs=pl.BlockSpec((None, TM, gw), lambda i, h, j: (i, j, h)),
        out_shape=jax.ShapeDtypeStruct(q.shape, BF16),
        scratch_shapes=[
            pltpu.VMEM((m, 1), F32),
            pltpu.VMEM((m, 1), F32),
            pltpu.VMEM((m, HEAD_DIM), F32),
        ],
        compiler_params=_cparams(("parallel", "parallel", "parallel")),
        name="attention",
    )(q, k, v)


def _residual_router(y, x, mod, w_out, g2n, rwt, rb, xo_ref, h2_ref, idx_ref, gate_ref, *, d):
    xn = x + mod[:, 2 * d:3 * d] * jnp.dot(y, w_out, preferred_element_type=F32)
    xo_ref[...] = xn
    h2 = _norm_mod(xn, g2n, mod[:, 4 * d:5 * d], mod[:, 3 * d:4 * d])
    h2_ref[...] = h2
    lg = lax.dot_general(rwt, h2, (((1,), (1,)), ((), ())), precision=HIGHEST,
                         preferred_element_type=F32) + rb
    row = lax.broadcasted_iota(jnp.int32, lg.shape, 0)
    vals, idxs = [], []
    for _ in range(TOP_K):
        m = jnp.max(lg, axis=0, keepdims=True)
        i = jnp.min(jnp.where(lg == m, row, N_EXPERTS), axis=0, keepdims=True)
        vals.append(m)
        idxs.append(i)
        lg = jnp.where(row == i, -jnp.inf, lg)
    es = [jnp.exp(vv - vals[0]) for vv in vals]
    tot = es[0] + es[1] + es[2] + es[3]
    idx_ref[...] = jnp.concatenate(idxs, axis=0)
    gate_ref[...] = jnp.concatenate([e / tot for e in es], axis=0)


def _post_kernel(y_ref, x_ref, mod_ref, w_ref, g_ref, rwt_ref, rb_ref, xo_ref, h2_ref, idx_ref, gate_ref, *, d):
    _residual_router(y_ref[...], x_ref[...], mod_ref[...], w_ref[...], g_ref[...], rwt_ref[...], rb_ref[...],
                     xo_ref, h2_ref, idx_ref, gate_ref, d=d)


def _post_specs(b, t, d, n_lat_tiles):
    nt = t // TM
    row = lambda i, j: (i, j, 0)
    const2 = lambda i, j: (0, 0)
    in_tail = [
        pl.BlockSpec((d, d), const2),
        pl.BlockSpec((1, d), const2),
        pl.BlockSpec((N_EXPERTS, d), const2),
        pl.BlockSpec((N_EXPERTS, 1), const2),
    ]
    out_specs = [
        pl.BlockSpec((None, TM, d), row),
        pl.BlockSpec((None, TM, d), row),
        pl.BlockSpec((TOP_K, TM), lambda i, j: (0, i * nt + j)),
        pl.BlockSpec((TOP_K, TM), lambda i, j: (0, i * nt + j)),
    ]
    out_shape = [
        jax.ShapeDtypeStruct((b, t, d), F32),
        jax.ShapeDtypeStruct((b, t, d), F32),
        jax.ShapeDtypeStruct((TOP_K, b * t), jnp.int32),
        jax.ShapeDtypeStruct((TOP_K, b * t), F32),
    ]
    mod_spec = pl.BlockSpec((None, 1, 6 * d), lambda i, j: (jnp.where(j >= n_lat_tiles, b, i), 0, 0))
    return mod_spec, in_tail, out_specs, out_shape


def _post(y, x, mod, w_out, g2n, router_w, router_b, n_lat_tiles):
    b, t, d = x.shape
    mod_spec, in_tail, out_specs, out_shape = _post_specs(b, t, d, n_lat_tiles)
    row = lambda i, j: (i, j, 0)
    return pl.pallas_call(
        functools.partial(_post_kernel, d=d),
        grid=(b, t // TM),
        in_specs=[pl.BlockSpec((None, TM, d), row), pl.BlockSpec((None, TM, d), row), mod_spec] + in_tail,
        out_specs=out_specs,
        out_shape=out_shape,
        compiler_params=_cparams(("parallel", "parallel")),
        name="outproj_router",
    )(y, x, mod, w_out, g2n.reshape(1, d), router_w.T, router_b.reshape(N_EXPERTS, 1))


def _gmlp_kernel(x_ref, mod_ref, g1_ref, win_ref, vg_ref, ws_ref, bs_ref, w_ref, g_ref, rwt_ref, rb_ref,
                 xo_ref, h2_ref, idx_ref, gate_ref, y_scr, *, d):
    mod = mod_ref[...]
    x = x_ref[...]
    h = _norm_mod(x, g1_ref[...], mod[:, d:2 * d], mod[:, 0:d]).astype(BF16)
    z = jnp.dot(h, win_ref[...], preferred_element_type=F32)
    z = 0.5 * z * (1.0 + lax.erf(z * (2.0 ** -0.5)))
    width = z.shape[1] // 2
    u, v = z[:, :width], z[:, width:]
    v = (v * lax.rsqrt(jnp.mean(v * v, axis=-1, keepdims=True) + EPS) * vg_ref[...]).astype(BF16)
    gd = width // GMLP_GROUPS
    for n in range(TM // CHUNK):
        r = slice(n * CHUNK, (n + 1) * CHUNK)
        for g in range(GMLP_GROUPS):
            cs = slice(g * gd, (g + 1) * gd)
            mixed = jnp.dot(ws_ref[g], v[r, cs], preferred_element_type=F32) + bs_ref[:, g:g + 1]
            y_scr[r, cs] = (u[r, cs] * mixed).astype(BF16)
    _residual_router(y_scr[...], x, mod, w_ref[...], g_ref[...], rwt_ref[...], rb_ref[...],
                     xo_ref, h2_ref, idx_ref, gate_ref, d=d)


def _gmlp_layer(x, mod, g1n, w_in, v_gain, w_s, b_s, w_out, g2n, router_w, router_b, n_lat_tiles):
    b, t, d = x.shape
    width = w_in.shape[1] // 2
    mod_spec, in_tail, out_specs, out_shape = _post_specs(b, t, d, n_lat_tiles)
    row = lambda i, j: (i, j, 0)
    const2 = lambda i, j: (0, 0)
    return pl.pallas_call(
        functools.partial(_gmlp_kernel, d=d),
        grid=(b, t // TM),
        in_specs=[
            pl.BlockSpec((None, TM, d), row),
            mod_spec,
            pl.BlockSpec((1, d), const2),
            pl.BlockSpec((d, 2 * width), const2),
            pl.BlockSpec((1, width), const2),
            pl.BlockSpec((GMLP_GROUPS, CHUNK, CHUNK), lambda i, j: (0, 0, 0)),
            pl.BlockSpec((CHUNK, GMLP_GROUPS), const2),
        ] + in_tail,
        out_specs=out_specs,
        out_shape=out_shape,
        scratch_shapes=[pltpu.VMEM((TM, width), BF16)],
        compiler_params=_cparams(("parallel", "parallel")),
        name="gmlp_layer",
    )(x, mod, g1n.reshape(1, d), w_in, v_gain.reshape(1, width), w_s, b_s.T, w_out, g2n.reshape(1, d),
      router_w.T, router_b.reshape(N_EXPERTS, 1))


def _filter_mlp_kernel(z_ref, t_ref, w1, b1, w2, b2, w3, b3, w4, b4, fr, dl, kf_ref, kb_ref, *, width, tl):
    dot = functools.partial(jnp.dot, precision=HIGHEST, preferred_element_type=F32)
    f = fr[...]
    a = jnp.sin(f * (dot(z_ref[...], w1[...]) + b1[...]))
    a = jnp.sin(f * (dot(a, w2[...]) + b2[...]))
    a = jnp.sin(f * (dot(a, w3[...]) + b3[...]))
    k = dot(a, w4[...]) + b4[...]
    window = jnp.exp(-t_ref[...] * dl[...])
    kf_ref[...] = k[:, :width] * window
    pos = pl.program_id(0) * tl + lax.broadcasted_iota(jnp.int32, (tl, 1), 0)
    kb_ref[...] = jnp.where(pos == 0, 0.0, k[:, width:] * window)


def _hyena_filters(length, f_w1, f_b1, f_w2, f_b2, f_w3, f_b3, f_w4, f_b4, freq):
    width = f_w4.shape[1] // 2
    hid = f_w1.shape[1]
    emb = 2 * N_BANDS + 1
    t = jnp.linspace(0.0, 1.0, length, dtype=F32)[:, None]
    w = 2.0 * math.pi * jnp.arange(length, dtype=F32)[:, None] / length
    f = jnp.linspace(1e-4, N_BANDS - 1, N_BANDS, dtype=F32)[None, :]
    z = jnp.concatenate([t, jnp.cos(f * w), -jnp.sin(f * w), jnp.zeros((length, hid - emb), F32)], axis=-1)
    w1p = jnp.concatenate([f_w1, jnp.zeros((hid - emb, hid), F32)], axis=0)
    deltas = jnp.abs(jnp.linspace(MIN_DECAY, MAX_DECAY, width, dtype=F32))[None, :]
    tl = min(length, 512)
    full = lambda shape: pl.BlockSpec(shape, lambda i: (0, 0))
    return pl.pallas_call(
        functools.partial(_filter_mlp_kernel, width=width, tl=tl),
        grid=(length // tl,),
        in_specs=[
            pl.BlockSpec((tl, hid), lambda i: (i, 0)),
            pl.BlockSpec((tl, 1), lambda i: (i, 0)),
            full((hid, hid)), full((1, hid)), full((hid, hid)), full((1, hid)), full((hid, hid)), full((1, hid)),
            full((hid, 2 * width)), full((1, 2 * width)), full((1, hid)), full((1, width)),
        ],
        out_specs=[pl.BlockSpec((tl, width), lambda i: (i, 0)), pl.BlockSpec((tl, width), lambda i: (i, 0))],
        out_shape=[jax.ShapeDtypeStruct((length, width), F32), jax.ShapeDtypeStruct((length, width), F32)],
        compiler_params=_cparams(("parallel",)),
        name="hyena_filter_mlp",
    )(z, t, w1p, f_b1.reshape(1, hid), f_w2, f_b2.reshape(1, hid), f_w3, f_b3.reshape(1, hid),
      f_w4, f_b4.reshape(1, 2 * width), freq.reshape(1, hid), deltas)


def _fft_plan(length):
    n = 2 * length
    n1 = n // FFT_N2
    nz = n1 // 2
    ku = -(-(nz + 1) // 8) * 8
    return n, n1, nz, ku


def _fft_tables(length):
    n, n1, nz, ku = _fft_plan(length)
    i2 = np.arange(FFT_N2)[:, None, None]
    k1 = np.arange(ku)[None, :, None]
    i1 = np.arange(nz)[None, None, :]
    phi = 2.0 * np.pi * (((FFT_N2 * i1 + i2) * k1) % n) / n
    fwd = np.concatenate([np.cos(phi), -np.sin(phi)], axis=1)
    wgt = np.where((k1 == 0) | (k1 == nz), 1.0, np.where(k1 < nz, 2.0, 0.0)) / n
    inv = np.concatenate([np.cos(phi) * wgt, -np.sin(phi) * wgt], axis=1).transpose(0, 2, 1)
    th = 2.0 * np.pi * ((np.arange(FFT_N2)[:, None] * np.arange(FFT_N2)[None, :]) % FFT_N2) / FFT_N2
    c, s = np.cos(th), np.sin(th)
    f2 = np.block([[c, s], [-s, c]])
    f2i = np.block([[c, -s], [s, c]])
    as_bf = lambda a: jnp.asarray(a, F32).astype(BF16)
    return as_bf(fwd), as_bf(inv), as_bf(f2), as_bf(f2i)


def _fft_stage1(src_ref, fwd_ref, a_ref, *, nz, ku):
    slab = 2 * FFT_N2

    def body(i2, carry):
        rows = src_ref[pl.ds(i2, nz, stride=FFT_N2), :].astype(BF16)
        r = jnp.dot(fwd_ref[i2], rows, preferred_element_type=F32)
        a_ref[pl.ds(i2, ku, stride=slab), :] = r[:ku]
        a_ref[pl.ds(FFT_N2 + i2, ku, stride=slab), :] = r[ku:]
        return carry

    lax.fori_loop(0, FFT_N2, body, 0)


def _short_conv(p_ref, w_ref, b_ref, pad_ref, emit, *, length):
    step = min(length, 512)
    cb = p_ref.shape[-1]
    pad_ref[pl.ds(0, 8), :] = jnp.zeros((8, cb), F32)
    pad_ref[pl.ds(length + 8, 8), :] = jnp.zeros((8, cb), F32)
    for j in range(length // step):
        pad_ref[pl.ds(8 + j * step, step), :] = p_ref[pl.ds(j * step, step), :].astype(F32)
    w = w_ref[...]
    for j in range(length // step):
        r0 = j * step
        val = (pad_ref[pl.ds(r0 + 7, step), :] * w[0:1] + pad_ref[pl.ds(r0 + 8, step), :] * w[1:2]
               + pad_ref[pl.ds(r0 + 9, step), :] * w[2:3] + b_ref[...])
        emit(r0, step, val)


def _kf_kernel(kf_ref, kb_ref, fwd_ref, f2_ref, o_ref, a_ref, *, nz, ku):
    slab = 2 * FFT_N2
    for src, sign in ((kf_ref, 1.0), (kb_ref, -1.0)):
        _fft_stage1(src, fwd_ref, a_ref, nz=nz, ku=ku)

        def body(k1, carry, sign=sign, first=(src is kf_ref)):
            a = a_ref[pl.ds(pl.multiple_of(k1 * slab, slab), slab), :].astype(BF16)
            xk = jnp.dot(f2_ref[...], a, preferred_element_type=F32)
            if first:
                o_ref[k1] = xk
            else:
                o_ref[k1, :FFT_N2, :] = o_ref[k1, :FFT_N2, :] + xk[:FFT_N2]
                o_ref[k1, FFT_N2:, :] = o_ref[k1, FFT_N2:, :] - xk[FFT_N2:]
            return carry

        lax.fori_loop(0, ku, body, 0)


def _filter_spectrum(k_fwd, k_bwd, tables):
    length, width = k_fwd.shape
    _, _, nz, ku = _fft_plan(length)
    fwd, _, f2, _ = tables
    blk = pl.BlockSpec((length, FFT_CB), lambda c: (0, c))
    return pl.pallas_call(
        functools.partial(_kf_kernel, nz=nz, ku=ku),
        grid=(width // FFT_CB,),
        in_specs=[
            blk, blk,
            pl.BlockSpec(fwd.shape, lambda c: (0, 0, 0)),
            pl.BlockSpec(f2.shape, lambda c: (0, 0)),
        ],
        out_specs=pl.BlockSpec((ku, 2 * FFT_N2, FFT_CB), lambda c: (0, 0, c)),
        out_shape=jax.ShapeDtypeStruct((ku, 2 * FFT_N2, width), F32),
        scratch_shapes=[pltpu.VMEM((ku * 2 * FFT_N2, FFT_CB), F32)],
        compiler_params=_cparams(("parallel",)),
        name="hyena_filter_spectrum",
    )(k_fwd, k_bwd, fwd, f2)


def _hyena_conv_kernel(px0_ref, px1_ref, pv_ref, w0_ref, w1_ref, wv_ref, b0_ref, b1_ref, bv_ref, dsk_ref,
                       kf_ref, fwd_ref, inv_ref, f2_ref, f2i_ref, o_ref, a_ref, vv_ref, *, length, nz, ku):
    slab = 2 * FFT_N2

    def set_vv(r0, rows, val):
        vv_ref[pl.ds(r0, rows), :] = val

    def mul_vv(r0, rows, val):
        vv_ref[pl.ds(r0, rows), :] = vv_ref[pl.ds(r0, rows), :] * val

    _short_conv(pv_ref, wv_ref, bv_ref, a_ref, set_vv, length=length)
    _short_conv(px1_ref, w1_ref, b1_ref, a_ref, mul_vv, length=length)
    _fft_stage1(vv_ref, fwd_ref, a_ref, nz=nz, ku=ku)

    def freq_body(k1, carry):
        rows = pl.ds(pl.multiple_of(k1 * slab, slab), slab)
        xk = jnp.dot(f2_ref[...], a_ref[rows, :].astype(BF16), preferred_element_type=F32)
        kf = kf_ref[k1]
        xr, xi = xk[:FFT_N2], xk[FFT_N2:]
        kr, ki = kf[:FFT_N2], kf[FFT_N2:]
        prod = jnp.concatenate([xr * kr - xi * ki, xr * ki + xi * kr], axis=0).astype(BF16)
        a_ref[rows, :] = jnp.dot(f2i_ref[...], prod, preferred_element_type=F32)
        return carry

    lax.fori_loop(0, nz + 1, freq_body, 0)

    def time_body(i2, carry):
        re = a_ref[pl.ds(i2, ku, stride=slab), :]
        im = a_ref[pl.ds(FFT_N2 + i2, ku, stride=slab), :]
        q = jnp.concatenate([re, im], axis=0).astype(BF16)
        y = jnp.dot(inv_ref[i2], q, preferred_element_type=F32)
        rows = pl.ds(i2, nz, stride=FFT_N2)
        vv_ref[rows, :] = y + vv_ref[rows, :] * dsk_ref[...]
        return carry

    lax.fori_loop(0, FFT_N2, time_body, 0)

    def emit_out(r0, rows, val):
        o_ref[pl.ds(r0, rows), :] = (vv_ref[pl.ds(r0, rows), :] * val).astype(o_ref.dtype)

    _short_conv(px0_ref, w0_ref, b0_ref, a_ref, emit_out, length=length)


def _hyena_conv(p, conv_w, conv_b, d_skip, kf, tables, length):
    b, _, w3 = p.shape
    width = w3 // 3
    ncb = width // FFT_CB
    _, _, nz, ku = _fft_plan(length)
    fwd, inv, f2, f2i = tables
    pblk = lambda part: _resident((None, length, FFT_CB), lambda c, i: (i, 0, part * ncb + c))
    wblk = lambda part: pl.BlockSpec((3, FFT_CB), lambda c, i: (0, part * ncb + c))
    bblk = lambda part: pl.BlockSpec((1, FFT_CB), lambda c, i: (0, part * ncb + c))
    return pl.pallas_call(
        functools.partial(_hyena_conv_kernel, length=length, nz=nz, ku=ku),
        grid=(ncb, b),
        in_specs=[
            pblk(0), pblk(1), pblk(2), wblk(0), wblk(1), wblk(2), bblk(0), bblk(1), bblk(2),
            pl.BlockSpec((1, FFT_CB), lambda c, i: (0, c)),
            _resident((ku, 2 * FFT_N2, FFT_CB), lambda c, i: (0, 0, c)),
            _resident(fwd.shape, lambda c, i: (0, 0, 0)),
            _resident(inv.shape, lambda c, i: (0, 0, 0)),
            _resident(f2.shape, lambda c, i: (0, 0)),
            _resident(f2i.shape, lambda c, i: (0, 0)),
        ],
        out_specs=pl.BlockSpec((None, length, FFT_CB), lambda c, i: (i, 0, c)),
        out_shape=jax.ShapeDtypeStruct((b, length, width), BF16),
        scratch_shapes=[
            pltpu.VMEM((max(ku * 2 * FFT_N2, length + 16), FFT_CB), F32),
            pltpu.VMEM((length, FFT_CB), F32),
        ],
        compiler_params=_cparams(("arbitrary", "arbitrary")),
        name="hyena_long_conv",
    )(p, p, p, conv_w, conv_w, conv_w, conv_b.reshape(1, w3), conv_b.reshape(1, w3), conv_b.reshape(1, w3),
      d_skip.reshape(1, width), kf, fwd, inv, f2, f2i)


def _dense_dft_tables(length):
    n = 2 * length
    th = 2.0 * np.pi * ((np.arange(n)[:, None] * np.arange(length)[None, :]) % n) / n
    fwd = np.concatenate([np.cos(th), -np.sin(th)], axis=0)
    inv = np.concatenate([np.cos(th), -np.sin(th)], axis=0).T / n
    return jnp.asarray(fwd, F32).astype(BF16), jnp.asarray(inv, F32).astype(BF16)


def _kf_dense_kernel(kf_ref, kb_ref, fwd_ref, o_ref, *, n):
    xf = jnp.dot(fwd_ref[...], kf_ref[...].astype(BF16), preferred_element_type=F32)
    xb = jnp.dot(fwd_ref[...], kb_ref[...].astype(BF16), preferred_element_type=F32)
    o_ref[:n, :] = xf[:n] + xb[:n]
    o_ref[n:, :] = xf[n:] - xb[n:]


def _filter_spectrum_dense(k_fwd, k_bwd, fwd):
    length, width = k_fwd.shape
    n = 2 * length
    blk = pl.BlockSpec((length, FFT_CB), lambda c: (0, c))
    return pl.pallas_call(
        functools.partial(_kf_dense_kernel, n=n),
        grid=(width // FFT_CB,),
        in_specs=[blk, blk, pl.BlockSpec(fwd.shape, lambda c: (0, 0))],
        out_specs=pl.BlockSpec((2 * n, FFT_CB), lambda c: (0, c)),
        out_shape=jax.ShapeDtypeStruct((2 * n, width), F32),
        compiler_params=_cparams(("parallel",)),
        name="hyena_ctx_filter_spectrum",
    )(k_fwd, k_bwd, fwd)


def _hyena_ctx_kernel(px0_ref, px1_ref, pv_ref, w0_ref, w1_ref, wv_ref, b0_ref, b1_ref, bv_ref, dsk_ref,
                      kf_ref, fwd_ref, inv_ref, o_ref, pad_ref, vv_ref, *, length):
    n = 2 * length

    def set_vv(r0, rows, val):
        vv_ref[pl.ds(r0, rows), :] = val

    def mul_vv(r0, rows, val):
        vv_ref[pl.ds(r0, rows), :] = vv_ref[pl.ds(r0, rows), :] * val

    _short_conv(pv_ref, wv_ref, bv_ref, pad_ref, set_vv, length=length)
    _short_conv(px1_ref, w1_ref, b1_ref, pad_ref, mul_vv, length=length)
    vv = vv_ref[...]
    xk = jnp.dot(fwd_ref[...], vv.astype(BF16), preferred_element_type=F32)
    kf = kf_ref[...]
    xr, xi, kr, ki = xk[:n], xk[n:], kf[:n], kf[n:]
    prod = jnp.concatenate([xr * kr - xi * ki, xr * ki + xi * kr], axis=0).astype(BF16)
    vv_ref[...] = jnp.dot(inv_ref[...], prod, preferred_element_type=F32) + vv * dsk_ref[...]

    def emit_out(r0, rows, val):
        o_ref[pl.ds(r0, rows), :] = (vv_ref[pl.ds(r0, rows), :] * val).astype(o_ref.dtype)

    _short_conv(px0_ref, w0_ref, b0_ref, pad_ref, emit_out, length=length)


def _hyena_ctx_conv(p, conv_w, conv_b, d_skip, kf, fwd, inv, n_lat, length):
    b, _, w3 = p.shape
    width = w3 // 3
    ncb = width // FFT_CB
    rb = n_lat // length
    pblk = lambda part: pl.BlockSpec((None, length, FFT_CB), lambda c, i: (i, rb, part * ncb + c))
    wblk = lambda part: pl.BlockSpec((3, FFT_CB), lambda c, i: (0, part * ncb + c))
    bblk = lambda part: pl.BlockSpec((1, FFT_CB), lambda c, i: (0, part * ncb + c))
    return pl.pallas_call(
        functools.partial(_hyena_ctx_kernel, length=length),
        grid=(ncb, b),
        in_specs=[
            pblk(0), pblk(1), pblk(2), wblk(0), wblk(1), wblk(2), bblk(0), bblk(1), bblk(2),
            pl.BlockSpec((1, FFT_CB), lambda c, i: (0, c)),
            pl.BlockSpec((4 * length, FFT_CB), lambda c, i: (0, c)),
            pl.BlockSpec(fwd.shape, lambda c, i: (0, 0)),
            pl.BlockSpec(inv.shape, lambda c, i: (0, 0)),
        ],
        out_specs=pl.BlockSpec((None, length, FFT_CB), lambda c, i: (i, 0, c)),
        out_shape=jax.ShapeDtypeStruct((b, length, width), BF16),
        scratch_shapes=[pltpu.VMEM((length + 16, FFT_CB), F32), pltpu.VMEM((length, FFT_CB), F32)],
        compiler_params=_cparams(("arbitrary", "arbitrary")),
        name="hyena_ctx_conv",
    )(p, p, p, conv_w, conv_w, conv_w, conv_b.reshape(1, w3), conv_b.reshape(1, w3), conv_b.reshape(1, w3),
      d_skip.reshape(1, width), kf, fwd, inv)


def _gather_kernel(tok_ref, h_hbm, o_ref, sem):
    rows = o_ref.shape[0]

    def issue(r, carry):
        pltpu.make_async_copy(h_hbm.at[pl.ds(tok_ref[0, r], 1)], o_ref.at[pl.ds(r, 1)], sem).start()
        return carry

    lax.fori_loop(0, rows, issue, 0, unroll=8)
    pltpu.make_async_copy(h_hbm.at[pl.ds(0, rows)], o_ref, sem).wait()


def _dispatch(h2_flat, row_token):
    n_blocks = row_token.shape[0]
    d = h2_flat.shape[1]
    return pl.pallas_call(
        _gather_kernel,
        grid=(n_blocks,),
        in_specs=[
            pl.BlockSpec((None, 1, EXPERT_BLOCK), lambda i: (i, 0, 0), memory_space=pltpu.SMEM),
            pl.BlockSpec(memory_space=pl.ANY),
        ],
        out_specs=pl.BlockSpec((EXPERT_BLOCK, d), lambda i: (i, 0)),
        out_shape=jax.ShapeDtypeStruct((n_blocks * EXPERT_BLOCK, d), h2_flat.dtype),
        scratch_shapes=[pltpu.SemaphoreType.DMA(())],
        compiler_params=_cparams(("arbitrary",)),
        name="moe_dispatch",
    )(row_token, h2_flat)


def _ffn_kernel(be_ref, x_ref, w1_ref, b1_ref, w2_ref, b2_ref, o_ref, *, d_ff):
    hh = jnp.dot(x_ref[...].astype(BF16), w1_ref[...], preferred_element_type=F32) + b1_ref[...]
    glu = jnp.minimum(hh[:, :d_ff], SWIGLU_LIMIT)
    lin = jnp.clip(hh[:, d_ff:], -SWIGLU_LIMIT, SWIGLU_LIMIT)
    act = (glu * jax.nn.sigmoid(SWIGLU_ALPHA * glu) * (lin + 1.0)).astype(BF16)
    o_ref[...] = jnp.dot(act, w2_ref[...], preferred_element_type=F32) + b2_ref[...]


def _expert_ffn(x_disp, block_e, w1, b1, w2, b2):
    rows, d = x_disp.shape
    n_e, _, ff2 = w1.shape
    d_ff = ff2 // 2
    return pl.pallas_call(
        functools.partial(_ffn_kernel, d_ff=d_ff),
        grid_spec=pltpu.PrefetchScalarGridSpec(
            num_scalar_prefetch=1,
            grid=(rows // EXPERT_BLOCK,),
            in_specs=[
                pl.BlockSpec((EXPERT_BLOCK, d), lambda i, be: (i, 0)),
                pl.BlockSpec((None, d, ff2), lambda i, be: (be[i], 0, 0)),
                pl.BlockSpec((None, 1, ff2), lambda i, be: (be[i], 0, 0)),
                pl.BlockSpec((None, d_ff, d), lambda i, be: (be[i], 0, 0)),
                pl.BlockSpec((None, 1, d), lambda i, be: (be[i], 0, 0)),
            ],
            out_specs=pl.BlockSpec((EXPERT_BLOCK, d), lambda i, be: (i, 0)),
        ),
        out_shape=jax.ShapeDtypeStruct((rows, d), F32),
        compiler_params=_cparams(("arbitrary",)),
        name="moe_expert_ffn",
    )(block_e, x_disp, w1, b1.reshape(n_e, 1, ff2), w2, b2.reshape(n_e, 1, d))


def _combine_kernel(pos_ref, y_hbm, gate_ref, x_ref, mod_ref, fg_ref, o_ref, buf, sem, *, d, final):
    def issue(r, carry):
        for k in range(TOP_K):
            pltpu.make_async_copy(y_hbm.at[pl.ds(pos_ref[k, r], 1)], buf.at[k, pl.ds(r, 1)], sem).start()
        return carry

    lax.fori_loop(0, TM, issue, 0, unroll=4)
    for k in range(TOP_K):
        pltpu.make_async_copy(y_hbm.at[pl.ds(0, TM)], buf.at[k], sem).wait()
    gate = gate_ref[...]
    out = buf[0] * gate[:, 0:1]
    for k in range(1, TOP_K):
        out = out + buf[k] * gate[:, k:k + 1]
    xn = x_ref[...] + mod_ref[:, 5 * d:6 * d] * out
    if final:
        xn = xn * lax.rsqrt(jnp.mean(xn * xn, axis=-1, keepdims=True) + EPS) * fg_ref[...]
    o_ref[...] = xn


def _combine(y_disp, pos, gates, x, mod, final_g, n_lat_tiles, final):
    b, t, d = x.shape
    nt = t // TM
    nt_out = n_lat_tiles if final else nt
    return pl.pallas_call(
        functools.partial(_combine_kernel, d=d, final=final),
        grid=(b, nt_out),
        in_specs=[
            pl.BlockSpec((None, TOP_K, TM), lambda i, j: (i * nt + j, 0, 0), memory_space=pltpu.SMEM),
            pl.BlockSpec(memory_space=pl.ANY),
            pl.BlockSpec((TM, TOP_K), lambda i, j: (i * nt + j, 0)),
            pl.BlockSpec((None, TM, d), lambda i, j: (i, j, 0)),
            pl.BlockSpec((None, 1, 6 * d), lambda i, j: (jnp.where(j >= n_lat_tiles, b, i), 0, 0)),
            pl.BlockSpec((1, d), lambda i, j: (0, 0)),
        ],
        out_specs=pl.BlockSpec((None, TM, d), lambda i, j: (i, j, 0)),
        out_shape=jax.ShapeDtypeStruct((b, nt_out * TM, d), F32),
        scratch_shapes=[pltpu.VMEM((TOP_K, TM, d), F32), pltpu.SemaphoreType.DMA(())],
        compiler_params=_cparams(("arbitrary", "arbitrary")),
        name="moe_combine",
    )(pos, y_disp, gates, x, mod, final_g.reshape(1, d))


def _moe(h2, idx_t, gates_t, x, mod, w1, b1, w2, b2, final_g, n_lat_tiles, final):
    b, t, d = x.shape
    n_tok = b * t
    n_assign = n_tok * TOP_K
    flat_e = idx_t.reshape(-1)
    order = jnp.argsort(flat_e)
    sorted_e = flat_e[order]
    counts = jnp.bincount(flat_e, length=N_EXPERTS)
    padded = (counts + EXPERT_BLOCK - 1) // EXPERT_BLOCK * EXPERT_BLOCK
    pad_end = jnp.cumsum(padded)
    pad_start = pad_end - padded
    start = jnp.cumsum(counts) - counts
    dest = (pad_start[sorted_e] + (jnp.arange(n_assign) - start[sorted_e])).astype(jnp.int32)
    n_blocks = -(-n_assign // EXPERT_BLOCK) + N_EXPERTS
    rows = n_blocks * EXPERT_BLOCK
    row_token = jnp.zeros((rows,), jnp.int32).at[dest].set((order % n_tok).astype(jnp.int32))
    pos = jnp.zeros((n_assign,), jnp.int32).at[order].set(dest)
    block_e = jnp.minimum(jnp.searchsorted(pad_end, jnp.arange(n_blocks) * EXPERT_BLOCK, side='right'),
                          N_EXPERTS - 1).astype(jnp.int32)

    x_disp = _dispatch(h2.reshape(n_tok, d), row_token.reshape(n_blocks, 1, EXPERT_BLOCK))
    y_disp = _expert_ffn(x_disp, block_e, w1, b1, w2, b2)
    pos_tiles = pos.reshape(TOP_K, n_tok // TM, TM).transpose(1, 0, 2)
    return _combine(y_disp, pos_tiles, gates_t.T, x, mod, final_g, n_lat_tiles, final)


def _rope_tables(n_lat, n_ctx):
    rows = n_lat // GRID_W
    row = jnp.repeat(jnp.arange(rows, dtype=F32), GRID_W)
    col = jnp.tile(jnp.arange(GRID_W, dtype=F32), rows)
    axis_dim = HEAD_DIM // 2
    inv = ROPE_THETA ** (-jnp.arange(0, axis_dim, 2, dtype=F32) / axis_dim)
    ang_r = row[:, None] * inv
    ang_c = col[:, None] * inv
    ang = jnp.concatenate([ang_r, ang_r, ang_c, ang_c], axis=-1)
    cos = jnp.concatenate([jnp.cos(ang), jnp.ones((n_ctx, HEAD_DIM), F32)], axis=0)
    sin = jnp.concatenate([jnp.sin(ang), jnp.zeros((n_ctx, HEAD_DIM), F32)], axis=0)
    even = (jnp.arange(HEAD_DIM) // 32) % 2 == 0
    return cos, jnp.where(even, -sin, 0.0), jnp.where(even, 0.0, sin)


def kernel(x, c, ctx, c_ctx, ada_w, ada_b, norm1_g, norm2_g, mix_w_out, router_w, router_b, exp_w1, exp_b1, exp_w2, exp_b2, attn_w_in, attn_q_gain, attn_k_gain, gmlp_w_in, gmlp_v_gain, gmlp_w_s, gmlp_b_s, hyena_w_in, hyena_conv_w, hyena_conv_b, hyena_f_w1, hyena_f_b1, hyena_f_w2, hyena_f_b2, hyena_f_w3, hyena_f_b3, hyena_f_w4, hyena_f_b4, hyena_freq, hyena_d, final_g):
    b, s, d = x.shape
    n_ctx = ctx.shape[1]
    depth = ada_w.shape[0]
    assert s % TM == 0 and n_ctx == TM and b < MOD_ROWS and s % (FFT_N2 * 8) == 0
    n_lat_tiles = s // TM

    xs = jnp.concatenate([x, ctx], axis=1)
    cc = jnp.concatenate([c, c_ctx[None, :], jnp.zeros((MOD_ROWS - b - 1, d), F32)], axis=0)
    mods = _modulation(cc, ada_w, ada_b).reshape(depth, MOD_ROWS, 1, 6 * d)
    rope = _rope_tables(s, n_ctx)

    for i in range(depth):
        kind, j = i % 3, i // 3
        last = i == depth - 1
        mod = mods[i]
        w_out = mix_w_out[i].astype(BF16)
        if kind == 0:
            q, k, v = _attn_proj(xs, mod, norm1_g[i], attn_w_in[j].astype(BF16), attn_q_gain[j], attn_k_gain[j],
                                 rope, n_lat_tiles)
            y = _attention(q, k, v, s)
        elif kind == 2:
            p = _inproj(xs, mod, norm1_g[i], hyena_w_in[j].astype(BF16), n_lat_tiles)
            fargs = (hyena_f_w1[j], hyena_f_b1[j], hyena_f_w2[j], hyena_f_b2[j], hyena_f_w3[j], hyena_f_b3[j],
                     hyena_f_w4[j], hyena_f_b4[j], hyena_freq[j])
            tables = _fft_tables(s)
            kf = _filter_spectrum(*_hyena_filters(s, *fargs), tables)
            y_lat = _hyena_conv(p, hyena_conv_w[j], hyena_conv_b[j], hyena_d[j], kf, tables, s)
            dfwd, dinv = _dense_dft_tables(n_ctx)
            kf_c = _filter_spectrum_dense(*_hyena_filters(n_ctx, *fargs), dfwd)
            y_ctx = _hyena_ctx_conv(p, hyena_conv_w[j], hyena_conv_b[j], hyena_d[j], kf_c, dfwd, dinv, s, n_ctx)
            y = jnp.concatenate([y_lat, y_ctx], axis=1)
        if kind == 1:
            xs, h2, idx_t, gates_t = _gmlp_layer(
                xs, mod, norm1_g[i], gmlp_w_in[j].astype(BF16), gmlp_v_gain[j], gmlp_w_s[j].astype(BF16),
                gmlp_b_s[j], w_out, norm2_g[i], router_w[i], router_b[i], n_lat_tiles)
        else:
            xs, h2, idx_t, gates_t = _post(y, xs, mod, w_out, norm2_g[i], router_w[i], router_b[i], n_lat_tiles)
        xs = _moe(h2, idx_t, gates_t, xs, mod, exp_w1[i].astype(BF16), exp_b1[i], exp_w2[i].astype(BF16), exp_b2[i],
                  final_g, n_lat_tiles, last)
    return xs
```

```python
import functools
import math

import jax
import jax.numpy as jnp
import numpy as np
from jax import lax
from jax.experimental import pallas as pl
from jax.experimental.pallas import tpu as pltpu

F32 = jnp.float32
BF16 = jnp.bfloat16
HIGHEST = lax.Precision.HIGHEST

EPS = 1e-6
GRID_W = 64
ROPE_THETA = 10000.0
HEAD_DIM = 128
N_KV_HEADS = 2
Q_GROUP = 4
CHUNK = 128
GMLP_GROUPS = 8
N_EXPERTS = 32
TOP_K = 4
SWIGLU_ALPHA = 1.702
SWIGLU_LIMIT = 7.0
N_BANDS = 16
DECAY_TARGET = 1e-2
MAX_DECAY = math.log(DECAY_TARGET) / 0.3
MIN_DECAY = math.log(DECAY_TARGET) / 1.5

LANES = 128
TM = 256
ATTN_TK = 512
MOD_ROWS = 16
EXPERT_BLOCK = 256
FFT_N2 = 128
FFT_CB = 128
FFT_UNROLL = 4
VMEM_LIMIT = 56 * 1024 * 1024


def _cparams(sem):
    return pltpu.CompilerParams(dimension_semantics=sem, vmem_limit_bytes=VMEM_LIMIT)


def _norm_mod(x, g, scale, shift):
    y = x * lax.rsqrt(jnp.mean(x * x, axis=-1, keepdims=True) + EPS)
    return y * g * (1.0 + scale) + shift


def _resident(shape, index_map):
    return pl.BlockSpec(shape, index_map, pipeline_mode=pl.Buffered(1))


def _mod_kernel(c_ref, w_ref, b_ref, o_ref):
    c = c_ref[...]
    s = c * jax.nn.sigmoid(c)
    o_ref[...] = jnp.dot(s, w_ref[...], precision=HIGHEST, preferred_element_type=F32) + b_ref[...]


def _modulation(cc, ada_w, ada_b):
    depth, d, d6 = ada_w.shape
    tn = 1536
    return pl.pallas_call(
        _mod_kernel,
        grid=(depth, d6 // tn),
        in_specs=[
            pl.BlockSpec((MOD_ROWS, d), lambda l, j: (0, 0)),
            pl.BlockSpec((None, d, tn), lambda l, j: (l, 0, j)),
            pl.BlockSpec((None, 1, tn), lambda l, j: (l, 0, j)),
        ],
        out_specs=pl.BlockSpec((None, MOD_ROWS, tn), lambda l, j: (l, 0, j)),
        out_shape=jax.ShapeDtypeStruct((depth, MOD_ROWS, d6), F32),
        compiler_params=_cparams(("arbitrary", "arbitrary")),
        name="adaln_mod",
    )(cc, ada_w, ada_b.reshape(depth, 1, d6))


def _inproj_kernel(x_ref, mod_ref, g_ref, w_ref, o_ref, *, d):
    mod = mod_ref[...]
    h = _norm_mod(x_ref[...], g_ref[...], mod[:, d:2 * d], mod[:, 0:d]).astype(BF16)
    o_ref[...] = jnp.dot(h, w_ref[...], preferred_element_type=F32).astype(o_ref.dtype)


def _inproj(x, mod, g, w, n_lat_tiles):
    b, t, d = x.shape
    n_out = w.shape[1]
    return pl.pallas_call(
        functools.partial(_inproj_kernel, d=d),
        grid=(b, t // TM),
        in_specs=[
            pl.BlockSpec((None, TM, d), lambda i, j: (i, j, 0)),
            pl.BlockSpec((None, 1, 6 * d), lambda i, j: (jnp.where(j >= n_lat_tiles, b, i), 0, 0)),
            pl.BlockSpec((1, d), lambda i, j: (0, 0)),
            pl.BlockSpec((d, n_out), lambda i, j: (0, 0)),
        ],
        out_specs=pl.BlockSpec((None, TM, n_out), lambda i, j: (i, j, 0)),
        out_shape=jax.ShapeDtypeStruct((b, t, n_out), BF16),
        compiler_params=_cparams(("parallel", "parallel")),
        name="inproj",
    )(x, mod, g.reshape(1, d), w)


def _attn_proj_kernel(x_ref, mod_ref, g_ref, w_ref, qg_ref, kg_ref, cos_ref, sa_ref, sb_ref,
                      q_ref, k_ref, v_ref, *, d):
    mod = mod_ref[...]
    h = _norm_mod(x_ref[...], g_ref[...], mod[:, d:2 * d], mod[:, 0:d]).astype(BF16)
    p = jnp.dot(h, w_ref[...], preferred_element_type=F32)
    cos, sin_a, sin_b = cos_ref[...], sa_ref[...], sb_ref[...]
    n_q = Q_GROUP * N_KV_HEADS

    def head(ph, gain):
        y = ph * lax.rsqrt(jnp.mean(ph * ph, axis=-1, keepdims=True) + EPS) * gain
        return y * cos + pltpu.roll(y, LANES - 32, 1) * sin_a + pltpu.roll(y, 32, 1) * sin_b

    qg = qg_ref[...] * (HEAD_DIM ** -0.5 * math.log2(math.e))
    kg = kg_ref[...]
    for i in range(n_q):
        q_ref[:, i * HEAD_DIM:(i + 1) * HEAD_DIM] = head(p[:, i * HEAD_DIM:(i + 1) * HEAD_DIM], qg).astype(BF16)
    k0 = n_q * HEAD_DIM
    for i in range(N_KV_HEADS):
        k_ref[:, i * HEAD_DIM:(i + 1) * HEAD_DIM] = head(
            p[:, k0 + i * HEAD_DIM:k0 + (i + 1) * HEAD_DIM], kg).astype(BF16)
    v0 = k0 + N_KV_HEADS * HEAD_DIM
    for i in range(N_KV_HEADS):
        v_ref[:, 2 * i * HEAD_DIM:(2 * i + 1) * HEAD_DIM] = p[:, v0 + i * HEAD_DIM:v0 + (i + 1) * HEAD_DIM].astype(BF16)
        v_ref[:, (2 * i + 1) * HEAD_DIM:(2 * i + 2) * HEAD_DIM] = jnp.ones((p.shape[0], HEAD_DIM), BF16)


def _attn_proj(x, mod, g, w, q_gain, k_gain, rope, n_lat_tiles):
    b, t, d = x.shape
    n_out = w.shape[1]
    qw = Q_GROUP * N_KV_HEADS * HEAD_DIM
    kw = N_KV_HEADS * HEAD_DIM
    row = lambda i, j: (i, j, 0)
    tab = pl.BlockSpec((TM, HEAD_DIM), lambda i, j: (j, 0))
    return pl.pallas_call(
        functools.partial(_attn_proj_kernel, d=d),
        grid=(b, t // TM),
        in_specs=[
            pl.BlockSpec((None, TM, d), row),
            pl.BlockSpec((None, 1, 6 * d), lambda i, j: (jnp.where(j >= n_lat_tiles, b, i), 0, 0)),
            pl.BlockSpec((1, d), lambda i, j: (0, 0)),
            pl.BlockSpec((d, n_out), lambda i, j: (0, 0)),
            pl.BlockSpec((1, HEAD_DIM), lambda i, j: (0, 0)),
            pl.BlockSpec((1, HEAD_DIM), lambda i, j: (0, 0)),
            tab, tab, tab,
        ],
        out_specs=[
            pl.BlockSpec((None, TM, qw), row),
            pl.BlockSpec((None, TM, kw), row),
            pl.BlockSpec((None, TM, 2 * kw), row),
        ],
        out_shape=[
            jax.ShapeDtypeStruct((b, t, qw), BF16),
            jax.ShapeDtypeStruct((b, t, kw), BF16),
            jax.ShapeDtypeStruct((b, t, 2 * kw), BF16),
        ],
        compiler_params=_cparams(("parallel", "parallel")),
        name="attn_proj",
    )(x, mod, g.reshape(1, d), w, q_gain.reshape(1, HEAD_DIM), k_gain.reshape(1, HEAD_DIM), *rope)


def _attn_kernel(q_ref, k_ref, v_ref, o_ref, q_scr, s0, s1, p0, p1, a0, a1, m_scr, acc_scr, *, n_lat, n_ctx, tk):
    is_ctx = pl.program_id(2) * TM >= n_lat
    s_b, p_b, a_b = (s0, s1), (p0, p1), (a0, a1)
    for g in range(Q_GROUP):
        q_scr[g * TM:(g + 1) * TM, :] = q_ref[:, g * HEAD_DIM:(g + 1) * HEAD_DIM]
    m_scr[...] = jnp.full(m_scr.shape, -jnp.inf, F32)
    acc_scr[...] = jnp.zeros(acc_scr.shape, F32)

    def scores(start, size, slot):
        k = k_ref[pl.ds(start, size), :]
        s_b[slot][:, :size] = lax.dot_general(q_scr[...], k, (((1,), (1,)), ((), ())), preferred_element_type=F32)

    def softmax(size, slot):
        nb = size // LANES
        s = s_b[slot][:, :size]
        mx = s[:, 0:LANES]
        for j in range(1, nb):
            mx = jnp.maximum(mx, s[:, j * LANES:(j + 1) * LANES])
        m_prev = m_scr[...]
        m_new = jnp.maximum(m_prev, jnp.max(mx, axis=-1, keepdims=True))
        a_b[slot][...] = jnp.exp2(m_prev - m_new)
        m_scr[...] = m_new
        m_rep = jnp.concatenate([m_new] * nb, axis=1) if nb > 1 else m_new
        p_b[slot][:, :size] = jnp.exp2(s - m_rep).astype(BF16)

    def values(start, size, slot):
        a = a_b[slot][...]
        acc_scr[...] = jnp.concatenate([a, a], axis=1) * acc_scr[...] + jnp.dot(
            p_b[slot][:, :size], v_ref[pl.ds(start, size), :], preferred_element_type=F32)

    scores(n_lat, n_ctx, 0)
    softmax(n_ctx, 0)
    values(n_lat, n_ctx, 0)

    n = n_lat // tk

    @pl.when(jnp.logical_not(is_ctx))
    def _():
        scores(0, tk, 0)
        softmax(tk, 0)
        scores(tk, tk, 1)

        def pair(i, carry):
            j = 2 + 2 * i
            values(pl.multiple_of((j - 2) * tk, tk), tk, 0)
            softmax(tk, 1)
            scores(pl.multiple_of(j * tk, tk), tk, 0)
            values(pl.multiple_of((j - 1) * tk, tk), tk, 1)
            softmax(tk, 0)
            scores(pl.multiple_of((j + 1) * tk, tk), tk, 1)
            return carry

        lax.fori_loop(0, (n - 2) // 2, pair, 0)
        values((n - 2) * tk, tk, 0)
        softmax(tk, 1)
        values((n - 1) * tk, tk, 1)

    out = acc_scr[:, :HEAD_DIM] / acc_scr[:, HEAD_DIM:]
    for g in range(Q_GROUP):
        o_ref[:, g * HEAD_DIM:(g + 1) * HEAD_DIM] = out[g * TM:(g + 1) * TM].astype(o_ref.dtype)


def _attention(q, k, v, n_lat):
    b, t, _ = q.shape
    n_ctx = t - n_lat
    gw = Q_GROUP * HEAD_DIM
    tk = ATTN_TK
    assert n_lat % (2 * tk) == 0 and n_ctx <= tk and n_ctx % LANES == 0
    m = Q_GROUP * TM
    return pl.pallas_call(
        functools.partial(_attn_kernel, n_lat=n_lat, n_ctx=n_ctx, tk=tk),
        grid=(b, N_KV_HEADS, t // TM),
        in_specs=[
            pl.BlockSpec((None, TM, gw), lambda i, h, j: (i, j, h)),
            pl.BlockSpec((None, t, HEAD_DIM), lambda i, h, j: (i, 0, h)),
            pl.BlockSpec((None, t, 2 * HEAD_DIM), lambda i, h, j: (i, 0, h)),
        ],
        out_specs=pl.BlockSpec((None, TM, gw), lambda i, h, j: (i, j, h)),
        out_shape=jax.ShapeDtypeStruct(q.shape, BF16),
        scratch_shapes=[
            pltpu.VMEM((m, HEAD_DIM), BF16),
            pltpu.VMEM((m, tk), F32), pltpu.VMEM((m, tk), F32),
            pltpu.VMEM((m, tk), BF16), pltpu.VMEM((m, tk), BF16),
            pltpu.VMEM((m, LANES), F32), pltpu.VMEM((m, LANES), F32),
            pltpu.VMEM((m, LANES), F32),
            pltpu.VMEM((m, 2 * HEAD_DIM), F32),
        ],
        compiler_params=_cparams(("parallel", "parallel", "parallel")),
        name="attention",
    )(q, k, v)


def _pack_bf16_pairs(a):
    n = a.shape[1] // 2
    bits = pltpu.bitcast(a.astype(BF16).astype(F32), jnp.uint32)
    return (bits[:, :n] >> 16) | (bits[:, n:] & jnp.uint32(0xFFFF0000))


def _unpack_bf16_pairs(w):
    lo = pltpu.bitcast(w << 16, F32)
    hi = pltpu.bitcast(w & jnp.uint32(0xFFFF0000), F32)
    return jnp.concatenate([lo, hi], axis=1)


def _residual_router(y, x, mod, w_out, g2n, rwt, rb, outs, cnt_scr, *, d):
    xo_ref, h2_ref, idx_ref, gate_ref, rank_ref, cnt_ref = outs
    first = jnp.logical_and(pl.program_id(0) == 0, pl.program_id(1) == 0)

    @pl.when(first)
    def _():
        cnt_scr[...] = jnp.zeros(cnt_scr.shape, F32)

    xn = x + mod[:, 2 * d:3 * d] * jnp.dot(y, w_out, preferred_element_type=F32)
    xo_ref[...] = xn
    h2 = _norm_mod(xn, g2n, mod[:, 4 * d:5 * d], mod[:, 3 * d:4 * d])
    h2_ref[...] = _pack_bf16_pairs(h2)
    lg = lax.dot_general(rwt, h2, (((1,), (1,)), ((), ())), precision=HIGHEST,
                         preferred_element_type=F32) + rb
    row = lax.broadcasted_iota(jnp.int32, lg.shape, 0)
    vals, idxs = [], []
    for _ in range(TOP_K):
        m = jnp.max(lg, axis=0, keepdims=True)
        i = jnp.min(jnp.where(lg == m, row, N_EXPERTS), axis=0, keepdims=True)
        vals.append(m)
        idxs.append(i)
        lg = jnp.where(row == i, -jnp.inf, lg)
    es = [jnp.exp(vv - vals[0]) for vv in vals]
    tot = es[0] + es[1] + es[2] + es[3]
    idx_ref[...] = jnp.concatenate(idxs, axis=0)
    gate_ref[...] = jnp.concatenate([e / tot for e in es], axis=0)

    tm = lg.shape[1]
    earlier = (lax.broadcasted_iota(jnp.int32, (tm, tm), 0) < lax.broadcasted_iota(jnp.int32, (tm, tm), 1)).astype(BF16)
    run = cnt_scr[...]
    ranks = []
    for i in idxs:
        hit = row == i
        before = jnp.dot(hit.astype(BF16), earlier, preferred_element_type=F32)
        ranks.append(jnp.sum(jnp.where(hit, run + before, 0.0), axis=0, keepdims=True))
        run = run + jnp.sum(hit.astype(F32), axis=1, keepdims=True)
    cnt_scr[...] = run
    rank_ref[...] = jnp.concatenate(ranks, axis=0).astype(jnp.int32)
    cnt_ref[...] = run.astype(jnp.int32)


def _post_kernel(y_ref, x_ref, mod_ref, w_ref, g_ref, rwt_ref, rb_ref, *rest, d):
    _residual_router(y_ref[...], x_ref[...], mod_ref[...], w_ref[...], g_ref[...], rwt_ref[...], rb_ref[...],
                     rest[:-1], rest[-1], d=d)


def _post_specs(b, t, d, n_lat_tiles):
    nt = t // TM
    row = lambda i, j: (i, j, 0)
    const2 = lambda i, j: (0, 0)
    in_tail = [
        pl.BlockSpec((d, d), const2),
        pl.BlockSpec((1, d), const2),
        pl.BlockSpec((N_EXPERTS, d), const2),
        pl.BlockSpec((N_EXPERTS, 1), const2),
    ]
    per_token = pl.BlockSpec((TOP_K, TM), lambda i, j: (0, i * nt + j))
    out_specs = [
        pl.BlockSpec((None, TM, d), row),
        pl.BlockSpec((None, TM, d // 2), row),
        per_token, per_token, per_token,
        pl.BlockSpec((N_EXPERTS, 1), const2),
    ]
    out_shape = [
        jax.ShapeDtypeStruct((b, t, d), F32),
        jax.ShapeDtypeStruct((b, t, d // 2), jnp.uint32),
        jax.ShapeDtypeStruct((TOP_K, b * t), jnp.int32),
        jax.ShapeDtypeStruct((TOP_K, b * t), F32),
        jax.ShapeDtypeStruct((TOP_K, b * t), jnp.int32),
        jax.ShapeDtypeStruct((N_EXPERTS, 1), jnp.int32),
    ]
    mod_spec = pl.BlockSpec((None, 1, 6 * d), lambda i, j: (jnp.where(j >= n_lat_tiles, b, i), 0, 0))
    return mod_spec, in_tail, out_specs, out_shape


def _post(y, x, mod, w_out, g2n, router_w, router_b, n_lat_tiles):
    b, t, d = x.shape
    mod_spec, in_tail, out_specs, out_shape = _post_specs(b, t, d, n_lat_tiles)
    row = lambda i, j: (i, j, 0)
    return pl.pallas_call(
        functools.partial(_post_kernel, d=d),
        grid=(b, t // TM),
        in_specs=[pl.BlockSpec((None, TM, d), row), pl.BlockSpec((None, TM, d), row), mod_spec] + in_tail,
        out_specs=out_specs,
        out_shape=out_shape,
        scratch_shapes=[pltpu.VMEM((N_EXPERTS, 1), F32)],
        compiler_params=_cparams(("arbitrary", "arbitrary")),
        name="outproj_router",
    )(y, x, mod, w_out, g2n.reshape(1, d), router_w.T, router_b.reshape(N_EXPERTS, 1))


def _gmlp_kernel(x_ref, mod_ref, g1_ref, win_ref, vg_ref, ws_ref, bs_ref, w_ref, g_ref, rwt_ref, rb_ref,
                 *rest, d):
    outs, cnt_scr, y_scr = rest[:-2], rest[-2], rest[-1]
    mod = mod_ref[...]
    x = x_ref[...]
    h = _norm_mod(x, g1_ref[...], mod[:, d:2 * d], mod[:, 0:d]).astype(BF16)
    z = jnp.dot(h, win_ref[...], preferred_element_type=F32)
    z = 0.5 * z * (1.0 + lax.erf(z * (2.0 ** -0.5)))
    width = z.shape[1] // 2
    u, v = z[:, :width], z[:, width:]
    v = (v * lax.rsqrt(jnp.mean(v * v, axis=-1, keepdims=True) + EPS) * vg_ref[...]).astype(BF16)
    gd = width // GMLP_GROUPS
    for n in range(TM // CHUNK):
        r = slice(n * CHUNK, (n + 1) * CHUNK)
        for g in range(GMLP_GROUPS):
            cs = slice(g * gd, (g + 1) * gd)
            mixed = jnp.dot(ws_ref[g], v[r, cs], preferred_element_type=F32) + bs_ref[:, g:g + 1]
            y_scr[r, cs] = (u[r, cs] * mixed).astype(BF16)
    _residual_router(y_scr[...], x, mod, w_ref[...], g_ref[...], rwt_ref[...], rb_ref[...], outs, cnt_scr, d=d)


def _gmlp_layer(x, mod, g1n, w_in, v_gain, w_s, b_s, w_out, g2n, router_w, router_b, n_lat_tiles):
    b, t, d = x.shape
    width = w_in.shape[1] // 2
    mod_spec, in_tail, out_specs, out_shape = _post_specs(b, t, d, n_lat_tiles)
    row = lambda i, j: (i, j, 0)
    const2 = lambda i, j: (0, 0)
    return pl.pallas_call(
        functools.partial(_gmlp_kernel, d=d),
        grid=(b, t // TM),
        in_specs=[
            pl.BlockSpec((None, TM, d), row),
            mod_spec,
            pl.BlockSpec((1, d), const2),
            pl.BlockSpec((d, 2 * width), const2),
            pl.BlockSpec((1, width), const2),
            pl.BlockSpec((GMLP_GROUPS, CHUNK, CHUNK), lambda i, j: (0, 0, 0)),
            pl.BlockSpec((CHUNK, GMLP_GROUPS), const2),
        ] + in_tail,
        out_specs=out_specs,
        out_shape=out_shape,
        scratch_shapes=[pltpu.VMEM((N_EXPERTS, 1), F32), pltpu.VMEM((TM, width), BF16)],
        compiler_params=_cparams(("arbitrary", "arbitrary")),
        name="gmlp_layer",
    )(x, mod, g1n.reshape(1, d), w_in, v_gain.reshape(1, width), w_s, b_s.T, w_out, g2n.reshape(1, d),
      router_w.T, router_b.reshape(N_EXPERTS, 1))


def _filter_mlp_kernel(z_ref, t_ref, w1, b1, w2, b2, w3, b3, w4, b4, fr, dl, kf_ref, kb_ref, *, width, tl):
    dot = functools.partial(jnp.dot, precision=HIGHEST, preferred_element_type=F32)
    f = fr[...]
    a = jnp.sin(f * (dot(z_ref[...], w1[...]) + b1[...]))
    a = jnp.sin(f * (dot(a, w2[...]) + b2[...]))
    a = jnp.sin(f * (dot(a, w3[...]) + b3[...]))
    k = dot(a, w4[...]) + b4[...]
    window = jnp.exp(-t_ref[...] * dl[...])
    kf_ref[...] = k[:, :width] * window
    pos = pl.program_id(0) * tl + lax.broadcasted_iota(jnp.int32, (tl, 1), 0)
    kb_ref[...] = jnp.where(pos == 0, 0.0, k[:, width:] * window)


def _hyena_filters(length, f_w1, f_b1, f_w2, f_b2, f_w3, f_b3, f_w4, f_b4, freq):
    width = f_w4.shape[1] // 2
    hid = f_w1.shape[1]
    emb = 2 * N_BANDS + 1
    t = jnp.linspace(0.0, 1.0, length, dtype=F32)[:, None]
    w = 2.0 * math.pi * jnp.arange(length, dtype=F32)[:, None] / length
    f = jnp.linspace(1e-4, N_BANDS - 1, N_BANDS, dtype=F32)[None, :]
    z = jnp.concatenate([t, jnp.cos(f * w), -jnp.sin(f * w), jnp.zeros((length, hid - emb), F32)], axis=-1)
    w1p = jnp.concatenate([f_w1, jnp.zeros((hid - emb, hid), F32)], axis=0)
    deltas = jnp.abs(jnp.linspace(MIN_DECAY, MAX_DECAY, width, dtype=F32))[None, :]
    tl = min(length, 512)
    full = lambda shape: pl.BlockSpec(shape, lambda i: (0, 0))
    return pl.pallas_call(
        functools.partial(_filter_mlp_kernel, width=width, tl=tl),
        grid=(length // tl,),
        in_specs=[
            pl.BlockSpec((tl, hid), lambda i: (i, 0)),
            pl.BlockSpec((tl, 1), lambda i: (i, 0)),
            full((hid, hid)), full((1, hid)), full((hid, hid)), full((1, hid)), full((hid, hid)), full((1, hid)),
            full((hid, 2 * width)), full((1, 2 * width)), full((1, hid)), full((1, width)),
        ],
        out_specs=[pl.BlockSpec((tl, width), lambda i: (i, 0)), pl.BlockSpec((tl, width), lambda i: (i, 0))],
        out_shape=[jax.ShapeDtypeStruct((length, width), F32), jax.ShapeDtypeStruct((length, width), F32)],
        compiler_params=_cparams(("parallel",)),
        name="hyena_filter_mlp",
    )(z, t, w1p, f_b1.reshape(1, hid), f_w2, f_b2.reshape(1, hid), f_w3, f_b3.reshape(1, hid),
      f_w4, f_b4.reshape(1, 2 * width), freq.reshape(1, hid), deltas)


def _fft_plan(length):
    n = 2 * length
    n1 = n // FFT_N2
    nz = n1 // 2
    ku = -(-(nz + 1) // 8) * 8
    return n, n1, nz, ku


def _fft_tables(length):
    n, n1, nz, ku = _fft_plan(length)
    i2 = np.arange(FFT_N2)[:, None, None]
    k1 = np.arange(ku)[None, :, None]
    i1 = np.arange(nz)[None, None, :]
    phi = 2.0 * np.pi * (((FFT_N2 * i1 + i2) * k1) % n) / n
    fwd = np.concatenate([np.cos(phi), -np.sin(phi)], axis=1)
    wgt = np.where((k1 == 0) | (k1 == nz), 1.0, np.where(k1 < nz, 2.0, 0.0)) / n
    inv = np.concatenate([np.cos(phi) * wgt, -np.sin(phi) * wgt], axis=1).transpose(0, 2, 1)
    th = 2.0 * np.pi * ((np.arange(FFT_N2)[:, None] * np.arange(FFT_N2)[None, :]) % FFT_N2) / FFT_N2
    c, s = np.cos(th), np.sin(th)
    f2 = np.block([[c, s], [-s, c]])
    f2i = np.block([[c, -s], [s, c]])
    as_bf = lambda a: jnp.asarray(a, F32).astype(BF16)
    return as_bf(fwd), as_bf(inv), as_bf(f2), as_bf(f2i)


def _fft_stage1(src_ref, fwd_ref, a_ref, *, nz, ku):
    slab = 2 * FFT_N2

    def body(i2, carry):
        rows = src_ref[pl.ds(i2, nz, stride=FFT_N2), :].astype(BF16)
        r = jnp.dot(fwd_ref[i2], rows, preferred_element_type=F32)
        a_ref[pl.ds(i2, ku, stride=slab), :] = r[:ku]
        a_ref[pl.ds(FFT_N2 + i2, ku, stride=slab), :] = r[ku:]
        return carry

    lax.fori_loop(0, FFT_N2, body, 0, unroll=FFT_UNROLL)


def _short_conv(p_ref, w_ref, b_ref, pad_ref, emit, *, length):
    step = min(length, 512)
    cb = p_ref.shape[-1]
    pad_ref[pl.ds(0, 8), :] = jnp.zeros((8, cb), F32)
    pad_ref[pl.ds(length + 8, 8), :] = jnp.zeros((8, cb), F32)
    for j in range(length // step):
        pad_ref[pl.ds(8 + j * step, step), :] = p_ref[pl.ds(j * step, step), :].astype(F32)
    w = w_ref[...]
    for j in range(length // step):
        r0 = j * step
        val = (pad_ref[pl.ds(r0 + 7, step), :] * w[0:1] + pad_ref[pl.ds(r0 + 8, step), :] * w[1:2]
               + pad_ref[pl.ds(r0 + 9, step), :] * w[2:3] + b_ref[...])
        emit(r0, step, val)


def _kf_kernel(kf_ref, kb_ref, fwd_ref, f2_ref, o_ref, a_ref, *, nz, ku):
    slab = 2 * FFT_N2
    for src, sign in ((kf_ref, 1.0), (kb_ref, -1.0)):
        _fft_stage1(src, fwd_ref, a_ref, nz=nz, ku=ku)

        def body(k1, carry, sign=sign, first=(src is kf_ref)):
            a = a_ref[pl.ds(pl.multiple_of(k1 * slab, slab), slab), :].astype(BF16)
            xk = jnp.dot(f2_ref[...], a, preferred_element_type=F32)
            if first:
                o_ref[k1] = xk
            else:
                o_ref[k1, :FFT_N2, :] = o_ref[k1, :FFT_N2, :] + xk[:FFT_N2]
                o_ref[k1, FFT_N2:, :] = o_ref[k1, FFT_N2:, :] - xk[FFT_N2:]
            return carry

        lax.fori_loop(0, ku, body, 0)


def _filter_spectrum(k_fwd, k_bwd, tables):
    length, width = k_fwd.shape
    _, _, nz, ku = _fft_plan(length)
    fwd, _, f2, _ = tables
    blk = pl.BlockSpec((length, FFT_CB), lambda c: (0, c))
    return pl.pallas_call(
        functools.partial(_kf_kernel, nz=nz, ku=ku),
        grid=(width // FFT_CB,),
        in_specs=[
            blk, blk,
            pl.BlockSpec(fwd.shape, lambda c: (0, 0, 0)),
            pl.BlockSpec(f2.shape, lambda c: (0, 0)),
        ],
        out_specs=pl.BlockSpec((ku, 2 * FFT_N2, FFT_CB), lambda c: (0, 0, c)),
        out_shape=jax.ShapeDtypeStruct((ku, 2 * FFT_N2, width), F32),
        scratch_shapes=[pltpu.VMEM((ku * 2 * FFT_N2, FFT_CB), F32)],
        compiler_params=_cparams(("parallel",)),
        name="hyena_filter_spectrum",
    )(k_fwd, k_bwd, fwd, f2)


def _hyena_conv_kernel(px0_ref, px1_ref, pv_ref, w0_ref, w1_ref, wv_ref, b0_ref, b1_ref, bv_ref, dsk_ref,
                       kf_ref, fwd_ref, inv_ref, f2_ref, f2i_ref, o_ref, a_ref, vv_ref, *, length, nz, ku):
    slab = 2 * FFT_N2

    def set_vv(r0, rows, val):
        vv_ref[pl.ds(r0, rows), :] = val

    def mul_vv(r0, rows, val):
        vv_ref[pl.ds(r0, rows), :] = vv_ref[pl.ds(r0, rows), :] * val

    _short_conv(pv_ref, wv_ref, bv_ref, a_ref, set_vv, length=length)
    _short_conv(px1_ref, w1_ref, b1_ref, a_ref, mul_vv, length=length)
    _fft_stage1(vv_ref, fwd_ref, a_ref, nz=nz, ku=ku)

    def freq_body(k1, carry):
        rows = pl.ds(k1 * slab if isinstance(k1, int) else pl.multiple_of(k1 * slab, slab), slab)
        xk = jnp.dot(f2_ref[...], a_ref[rows, :].astype(BF16), preferred_element_type=F32)
        kf = kf_ref[k1]
        xr, xi = xk[:FFT_N2], xk[FFT_N2:]
        kr, ki = kf[:FFT_N2], kf[FFT_N2:]
        prod = jnp.concatenate([xr * kr - xi * ki, xr * ki + xi * kr], axis=0).astype(BF16)
        a_ref[rows, :] = jnp.dot(f2i_ref[...], prod, preferred_element_type=F32)
        return carry

    lax.fori_loop(0, nz, freq_body, 0, unroll=2)
    freq_body(nz, 0)

    def time_body(i2, carry):
        re = a_ref[pl.ds(i2, ku, stride=slab), :]
        im = a_ref[pl.ds(FFT_N2 + i2, ku, stride=slab), :]
        q = jnp.concatenate([re, im], axis=0).astype(BF16)
        y = jnp.dot(inv_ref[i2], q, preferred_element_type=F32)
        rows = pl.ds(i2, nz, stride=FFT_N2)
        vv_ref[rows, :] = y + vv_ref[rows, :] * dsk_ref[...]
        return carry

    lax.fori_loop(0, FFT_N2, time_body, 0, unroll=FFT_UNROLL)

    def emit_out(r0, rows, val):
        o_ref[pl.ds(r0, rows), :] = (vv_ref[pl.ds(r0, rows), :] * val).astype(o_ref.dtype)

    _short_conv(px0_ref, w0_ref, b0_ref, a_ref, emit_out, length=length)


def _hyena_conv(p, conv_w, conv_b, d_skip, kf, tables, length):
    b, _, w3 = p.shape
    width = w3 // 3
    ncb = width // FFT_CB
    _, _, nz, ku = _fft_plan(length)
    fwd, inv, f2, f2i = tables
    pblk = lambda part: _resident((None, length, FFT_CB), lambda c, i: (i, 0, part * ncb + c))
    wblk = lambda part: pl.BlockSpec((3, FFT_CB), lambda c, i: (0, part * ncb + c))
    bblk = lambda part: pl.BlockSpec((1, FFT_CB), lambda c, i: (0, part * ncb + c))
    return pl.pallas_call(
        functools.partial(_hyena_conv_kernel, length=length, nz=nz, ku=ku),
        grid=(ncb, b),
        in_specs=[
            pblk(0), pblk(1), pblk(2), wblk(0), wblk(1), wblk(2), bblk(0), bblk(1), bblk(2),
            pl.BlockSpec((1, FFT_CB), lambda c, i: (0, c)),
            _resident((ku, 2 * FFT_N2, FFT_CB), lambda c, i: (0, 0, c)),
            _resident(fwd.shape, lambda c, i: (0, 0, 0)),
            _resident(inv.shape, lambda c, i: (0, 0, 0)),
            _resident(f2.shape, lambda c, i: (0, 0)),
            _resident(f2i.shape, lambda c, i: (0, 0)),
        ],
        out_specs=pl.BlockSpec((None, length, FFT_CB), lambda c, i: (i, 0, c)),
        out_shape=jax.ShapeDtypeStruct((b, length, width), BF16),
        scratch_shapes=[
            pltpu.VMEM((max(ku * 2 * FFT_N2, length + 16), FFT_CB), F32),
            pltpu.VMEM((length, FFT_CB), F32),
        ],
        compiler_params=_cparams(("arbitrary", "arbitrary")),
        name="hyena_long_conv",
    )(p, p, p, conv_w, conv_w, conv_w, conv_b.reshape(1, w3), conv_b.reshape(1, w3), conv_b.reshape(1, w3),
      d_skip.reshape(1, width), kf, fwd, inv, f2, f2i)


def _dense_dft_tables(length):
    n = 2 * length
    th = 2.0 * np.pi * ((np.arange(n)[:, None] * np.arange(length)[None, :]) % n) / n
    fwd = np.concatenate([np.cos(th), -np.sin(th)], axis=0)
    inv = np.concatenate([np.cos(th), -np.sin(th)], axis=0).T / n
    return jnp.asarray(fwd, F32).astype(BF16), jnp.asarray(inv, F32).astype(BF16)


def _kf_dense_kernel(kf_ref, kb_ref, fwd_ref, o_ref, *, n):
    xf = jnp.dot(fwd_ref[...], kf_ref[...].astype(BF16), preferred_element_type=F32)
    xb = jnp.dot(fwd_ref[...], kb_ref[...].astype(BF16), preferred_element_type=F32)
    o_ref[:n, :] = xf[:n] + xb[:n]
    o_ref[n:, :] = xf[n:] - xb[n:]


def _filter_spectrum_dense(k_fwd, k_bwd, fwd):
    length, width = k_fwd.shape
    n = 2 * length
    blk = pl.BlockSpec((length, FFT_CB), lambda c: (0, c))
    return pl.pallas_call(
        functools.partial(_kf_dense_kernel, n=n),
        grid=(width // FFT_CB,),
        in_specs=[blk, blk, pl.BlockSpec(fwd.shape, lambda c: (0, 0))],
        out_specs=pl.BlockSpec((2 * n, FFT_CB), lambda c: (0, c)),
        out_shape=jax.ShapeDtypeStruct((2 * n, width), F32),
        compiler_params=_cparams(("parallel",)),
        name="hyena_ctx_filter_spectrum",
    )(k_fwd, k_bwd, fwd)


def _hyena_ctx_kernel(px0_ref, px1_ref, pv_ref, w0_ref, w1_ref, wv_ref, b0_ref, b1_ref, bv_ref, dsk_ref,
                      kf_ref, fwd_ref, inv_ref, o_ref, pad_ref, vv_ref, *, length):
    n = 2 * length

    def set_vv(r0, rows, val):
        vv_ref[pl.ds(r0, rows), :] = val

    def mul_vv(r0, rows, val):
        vv_ref[pl.ds(r0, rows), :] = vv_ref[pl.ds(r0, rows), :] * val

    _short_conv(pv_ref, wv_ref, bv_ref, pad_ref, set_vv, length=length)
    _short_conv(px1_ref, w1_ref, b1_ref, pad_ref, mul_vv, length=length)
    vv = vv_ref[...]
    xk = jnp.dot(fwd_ref[...], vv.astype(BF16), preferred_element_type=F32)
    kf = kf_ref[...]
    xr, xi, kr, ki = xk[:n], xk[n:], kf[:n], kf[n:]
    prod = jnp.concatenate([xr * kr - xi * ki, xr * ki + xi * kr], axis=0).astype(BF16)
    vv_ref[...] = jnp.dot(inv_ref[...], prod, preferred_element_type=F32) + vv * dsk_ref[...]

    def emit_out(r0, rows, val):
        o_ref[pl.ds(r0, rows), :] = (vv_ref[pl.ds(r0, rows), :] * val).astype(o_ref.dtype)

    _short_conv(px0_ref, w0_ref, b0_ref, pad_ref, emit_out, length=length)


def _hyena_ctx_conv(p, conv_w, conv_b, d_skip, kf, fwd, inv, n_lat, length):
    b, _, w3 = p.shape
    width = w3 // 3
    ncb = width // FFT_CB
    rb = n_lat // length
    pblk = lambda part: pl.BlockSpec((None, length, FFT_CB), lambda c, i: (i, rb, part * ncb + c))
    wblk = lambda part: pl.BlockSpec((3, FFT_CB), lambda c, i: (0, part * ncb + c))
    bblk = lambda part: pl.BlockSpec((1, FFT_CB), lambda c, i: (0, part * ncb + c))
    return pl.pallas_call(
        functools.partial(_hyena_ctx_kernel, length=length),
        grid=(ncb, b),
        in_specs=[
            pblk(0), pblk(1), pblk(2), wblk(0), wblk(1), wblk(2), bblk(0), bblk(1), bblk(2),
            pl.BlockSpec((1, FFT_CB), lambda c, i: (0, c)),
            pl.BlockSpec((4 * length, FFT_CB), lambda c, i: (0, c)),
            pl.BlockSpec(fwd.shape, lambda c, i: (0, 0)),
            pl.BlockSpec(inv.shape, lambda c, i: (0, 0)),
        ],
        out_specs=pl.BlockSpec((None, length, FFT_CB), lambda c, i: (i, 0, c)),
        out_shape=jax.ShapeDtypeStruct((b, length, width), BF16),
        scratch_shapes=[pltpu.VMEM((length + 16, FFT_CB), F32), pltpu.VMEM((length, FFT_CB), F32)],
        compiler_params=_cparams(("arbitrary", "arbitrary")),
        name="hyena_ctx_conv",
    )(p, p, p, conv_w, conv_w, conv_w, conv_b.reshape(1, w3), conv_b.reshape(1, w3), conv_b.reshape(1, w3),
      d_skip.reshape(1, width), kf, fwd, inv)


def _row_copy(src, dst, sem):
    return pltpu.make_async_copy(src, dst, sem)


def _scatter_kernel(cnt_ref, start_ref, dest_ref, h_ref, x_hbm, zero_scr, sem, zsem):
    rows = h_ref.shape[0]

    @pl.when(pl.program_id(0) == 0)
    def _():
        zero_scr[...] = jnp.zeros(zero_scr.shape, zero_scr.dtype)

        def per_expert(e, total):
            cnt = cnt_ref[e]
            n_pad = (EXPERT_BLOCK - cnt % EXPERT_BLOCK) % EXPERT_BLOCK
            base = start_ref[e] + cnt

            def one(r, carry):
                _row_copy(zero_scr.at[pl.ds(0, 1)], x_hbm.at[pl.ds(base + r, 1)], zsem).start()
                return carry

            lax.fori_loop(0, n_pad, one, 0)
            return total + n_pad

        total = lax.fori_loop(0, N_EXPERTS, per_expert, 0)

        def wait_one(r, carry):
            _row_copy(zero_scr.at[pl.ds(0, 1)], x_hbm.at[pl.ds(0, 1)], zsem).wait()
            return carry

        lax.fori_loop(0, total, wait_one, 0)
        first_free = start_ref[N_EXPERTS] // EXPERT_BLOCK
        n_blocks = x_hbm.shape[0] // EXPERT_BLOCK

        def tail(i, carry):
            cp = _row_copy(zero_scr, x_hbm.at[pl.ds(pl.multiple_of(i * EXPERT_BLOCK, EXPERT_BLOCK), EXPERT_BLOCK)],
                           zsem)
            cp.start()
            cp.wait()
            return carry

        lax.fori_loop(first_free, n_blocks, tail, 0)

    def issue(r, carry):
        for k in range(TOP_K):
            _row_copy(h_ref.at[pl.ds(r, 1)], x_hbm.at[pl.ds(dest_ref[k, r], 1)], sem).start()
        return carry

    lax.fori_loop(0, rows, issue, 0, unroll=4)
    for k in range(TOP_K):
        _row_copy(h_ref, x_hbm.at[pl.ds(0, rows)], sem).wait()


def _dispatch(h2p, dest_tiles, counts, pad_start, n_blocks):
    n_tok, half = h2p.shape
    assert TM == EXPERT_BLOCK
    return pl.pallas_call(
        _scatter_kernel,
        grid_spec=pltpu.PrefetchScalarGridSpec(
            num_scalar_prefetch=2,
            grid=(n_tok // TM,),
            in_specs=[
                pl.BlockSpec((None, TOP_K, TM), lambda i, c, s: (i, 0, 0), memory_space=pltpu.SMEM),
                pl.BlockSpec((TM, half), lambda i, c, s: (i, 0)),
            ],
            out_specs=pl.BlockSpec(memory_space=pl.ANY),
            scratch_shapes=[pltpu.VMEM((EXPERT_BLOCK, half), jnp.uint32), pltpu.SemaphoreType.DMA(()),
                            pltpu.SemaphoreType.DMA(())],
        ),
        out_shape=jax.ShapeDtypeStruct((n_blocks * EXPERT_BLOCK, half), jnp.uint32),
        compiler_params=_cparams(("arbitrary",)),
        name="moe_dispatch",
    )(counts, pad_start, dest_tiles, h2p)


def _ffn_kernel(be_ref, x_ref, w1_ref, b1_ref, w2_ref, b2_ref, o_ref, *, d_ff):
    x = _unpack_bf16_pairs(x_ref[...]).astype(BF16)
    hh = jnp.dot(x, w1_ref[...], preferred_element_type=F32) + b1_ref[...]
    glu = jnp.minimum(hh[:, :d_ff], SWIGLU_LIMIT)
    lin = jnp.clip(hh[:, d_ff:], -SWIGLU_LIMIT, SWIGLU_LIMIT)
    act = (glu * jax.nn.sigmoid(SWIGLU_ALPHA * glu) * (lin + 1.0)).astype(BF16)
    o_ref[...] = _pack_bf16_pairs(jnp.dot(act, w2_ref[...], preferred_element_type=F32) + b2_ref[...])


def _expert_ffn(x_disp, block_e, w1, b1, w2, b2):
    rows, half = x_disp.shape
    n_e, d, ff2 = w1.shape
    d_ff = ff2 // 2
    return pl.pallas_call(
        functools.partial(_ffn_kernel, d_ff=d_ff),
        grid_spec=pltpu.PrefetchScalarGridSpec(
            num_scalar_prefetch=1,
            grid=(rows // EXPERT_BLOCK,),
            in_specs=[
                pl.BlockSpec((EXPERT_BLOCK, half), lambda i, be: (i, 0)),
                pl.BlockSpec((None, d, ff2), lambda i, be: (be[i], 0, 0)),
                pl.BlockSpec((None, 1, ff2), lambda i, be: (be[i], 0, 0)),
                pl.BlockSpec((None, d_ff, d), lambda i, be: (be[i], 0, 0)),
                pl.BlockSpec((None, 1, d), lambda i, be: (be[i], 0, 0)),
            ],
            out_specs=pl.BlockSpec((EXPERT_BLOCK, half), lambda i, be: (i, 0)),
        ),
        out_shape=jax.ShapeDtypeStruct((rows, half), jnp.uint32),
        compiler_params=_cparams(("arbitrary",)),
        name="moe_expert_ffn",
    )(block_e, x_disp, w1, b1.reshape(n_e, 1, ff2), w2, b2.reshape(n_e, 1, d))


def _combine_kernel(pos_ref, y_hbm, gate_ref, x_ref, mod_ref, fg_ref, o_ref, buf, sem, *, d, final):
    def issue(r, carry):
        for k in range(TOP_K):
            _row_copy(y_hbm.at[pl.ds(pos_ref[k, r], 1)], buf.at[k, pl.ds(r, 1)], sem).start()
        return carry

    lax.fori_loop(0, TM, issue, 0, unroll=4)
    for k in range(TOP_K):
        _row_copy(y_hbm.at[pl.ds(0, TM)], buf.at[k], sem).wait()
    gate = gate_ref[...]
    out = _unpack_bf16_pairs(buf[0]) * gate[:, 0:1]
    for k in range(1, TOP_K):
        out = out + _unpack_bf16_pairs(buf[k]) * gate[:, k:k + 1]
    xn = x_ref[...] + mod_ref[:, 5 * d:6 * d] * out
    if final:
        xn = xn * lax.rsqrt(jnp.mean(xn * xn, axis=-1, keepdims=True) + EPS) * fg_ref[...]
    o_ref[...] = xn


def _combine(y_disp, pos, gates, x, mod, final_g, n_lat_tiles, final):
    b, t, d = x.shape
    nt = t // TM
    nt_out = n_lat_tiles if final else nt
    return pl.pallas_call(
        functools.partial(_combine_kernel, d=d, final=final),
        grid=(b, nt_out),
        in_specs=[
            pl.BlockSpec((None, TOP_K, TM), lambda i, j: (i * nt + j, 0, 0), memory_space=pltpu.SMEM),
            pl.BlockSpec(memory_space=pl.ANY),
            pl.BlockSpec((TM, TOP_K), lambda i, j: (i * nt + j, 0)),
            pl.BlockSpec((None, TM, d), lambda i, j: (i, j, 0)),
            pl.BlockSpec((None, 1, 6 * d), lambda i, j: (jnp.where(j >= n_lat_tiles, b, i), 0, 0)),
            pl.BlockSpec((1, d), lambda i, j: (0, 0)),
        ],
        out_specs=pl.BlockSpec((None, TM, d), lambda i, j: (i, j, 0)),
        out_shape=jax.ShapeDtypeStruct((b, nt_out * TM, d), F32),
        scratch_shapes=[pltpu.VMEM((TOP_K, TM, d // 2), jnp.uint32), pltpu.SemaphoreType.DMA(())],
        compiler_params=_cparams(("arbitrary", "arbitrary")),
        name="moe_combine",
    )(pos, y_disp, gates, x, mod, final_g.reshape(1, d))


def _moe(h2p, idx_t, gates_t, rank_t, cnt, x, mod, w1, b1, w2, b2, final_g, n_lat_tiles, final):
    b, t, d = x.shape
    n_tok = b * t
    n_assign = n_tok * TOP_K
    counts = cnt[:, 0]
    padded = (counts + EXPERT_BLOCK - 1) // EXPERT_BLOCK * EXPERT_BLOCK
    pad_end = jnp.cumsum(padded).astype(jnp.int32)
    pad_start = pad_end - padded
    experts = jnp.arange(N_EXPERTS, dtype=jnp.int32)
    dest_t = rank_t + jnp.sum(jnp.where(idx_t[..., None] == experts, pad_start, 0), axis=-1)
    n_blocks = -(-n_assign // EXPERT_BLOCK) + N_EXPERTS
    block_start = jnp.arange(n_blocks, dtype=jnp.int32) * EXPERT_BLOCK
    block_e = jnp.minimum(jnp.sum(pad_end[None, :] <= block_start[:, None], axis=1), N_EXPERTS - 1).astype(jnp.int32)
    dest_tiles = dest_t.reshape(TOP_K, n_tok // TM, TM).transpose(1, 0, 2)

    x_disp = _dispatch(h2p.reshape(n_tok, d // 2), dest_tiles, counts,
                       jnp.concatenate([pad_start, pad_end[-1:]]), n_blocks)
    y_disp = _expert_ffn(x_disp, block_e, w1, b1, w2, b2)
    return _combine(y_disp, dest_tiles, gates_t.T, x, mod, final_g, n_lat_tiles, final)


def _rope_tables(n_lat, n_ctx):
    rows = n_lat // GRID_W
    row = jnp.repeat(jnp.arange(rows, dtype=F32), GRID_W)
    col = jnp.tile(jnp.arange(GRID_W, dtype=F32), rows)
    axis_dim = HEAD_DIM // 2
    inv = ROPE_THETA ** (-jnp.arange(0, axis_dim, 2, dtype=F32) / axis_dim)
    ang_r = row[:, None] * inv
    ang_c = col[:, None] * inv
    ang = jnp.concatenate([ang_r, ang_r, ang_c, ang_c], axis=-1)
    cos = jnp.concatenate([jnp.cos(ang), jnp.ones((n_ctx, HEAD_DIM), F32)], axis=0)
    sin = jnp.concatenate([jnp.sin(ang), jnp.zeros((n_ctx, HEAD_DIM), F32)], axis=0)
    even = (jnp.arange(HEAD_DIM) // 32) % 2 == 0
    return cos, jnp.where(even, -sin, 0.0), jnp.where(even, 0.0, sin)


def kernel(x, c, ctx, c_ctx, ada_w, ada_b, norm1_g, norm2_g, mix_w_out, router_w, router_b, exp_w1, exp_b1, exp_w2, exp_b2, attn_w_in, attn_q_gain, attn_k_gain, gmlp_w_in, gmlp_v_gain, gmlp_w_s, gmlp_b_s, hyena_w_in, hyena_conv_w, hyena_conv_b, hyena_f_w1, hyena_f_b1, hyena_f_w2, hyena_f_b2, hyena_f_w3, hyena_f_b3, hyena_f_w4, hyena_f_b4, hyena_freq, hyena_d, final_g):
    b, s, d = x.shape
    n_ctx = ctx.shape[1]
    depth = ada_w.shape[0]
    assert s % TM == 0 and n_ctx == TM and b < MOD_ROWS and s % (FFT_N2 * 8) == 0
    n_lat_tiles = s // TM

    xs = jnp.concatenate([x, ctx], axis=1)
    cc = jnp.concatenate([c, c_ctx[None, :], jnp.zeros((MOD_ROWS - b - 1, d), F32)], axis=0)
    mods = _modulation(cc, ada_w, ada_b).reshape(depth, MOD_ROWS, 1, 6 * d)
    rope = _rope_tables(s, n_ctx)

    for i in range(depth):
        kind, j = i % 3, i // 3
        last = i == depth - 1
        mod = mods[i]
        w_out = mix_w_out[i].astype(BF16)
        if kind == 0:
            q, k, v = _attn_proj(xs, mod, norm1_g[i], attn_w_in[j].astype(BF16), attn_q_gain[j], attn_k_gain[j],
                                 rope, n_lat_tiles)
            y = _attention(q, k, v, s)
        elif kind == 2:
            p = _inproj(xs, mod, norm1_g[i], hyena_w_in[j].astype(BF16), n_lat_tiles)
            fargs = (hyena_f_w1[j], hyena_f_b1[j], hyena_f_w2[j], hyena_f_b2[j], hyena_f_w3[j], hyena_f_b3[j],
                     hyena_f_w4[j], hyena_f_b4[j], hyena_freq[j])
            tables = _fft_tables(s)
            kf = _filter_spectrum(*_hyena_filters(s, *fargs), tables)
            y_lat = _hyena_conv(p, hyena_conv_w[j], hyena_conv_b[j], hyena_d[j], kf, tables, s)
            dfwd, dinv = _dense_dft_tables(n_ctx)
            kf_c = _filter_spectrum_dense(*_hyena_filters(n_ctx, *fargs), dfwd)
            y_ctx = _hyena_ctx_conv(p, hyena_conv_w[j], hyena_conv_b[j], hyena_d[j], kf_c, dfwd, dinv, s, n_ctx)
            y = jnp.concatenate([y_lat, y_ctx], axis=1)
        if kind == 1:
            xs, *routed = _gmlp_layer(
                xs, mod, norm1_g[i], gmlp_w_in[j].astype(BF16), gmlp_v_gain[j], gmlp_w_s[j].astype(BF16),
                gmlp_b_s[j], w_out, norm2_g[i], router_w[i], router_b[i], n_lat_tiles)
        else:
            xs, *routed = _post(y, xs, mod, w_out, norm2_g[i], router_w[i], router_b[i], n_lat_tiles)
        xs = _moe(*routed, xs, mod, exp_w1[i].astype(BF16), exp_b1[i], exp_w2[i].astype(BF16), exp_b2[i],
                  final_g, n_lat_tiles, last)
    return xs
```

```python
import functools
import math

import jax
import jax.numpy as jnp
import numpy as np
from jax import lax
from jax.experimental import pallas as pl
from jax.experimental.pallas import tpu as pltpu

F32 = jnp.float32
BF16 = jnp.bfloat16
HIGHEST = lax.Precision.HIGHEST

EPS = 1e-6
GRID_W = 64
ROPE_THETA = 10000.0
HEAD_DIM = 128
N_KV_HEADS = 2
Q_GROUP = 4
CHUNK = 128
GMLP_GROUPS = 8
N_EXPERTS = 32
TOP_K = 4
SWIGLU_ALPHA = 1.702
SWIGLU_LIMIT = 7.0
N_BANDS = 16
DECAY_TARGET = 1e-2
MAX_DECAY = math.log(DECAY_TARGET) / 0.3
MIN_DECAY = math.log(DECAY_TARGET) / 1.5

LANES = 128
TM = 256
ATTN_TK = 512
MOD_ROWS = 16
EXPERT_BLOCK = 256
FFT_N2 = 128
FFT_CB = 128
FFT_UNROLL = 4
VMEM_LIMIT = 56 * 1024 * 1024


def _cparams(sem):
    return pltpu.CompilerParams(dimension_semantics=sem, vmem_limit_bytes=VMEM_LIMIT)


def _norm_mod(x, g, scale, shift):
    y = x * lax.rsqrt(jnp.mean(x * x, axis=-1, keepdims=True) + EPS)
    return y * g * (1.0 + scale) + shift


def _resident(shape, index_map):
    return pl.BlockSpec(shape, index_map, pipeline_mode=pl.Buffered(1))


def _mod_kernel(c_ref, w_ref, b_ref, o_ref):
    c = c_ref[...]
    s = c * jax.nn.sigmoid(c)
    o_ref[...] = jnp.dot(s, w_ref[...], precision=HIGHEST, preferred_element_type=F32) + b_ref[...]


def _modulation(cc, ada_w, ada_b):
    depth, d, d6 = ada_w.shape
    tn = 1536
    return pl.pallas_call(
        _mod_kernel,
        grid=(depth, d6 // tn),
        in_specs=[
            pl.BlockSpec((MOD_ROWS, d), lambda l, j: (0, 0)),
            pl.BlockSpec((None, d, tn), lambda l, j: (l, 0, j)),
            pl.BlockSpec((None, 1, tn), lambda l, j: (l, 0, j)),
        ],
        out_specs=pl.BlockSpec((None, MOD_ROWS, tn), lambda l, j: (l, 0, j)),
        out_shape=jax.ShapeDtypeStruct((depth, MOD_ROWS, d6), F32),
        compiler_params=_cparams(("arbitrary", "arbitrary")),
        name="adaln_mod",
    )(cc, ada_w, ada_b.reshape(depth, 1, d6))


def _inproj_kernel(x_ref, mod_ref, g_ref, w_ref, o_ref, *, d):
    mod = mod_ref[...]
    h = _norm_mod(x_ref[...], g_ref[...], mod[:, d:2 * d], mod[:, 0:d]).astype(BF16)
    o_ref[...] = jnp.dot(h, w_ref[...], preferred_element_type=F32).astype(o_ref.dtype)


def _inproj(x, mod, g, w, n_lat_tiles):
    b, t, d = x.shape
    n_out = w.shape[1]
    return pl.pallas_call(
        functools.partial(_inproj_kernel, d=d),
        grid=(b, t // TM),
        in_specs=[
            pl.BlockSpec((None, TM, d), lambda i, j: (i, j, 0)),
            pl.BlockSpec((None, 1, 6 * d), lambda i, j: (jnp.where(j >= n_lat_tiles, b, i), 0, 0)),
            pl.BlockSpec((1, d), lambda i, j: (0, 0)),
            pl.BlockSpec((d, n_out), lambda i, j: (0, 0)),
        ],
        out_specs=pl.BlockSpec((None, TM, n_out), lambda i, j: (i, j, 0)),
        out_shape=jax.ShapeDtypeStruct((b, t, n_out), BF16),
        compiler_params=_cparams(("parallel", "parallel")),
        name="inproj",
    )(x, mod, g.reshape(1, d), w)


def _attn_proj_kernel(x_ref, mod_ref, g_ref, w_ref, qg_ref, kg_ref, cos_ref, sa_ref, sb_ref,
                      q_ref, k_ref, v_ref, *, d):
    mod = mod_ref[...]
    h = _norm_mod(x_ref[...], g_ref[...], mod[:, d:2 * d], mod[:, 0:d]).astype(BF16)
    p = jnp.dot(h, w_ref[...], preferred_element_type=F32)
    cos, sin_a, sin_b = cos_ref[...], sa_ref[...], sb_ref[...]
    n_q = Q_GROUP * N_KV_HEADS

    def head(ph, gain):
        y = ph * lax.rsqrt(jnp.mean(ph * ph, axis=-1, keepdims=True) + EPS) * gain
        return y * cos + pltpu.roll(y, LANES - 32, 1) * sin_a + pltpu.roll(y, 32, 1) * sin_b

    qg = qg_ref[...] * (HEAD_DIM ** -0.5 * math.log2(math.e))
    kg = kg_ref[...]
    for i in range(n_q):
        q_ref[:, i * HEAD_DIM:(i + 1) * HEAD_DIM] = head(p[:, i * HEAD_DIM:(i + 1) * HEAD_DIM], qg).astype(BF16)
    k0 = n_q * HEAD_DIM
    for i in range(N_KV_HEADS):
        k_ref[:, i * HEAD_DIM:(i + 1) * HEAD_DIM] = head(
            p[:, k0 + i * HEAD_DIM:k0 + (i + 1) * HEAD_DIM], kg).astype(BF16)
    v0 = k0 + N_KV_HEADS * HEAD_DIM
    for i in range(N_KV_HEADS):
        v_ref[:, 2 * i * HEAD_DIM:(2 * i + 1) * HEAD_DIM] = p[:, v0 + i * HEAD_DIM:v0 + (i + 1) * HEAD_DIM].astype(BF16)
        v_ref[:, (2 * i + 1) * HEAD_DIM:(2 * i + 2) * HEAD_DIM] = jnp.ones((p.shape[0], HEAD_DIM), BF16)


def _attn_proj(x, mod, g, w, q_gain, k_gain, rope, n_lat_tiles):
    b, t, d = x.shape
    n_out = w.shape[1]
    qw = Q_GROUP * N_KV_HEADS * HEAD_DIM
    kw = N_KV_HEADS * HEAD_DIM
    row = lambda i, j: (i, j, 0)
    tab = pl.BlockSpec((TM, HEAD_DIM), lambda i, j: (j, 0))
    return pl.pallas_call(
        functools.partial(_attn_proj_kernel, d=d),
        grid=(b, t // TM),
        in_specs=[
            pl.BlockSpec((None, TM, d), row),
            pl.BlockSpec((None, 1, 6 * d), lambda i, j: (jnp.where(j >= n_lat_tiles, b, i), 0, 0)),
            pl.BlockSpec((1, d), lambda i, j: (0, 0)),
            pl.BlockSpec((d, n_out), lambda i, j: (0, 0)),
            pl.BlockSpec((1, HEAD_DIM), lambda i, j: (0, 0)),
            pl.BlockSpec((1, HEAD_DIM), lambda i, j: (0, 0)),
            tab, tab, tab,
        ],
        out_specs=[
            pl.BlockSpec((None, TM, qw), row),
            pl.BlockSpec((None, TM, kw), row),
            pl.BlockSpec((None, TM, 2 * kw), row),
        ],
        out_shape=[
            jax.ShapeDtypeStruct((b, t, qw), BF16),
            jax.ShapeDtypeStruct((b, t, kw), BF16),
            jax.ShapeDtypeStruct((b, t, 2 * kw), BF16),
        ],
        compiler_params=_cparams(("parallel", "parallel")),
        name="attn_proj",
    )(x, mod, g.reshape(1, d), w, q_gain.reshape(1, HEAD_DIM), k_gain.reshape(1, HEAD_DIM), *rope)


def _attn_kernel(q_ref, k_ref, v_ref, o_ref, q_scr, s0, s1, p0, p1, a0, a1, m_scr, acc_scr, *, n_lat, n_ctx, tk):
    is_ctx = pl.program_id(2) * TM >= n_lat
    s_b, p_b, a_b = (s0, s1), (p0, p1), (a0, a1)
    for g in range(Q_GROUP):
        q_scr[g * TM:(g + 1) * TM, :] = q_ref[:, g * HEAD_DIM:(g + 1) * HEAD_DIM]
    m_scr[...] = jnp.full(m_scr.shape, -jnp.inf, F32)
    acc_scr[...] = jnp.zeros(acc_scr.shape, F32)

    def scores(start, size, slot):
        k = k_ref[pl.ds(start, size), :]
        s_b[slot][:, :size] = lax.dot_general(q_scr[...], k, (((1,), (1,)), ((), ())), preferred_element_type=F32)

    def softmax(size, slot):
        nb = size // LANES
        s = s_b[slot][:, :size]
        mx = s[:, 0:LANES]
        for j in range(1, nb):
            mx = jnp.maximum(mx, s[:, j * LANES:(j + 1) * LANES])
        m_prev = m_scr[...]
        m_new = jnp.maximum(m_prev, jnp.max(mx, axis=-1, keepdims=True))
        a_b[slot][...] = jnp.exp2(m_prev - m_new)
        m_scr[...] = m_new
        m_rep = jnp.concatenate([m_new] * nb, axis=1) if nb > 1 else m_new
        p_b[slot][:, :size] = jnp.exp2(s - m_rep).astype(BF16)

    def values(start, size, slot):
        a = a_b[slot][...]
        acc_scr[...] = jnp.concatenate([a, a], axis=1) * acc_scr[...] + jnp.dot(
            p_b[slot][:, :size], v_ref[pl.ds(start, size), :], preferred_element_type=F32)

    scores(n_lat, n_ctx, 0)
    softmax(n_ctx, 0)
    values(n_lat, n_ctx, 0)

    n = n_lat // tk

    @pl.when(jnp.logical_not(is_ctx))
    def _():
        scores(0, tk, 0)
        softmax(tk, 0)
        scores(tk, tk, 1)

        def pair(i, carry):
            j = 2 + 2 * i
            values(pl.multiple_of((j - 2) * tk, tk), tk, 0)
            softmax(tk, 1)
            scores(pl.multiple_of(j * tk, tk), tk, 0)
            values(pl.multiple_of((j - 1) * tk, tk), tk, 1)
            softmax(tk, 0)
            scores(pl.multiple_of((j + 1) * tk, tk), tk, 1)
            return carry

        lax.fori_loop(0, (n - 2) // 2, pair, 0)
        values((n - 2) * tk, tk, 0)
        softmax(tk, 1)
        values((n - 1) * tk, tk, 1)

    out = acc_scr[:, :HEAD_DIM] / acc_scr[:, HEAD_DIM:]
    for g in range(Q_GROUP):
        o_ref[:, g * HEAD_DIM:(g + 1) * HEAD_DIM] = out[g * TM:(g + 1) * TM].astype(o_ref.dtype)


def _attention(q, k, v, n_lat):
    b, t, _ = q.shape
    n_ctx = t - n_lat
    gw = Q_GROUP * HEAD_DIM
    tk = ATTN_TK
    assert n_lat % (2 * tk) == 0 and n_ctx <= tk and n_ctx % LANES == 0
    m = Q_GROUP * TM
    return pl.pallas_call(
        functools.partial(_attn_kernel, n_lat=n_lat, n_ctx=n_ctx, tk=tk),
        grid=(b, N_KV_HEADS, t // TM),
        in_specs=[
            pl.BlockSpec((None, TM, gw), lambda i, h, j: (i, j, h)),
            pl.BlockSpec((None, t, HEAD_DIM), lambda i, h, j: (i, 0, h)),
            pl.BlockSpec((None, t, 2 * HEAD_DIM), lambda i, h, j: (i, 0, h)),
        ],
        out_specs=pl.BlockSpec((None, TM, gw), lambda i, h, j: (i, j, h)),
        out_shape=jax.ShapeDtypeStruct(q.shape, BF16),
        scratch_shapes=[
            pltpu.VMEM((m, HEAD_DIM), BF16),
            pltpu.VMEM((m, tk), F32), pltpu.VMEM((m, tk), F32),
            pltpu.VMEM((m, tk), BF16), pltpu.VMEM((m, tk), BF16),
            pltpu.VMEM((m, LANES), F32), pltpu.VMEM((m, LANES), F32),
            pltpu.VMEM((m, LANES), F32),
            pltpu.VMEM((m, 2 * HEAD_DIM), F32),
        ],
        compiler_params=_cparams(("parallel", "parallel", "parallel")),
        name="attention",
    )(q, k, v)


def _pack_bf16_pairs(a):
    n = a.shape[1] // 2
    bits = pltpu.bitcast(a.astype(BF16).astype(F32), jnp.uint32)
    return (bits[:, :n] >> 16) | (bits[:, n:] & jnp.uint32(0xFFFF0000))


def _unpack_bf16_pairs(w):
    lo = pltpu.bitcast(w << 16, F32)
    hi = pltpu.bitcast(w & jnp.uint32(0xFFFF0000), F32)
    return jnp.concatenate([lo, hi], axis=1)


def _residual_router(y, x, mod, w_out, g2n, rwt, rb, outs, cnt_scr, *, d):
    xo_ref, h2_ref, idx_ref, gate_ref, rank_ref, cnt_ref = outs
    first = jnp.logical_and(pl.program_id(0) == 0, pl.program_id(1) == 0)

    @pl.when(first)
    def _():
        cnt_scr[...] = jnp.zeros(cnt_scr.shape, F32)

    xn = x + mod[:, 2 * d:3 * d] * jnp.dot(y, w_out, preferred_element_type=F32)
    xo_ref[...] = xn
    h2 = _norm_mod(xn, g2n, mod[:, 4 * d:5 * d], mod[:, 3 * d:4 * d])
    h2_ref[...] = _pack_bf16_pairs(h2)
    lg = lax.dot_general(rwt, h2, (((1,), (1,)), ((), ())), precision=HIGHEST,
                         preferred_element_type=F32) + rb
    row = lax.broadcasted_iota(jnp.int32, lg.shape, 0)
    vals, idxs = [], []
    for _ in range(TOP_K):
        m = jnp.max(lg, axis=0, keepdims=True)
        i = jnp.min(jnp.where(lg == m, row, N_EXPERTS), axis=0, keepdims=True)
        vals.append(m)
        idxs.append(i)
        lg = jnp.where(row == i, -jnp.inf, lg)
    es = [jnp.exp(vv - vals[0]) for vv in vals]
    tot = es[0] + es[1] + es[2] + es[3]
    idx_ref[...] = jnp.concatenate(idxs, axis=0)
    gate_ref[...] = jnp.concatenate([e / tot for e in es], axis=0)

    tm = lg.shape[1]
    earlier = (lax.broadcasted_iota(jnp.int32, (tm, tm), 0) < lax.broadcasted_iota(jnp.int32, (tm, tm), 1)).astype(BF16)
    run = cnt_scr[...]
    ranks = []
    for i in idxs:
        hit = row == i
        before = jnp.dot(hit.astype(BF16), earlier, preferred_element_type=F32)
        ranks.append(jnp.sum(jnp.where(hit, run + before, 0.0), axis=0, keepdims=True))
        run = run + jnp.sum(hit.astype(F32), axis=1, keepdims=True)
    cnt_scr[...] = run
    rank_ref[...] = jnp.concatenate(ranks, axis=0).astype(jnp.int32)
    cnt_ref[...] = run.astype(jnp.int32)


def _post_kernel(y_ref, x_ref, mod_ref, w_ref, g_ref, rwt_ref, rb_ref, *rest, d):
    _residual_router(y_ref[...], x_ref[...], mod_ref[...], w_ref[...], g_ref[...], rwt_ref[...], rb_ref[...],
                     rest[:-1], rest[-1], d=d)


def _post_specs(b, t, d, n_lat_tiles):
    nt = t // TM
    row = lambda i, j: (i, j, 0)
    const2 = lambda i, j: (0, 0)
    in_tail = [
        pl.BlockSpec((d, d), const2),
        pl.BlockSpec((1, d), const2),
        pl.BlockSpec((N_EXPERTS, d), const2),
        pl.BlockSpec((N_EXPERTS, 1), const2),
    ]
    per_token = pl.BlockSpec((TOP_K, TM), lambda i, j: (0, i * nt + j))
    out_specs = [
        pl.BlockSpec((None, TM, d), row),
        pl.BlockSpec((None, TM, d // 2), row),
        per_token, per_token, per_token,
        pl.BlockSpec((N_EXPERTS, 1), const2),
    ]
    out_shape = [
        jax.ShapeDtypeStruct((b, t, d), F32),
        jax.ShapeDtypeStruct((b, t, d // 2), jnp.uint32),
        jax.ShapeDtypeStruct((TOP_K, b * t), jnp.int32),
        jax.ShapeDtypeStruct((TOP_K, b * t), F32),
        jax.ShapeDtypeStruct((TOP_K, b * t), jnp.int32),
        jax.ShapeDtypeStruct((N_EXPERTS, 1), jnp.int32),
    ]
    mod_spec = pl.BlockSpec((None, 1, 6 * d), lambda i, j: (jnp.where(j >= n_lat_tiles, b, i), 0, 0))
    return mod_spec, in_tail, out_specs, out_shape


def _post(y, x, mod, w_out, g2n, router_w, router_b, n_lat_tiles):
    b, t, d = x.shape
    mod_spec, in_tail, out_specs, out_shape = _post_specs(b, t, d, n_lat_tiles)
    row = lambda i, j: (i, j, 0)
    return pl.pallas_call(
        functools.partial(_post_kernel, d=d),
        grid=(b, t // TM),
        in_specs=[pl.BlockSpec((None, TM, d), row), pl.BlockSpec((None, TM, d), row), mod_spec] + in_tail,
        out_specs=out_specs,
        out_shape=out_shape,
        scratch_shapes=[pltpu.VMEM((N_EXPERTS, 1), F32)],
        compiler_params=_cparams(("arbitrary", "arbitrary")),
        name="outproj_router",
    )(y, x, mod, w_out, g2n.reshape(1, d), router_w.T, router_b.reshape(N_EXPERTS, 1))


def _gmlp_kernel(x_ref, mod_ref, g1_ref, win_ref, vg_ref, ws_ref, bs_ref, w_ref, g_ref, rwt_ref, rb_ref,
                 *rest, d):
    outs, cnt_scr, y_scr = rest[:-2], rest[-2], rest[-1]
    mod = mod_ref[...]
    x = x_ref[...]
    h = _norm_mod(x, g1_ref[...], mod[:, d:2 * d], mod[:, 0:d]).astype(BF16)
    z = jnp.dot(h, win_ref[...], preferred_element_type=F32)
    z = 0.5 * z * (1.0 + lax.erf(z * (2.0 ** -0.5)))
    width = z.shape[1] // 2
    u, v = z[:, :width], z[:, width:]
    v = (v * lax.rsqrt(jnp.mean(v * v, axis=-1, keepdims=True) + EPS) * vg_ref[...]).astype(BF16)
    gd = width // GMLP_GROUPS
    for n in range(TM // CHUNK):
        r = slice(n * CHUNK, (n + 1) * CHUNK)
        for g in range(GMLP_GROUPS):
            cs = slice(g * gd, (g + 1) * gd)
            mixed = jnp.dot(ws_ref[g], v[r, cs], preferred_element_type=F32) + bs_ref[:, g:g + 1]
            y_scr[r, cs] = (u[r, cs] * mixed).astype(BF16)
    _residual_router(y_scr[...], x, mod, w_ref[...], g_ref[...], rwt_ref[...], rb_ref[...], outs, cnt_scr, d=d)


def _gmlp_layer(x, mod, g1n, w_in, v_gain, w_s, b_s, w_out, g2n, router_w, router_b, n_lat_tiles):
    b, t, d = x.shape
    width = w_in.shape[1] // 2
    mod_spec, in_tail, out_specs, out_shape = _post_specs(b, t, d, n_lat_tiles)
    row = lambda i, j: (i, j, 0)
    const2 = lambda i, j: (0, 0)
    return pl.pallas_call(
        functools.partial(_gmlp_kernel, d=d),
        grid=(b, t // TM),
        in_specs=[
            pl.BlockSpec((None, TM, d), row),
            mod_spec,
            pl.BlockSpec((1, d), const2),
            pl.BlockSpec((d, 2 * width), const2),
            pl.BlockSpec((1, width), const2),
            pl.BlockSpec((GMLP_GROUPS, CHUNK, CHUNK), lambda i, j: (0, 0, 0)),
            pl.BlockSpec((CHUNK, GMLP_GROUPS), const2),
        ] + in_tail,
        out_specs=out_specs,
        out_shape=out_shape,
        scratch_shapes=[pltpu.VMEM((N_EXPERTS, 1), F32), pltpu.VMEM((TM, width), BF16)],
        compiler_params=_cparams(("arbitrary", "arbitrary")),
        name="gmlp_layer",
    )(x, mod, g1n.reshape(1, d), w_in, v_gain.reshape(1, width), w_s, b_s.T, w_out, g2n.reshape(1, d),
      router_w.T, router_b.reshape(N_EXPERTS, 1))


def _filter_mlp_kernel(z_ref, t_ref, w1, b1, w2, b2, w3, b3, w4, b4, fr, dl, kf_ref, kb_ref, *, width, tl):
    dot = functools.partial(jnp.dot, precision=HIGHEST, preferred_element_type=F32)
    f = fr[...]
    a = jnp.sin(f * (dot(z_ref[...], w1[...]) + b1[...]))
    a = jnp.sin(f * (dot(a, w2[...]) + b2[...]))
    a = jnp.sin(f * (dot(a, w3[...]) + b3[...]))
    k = dot(a, w4[...]) + b4[...]
    window = jnp.exp(-t_ref[...] * dl[...])
    kf_ref[...] = k[:, :width] * window
    pos = pl.program_id(0) * tl + lax.broadcasted_iota(jnp.int32, (tl, 1), 0)
    kb_ref[...] = jnp.where(pos == 0, 0.0, k[:, width:] * window)


def _hyena_filters(length, f_w1, f_b1, f_w2, f_b2, f_w3, f_b3, f_w4, f_b4, freq):
    width = f_w4.shape[1] // 2
    hid = f_w1.shape[1]
    emb = 2 * N_BANDS + 1
    t = jnp.linspace(0.0, 1.0, length, dtype=F32)[:, None]
    w = 2.0 * math.pi * jnp.arange(length, dtype=F32)[:, None] / length
    f = jnp.linspace(1e-4, N_BANDS - 1, N_BANDS, dtype=F32)[None, :]
    z = jnp.concatenate([t, jnp.cos(f * w), -jnp.sin(f * w), jnp.zeros((length, hid - emb), F32)], axis=-1)
    w1p = jnp.concatenate([f_w1, jnp.zeros((hid - emb, hid), F32)], axis=0)
    deltas = jnp.abs(jnp.linspace(MIN_DECAY, MAX_DECAY, width, dtype=F32))[None, :]
    tl = min(length, 512)
    full = lambda shape: pl.BlockSpec(shape, lambda i: (0, 0))
    return pl.pallas_call(
        functools.partial(_filter_mlp_kernel, width=width, tl=tl),
        grid=(length // tl,),
        in_specs=[
            pl.BlockSpec((tl, hid), lambda i: (i, 0)),
            pl.BlockSpec((tl, 1), lambda i: (i, 0)),
            full((hid, hid)), full((1, hid)), full((hid, hid)), full((1, hid)), full((hid, hid)), full((1, hid)),
            full((hid, 2 * width)), full((1, 2 * width)), full((1, hid)), full((1, width)),
        ],
        out_specs=[pl.BlockSpec((tl, width), lambda i: (i, 0)), pl.BlockSpec((tl, width), lambda i: (i, 0))],
        out_shape=[jax.ShapeDtypeStruct((length, width), F32), jax.ShapeDtypeStruct((length, width), F32)],
        compiler_params=_cparams(("parallel",)),
        name="hyena_filter_mlp",
    )(z, t, w1p, f_b1.reshape(1, hid), f_w2, f_b2.reshape(1, hid), f_w3, f_b3.reshape(1, hid),
      f_w4, f_b4.reshape(1, 2 * width), freq.reshape(1, hid), deltas)


def _fft_plan(length):
    n = 2 * length
    n1 = n // FFT_N2
    nz = n1 // 2
    ku = -(-(nz + 1) // 8) * 8
    return n, n1, nz, ku


def _fft_tables(length):
    n, n1, nz, ku = _fft_plan(length)
    i2 = np.arange(FFT_N2)[:, None, None]
    k1 = np.arange(ku)[None, :, None]
    i1 = np.arange(nz)[None, None, :]
    phi = 2.0 * np.pi * (((FFT_N2 * i1 + i2) * k1) % n) / n
    fwd = np.concatenate([np.cos(phi), -np.sin(phi)], axis=1)
    wgt = np.where((k1 == 0) | (k1 == nz), 1.0, np.where(k1 < nz, 2.0, 0.0)) / n
    inv = np.concatenate([np.cos(phi) * wgt, -np.sin(phi) * wgt], axis=1).transpose(0, 2, 1)
    th = 2.0 * np.pi * ((np.arange(FFT_N2)[:, None] * np.arange(FFT_N2)[None, :]) % FFT_N2) / FFT_N2
    c, s = np.cos(th), np.sin(th)
    f2 = np.block([[c, s], [-s, c]])
    f2i = np.block([[c, -s], [s, c]])
    as_bf = lambda a: jnp.asarray(a, F32).astype(BF16)
    return as_bf(fwd), as_bf(inv), as_bf(f2), as_bf(f2i)


def _fft_stage1(src_ref, fwd_ref, a_ref, *, nz, ku):
    slab = 2 * FFT_N2

    def body(i2, carry):
        rows = src_ref[pl.ds(i2, nz, stride=FFT_N2), :].astype(BF16)
        r = jnp.dot(fwd_ref[i2], rows, preferred_element_type=F32)
        a_ref[pl.ds(i2, ku, stride=slab), :] = r[:ku]
        a_ref[pl.ds(FFT_N2 + i2, ku, stride=slab), :] = r[ku:]
        return carry

    lax.fori_loop(0, FFT_N2, body, 0, unroll=FFT_UNROLL)


def _short_conv(p_ref, w_ref, b_ref, pad_ref, emit, *, length):
    step = min(length, 512)
    cb = p_ref.shape[-1]
    pad_ref[pl.ds(0, 8), :] = jnp.zeros((8, cb), F32)
    pad_ref[pl.ds(length + 8, 8), :] = jnp.zeros((8, cb), F32)
    for j in range(length // step):
        pad_ref[pl.ds(8 + j * step, step), :] = p_ref[pl.ds(j * step, step), :].astype(F32)
    w = w_ref[...]
    for j in range(length // step):
        r0 = j * step
        val = (pad_ref[pl.ds(r0 + 7, step), :] * w[0:1] + pad_ref[pl.ds(r0 + 8, step), :] * w[1:2]
               + pad_ref[pl.ds(r0 + 9, step), :] * w[2:3] + b_ref[...])
        emit(r0, step, val)


def _kf_kernel(kf_ref, kb_ref, fwd_ref, f2_ref, o_ref, a_ref, *, nz, ku):
    slab = 2 * FFT_N2
    for src, sign in ((kf_ref, 1.0), (kb_ref, -1.0)):
        _fft_stage1(src, fwd_ref, a_ref, nz=nz, ku=ku)

        def body(k1, carry, sign=sign, first=(src is kf_ref)):
            a = a_ref[pl.ds(pl.multiple_of(k1 * slab, slab), slab), :].astype(BF16)
            xk = jnp.dot(f2_ref[...], a, preferred_element_type=F32)
            if first:
                o_ref[k1] = xk
            else:
                o_ref[k1, :FFT_N2, :] = o_ref[k1, :FFT_N2, :] + xk[:FFT_N2]
                o_ref[k1, FFT_N2:, :] = o_ref[k1, FFT_N2:, :] - xk[FFT_N2:]
            return carry

        lax.fori_loop(0, ku, body, 0)


def _filter_spectrum(k_fwd, k_bwd, tables):
    length, width = k_fwd.shape
    _, _, nz, ku = _fft_plan(length)
    fwd, _, f2, _ = tables
    blk = pl.BlockSpec((length, FFT_CB), lambda c: (0, c))
    return pl.pallas_call(
        functools.partial(_kf_kernel, nz=nz, ku=ku),
        grid=(width // FFT_CB,),
        in_specs=[
            blk, blk,
            pl.BlockSpec(fwd.shape, lambda c: (0, 0, 0)),
            pl.BlockSpec(f2.shape, lambda c: (0, 0)),
        ],
        out_specs=pl.BlockSpec((ku, 2 * FFT_N2, FFT_CB), lambda c: (0, 0, c)),
        out_shape=jax.ShapeDtypeStruct((ku, 2 * FFT_N2, width), F32),
        scratch_shapes=[pltpu.VMEM((ku * 2 * FFT_N2, FFT_CB), F32)],
        compiler_params=_cparams(("parallel",)),
        name="hyena_filter_spectrum",
    )(k_fwd, k_bwd, fwd, f2)


def _hyena_conv_kernel(px0_ref, px1_ref, pv_ref, w0_ref, w1_ref, wv_ref, b0_ref, b1_ref, bv_ref, dsk_ref,
                       kf_ref, fwd_ref, inv_ref, f2_ref, f2i_ref, o_ref, a_ref, vv_ref, *, length, nz, ku):
    slab = 2 * FFT_N2

    def set_vv(r0, rows, val):
        vv_ref[pl.ds(r0, rows), :] = val

    def mul_vv(r0, rows, val):
        vv_ref[pl.ds(r0, rows), :] = vv_ref[pl.ds(r0, rows), :] * val

    _short_conv(pv_ref, wv_ref, bv_ref, a_ref, set_vv, length=length)
    _short_conv(px1_ref, w1_ref, b1_ref, a_ref, mul_vv, length=length)
    _fft_stage1(vv_ref, fwd_ref, a_ref, nz=nz, ku=ku)

    def freq_body(k1, carry):
        rows = pl.ds(k1 * slab if isinstance(k1, int) else pl.multiple_of(k1 * slab, slab), slab)
        xk = jnp.dot(f2_ref[...], a_ref[rows, :].astype(BF16), preferred_element_type=F32)
        kf = kf_ref[k1]
        xr, xi = xk[:FFT_N2], xk[FFT_N2:]
        kr, ki = kf[:FFT_N2], kf[FFT_N2:]
        prod = jnp.concatenate([xr * kr - xi * ki, xr * ki + xi * kr], axis=0).astype(BF16)
        a_ref[rows, :] = jnp.dot(f2i_ref[...], prod, preferred_element_type=F32)
        return carry

    lax.fori_loop(0, nz, freq_body, 0, unroll=2)
    freq_body(nz, 0)

    def time_body(i2, carry):
        re = a_ref[pl.ds(i2, ku, stride=slab), :]
        im = a_ref[pl.ds(FFT_N2 + i2, ku, stride=slab), :]
        q = jnp.concatenate([re, im], axis=0).astype(BF16)
        y = jnp.dot(inv_ref[i2], q, preferred_element_type=F32)
        rows = pl.ds(i2, nz, stride=FFT_N2)
        vv_ref[rows, :] = y + vv_ref[rows, :] * dsk_ref[...]
        return carry

    lax.fori_loop(0, FFT_N2, time_body, 0, unroll=FFT_UNROLL)

    def emit_out(r0, rows, val):
        o_ref[pl.ds(r0, rows), :] = (vv_ref[pl.ds(r0, rows), :] * val).astype(o_ref.dtype)

    _short_conv(px0_ref, w0_ref, b0_ref, a_ref, emit_out, length=length)


def _hyena_conv(p, conv_w, conv_b, d_skip, kf, tables, length):
    b, _, w3 = p.shape
    width = w3 // 3
    ncb = width // FFT_CB
    _, _, nz, ku = _fft_plan(length)
    fwd, inv, f2, f2i = tables
    pblk = lambda part: _resident((None, length, FFT_CB), lambda c, i: (i, 0, part * ncb + c))
    wblk = lambda part: pl.BlockSpec((3, FFT_CB), lambda c, i: (0, part * ncb + c))
    bblk = lambda part: pl.BlockSpec((1, FFT_CB), lambda c, i: (0, part * ncb + c))
    return pl.pallas_call(
        functools.partial(_hyena_conv_kernel, length=length, nz=nz, ku=ku),
        grid=(ncb, b),
        in_specs=[
            pblk(0), pblk(1), pblk(2), wblk(0), wblk(1), wblk(2), bblk(0), bblk(1), bblk(2),
            pl.BlockSpec((1, FFT_CB), lambda c, i: (0, c)),
            _resident((ku, 2 * FFT_N2, FFT_CB), lambda c, i: (0, 0, c)),
            _resident(fwd.shape, lambda c, i: (0, 0, 0)),
            _resident(inv.shape, lambda c, i: (0, 0, 0)),
            _resident(f2.shape, lambda c, i: (0, 0)),
            _resident(f2i.shape, lambda c, i: (0, 0)),
        ],
        out_specs=pl.BlockSpec((None, length, FFT_CB), lambda c, i: (i, 0, c)),
        out_shape=jax.ShapeDtypeStruct((b, length, width), BF16),
        scratch_shapes=[
            pltpu.VMEM((max(ku * 2 * FFT_N2, length + 16), FFT_CB), F32),
            pltpu.VMEM((length, FFT_CB), F32),
        ],
        compiler_params=_cparams(("arbitrary", "arbitrary")),
        name="hyena_long_conv",
    )(p, p, p, conv_w, conv_w, conv_w, conv_b.reshape(1, w3), conv_b.reshape(1, w3), conv_b.reshape(1, w3),
      d_skip.reshape(1, width), kf, fwd, inv, f2, f2i)


def _dense_dft_tables(length):
    n = 2 * length
    th = 2.0 * np.pi * ((np.arange(n)[:, None] * np.arange(length)[None, :]) % n) / n
    fwd = np.concatenate([np.cos(th), -np.sin(th)], axis=0)
    inv = np.concatenate([np.cos(th), -np.sin(th)], axis=0).T / n
    return jnp.asarray(fwd, F32).astype(BF16), jnp.asarray(inv, F32).astype(BF16)


def _kf_dense_kernel(kf_ref, kb_ref, fwd_ref, o_ref, *, n):
    xf = jnp.dot(fwd_ref[...], kf_ref[...].astype(BF16), preferred_element_type=F32)
    xb = jnp.dot(fwd_ref[...], kb_ref[...].astype(BF16), preferred_element_type=F32)
    o_ref[:n, :] = xf[:n] + xb[:n]
    o_ref[n:, :] = xf[n:] - xb[n:]


def _filter_spectrum_dense(k_fwd, k_bwd, fwd):
    length, width = k_fwd.shape
    n = 2 * length
    blk = pl.BlockSpec((length, FFT_CB), lambda c: (0, c))
    return pl.pallas_call(
        functools.partial(_kf_dense_kernel, n=n),
        grid=(width // FFT_CB,),
        in_specs=[blk, blk, pl.BlockSpec(fwd.shape, lambda c: (0, 0))],
        out_specs=pl.BlockSpec((2 * n, FFT_CB), lambda c: (0, c)),
        out_shape=jax.ShapeDtypeStruct((2 * n, width), F32),
        compiler_params=_cparams(("parallel",)),
        name="hyena_ctx_filter_spectrum",
    )(k_fwd, k_bwd, fwd)


def _hyena_ctx_kernel(px0_ref, px1_ref, pv_ref, w0_ref, w1_ref, wv_ref, b0_ref, b1_ref, bv_ref, dsk_ref,
                      kf_ref, fwd_ref, inv_ref, o_ref, pad_ref, vv_ref, *, length):
    n = 2 * length

    def set_vv(r0, rows, val):
        vv_ref[pl.ds(r0, rows), :] = val

    def mul_vv(r0, rows, val):
        vv_ref[pl.ds(r0, rows), :] = vv_ref[pl.ds(r0, rows), :] * val

    _short_conv(pv_ref, wv_ref, bv_ref, pad_ref, set_vv, length=length)
    _short_conv(px1_ref, w1_ref, b1_ref, pad_ref, mul_vv, length=length)
    vv = vv_ref[...]
    xk = jnp.dot(fwd_ref[...], vv.astype(BF16), preferred_element_type=F32)
    kf = kf_ref[...]
    xr, xi, kr, ki = xk[:n], xk[n:], kf[:n], kf[n:]
    prod = jnp.concatenate([xr * kr - xi * ki, xr * ki + xi * kr], axis=0).astype(BF16)
    vv_ref[...] = jnp.dot(inv_ref[...], prod, preferred_element_type=F32) + vv * dsk_ref[...]

    def emit_out(r0, rows, val):
        o_ref[pl.ds(r0, rows), :] = (vv_ref[pl.ds(r0, rows), :] * val).astype(o_ref.dtype)

    _short_conv(px0_ref, w0_ref, b0_ref, pad_ref, emit_out, length=length)


def _hyena_ctx_conv(p, conv_w, conv_b, d_skip, kf, fwd, inv, n_lat, length):
    b, _, w3 = p.shape
    width = w3 // 3
    ncb = width // FFT_CB
    rb = n_lat // length
    pblk = lambda part: pl.BlockSpec((None, length, FFT_CB), lambda c, i: (i, rb, part * ncb + c))
    wblk = lambda part: pl.BlockSpec((3, FFT_CB), lambda c, i: (0, part * ncb + c))
    bblk = lambda part: pl.BlockSpec((1, FFT_CB), lambda c, i: (0, part * ncb + c))
    return pl.pallas_call(
        functools.partial(_hyena_ctx_kernel, length=length),
        grid=(ncb, b),
        in_specs=[
            pblk(0), pblk(1), pblk(2), wblk(0), wblk(1), wblk(2), bblk(0), bblk(1), bblk(2),
            pl.BlockSpec((1, FFT_CB), lambda c, i: (0, c)),
            pl.BlockSpec((4 * length, FFT_CB), lambda c, i: (0, c)),
            pl.BlockSpec(fwd.shape, lambda c, i: (0, 0)),
            pl.BlockSpec(inv.shape, lambda c, i: (0, 0)),
        ],
        out_specs=pl.BlockSpec((None, length, FFT_CB), lambda c, i: (i, 0, c)),
        out_shape=jax.ShapeDtypeStruct((b, length, width), BF16),
        scratch_shapes=[pltpu.VMEM((length + 16, FFT_CB), F32), pltpu.VMEM((length, FFT_CB), F32)],
        compiler_params=_cparams(("arbitrary", "arbitrary")),
        name="hyena_ctx_conv",
    )(p, p, p, conv_w, conv_w, conv_w, conv_b.reshape(1, w3), conv_b.reshape(1, w3), conv_b.reshape(1, w3),
      d_skip.reshape(1, width), kf, fwd, inv)


def _row_copy(src, dst, sem):
    return pltpu.make_async_copy(src, dst, sem)


def _scatter_kernel(cnt_ref, start_ref, dest_ref, h_ref, x_hbm, src0, src1, src2, src3, zero_scr, sem, zsem, *, n_tok):
    rows, half = h_ref.shape
    src_scr = (src0, src1, src2, src3)
    n_real = TOP_K * n_tok
    row_iota = lax.broadcasted_iota(jnp.int32, (rows, LANES), 0)

    def ids_from(base):
        return pltpu.bitcast(base + row_iota, jnp.uint32)

    @pl.when(pl.program_id(0) == 0)
    def _():
        zero_scr[:, :half] = jnp.zeros((rows, half), jnp.uint32)

        def per_expert(e, total):
            cnt = cnt_ref[e]
            n_pad = (EXPERT_BLOCK - cnt % EXPERT_BLOCK) % EXPERT_BLOCK
            base = start_ref[e] + cnt
            zero_scr[:, half:] = ids_from(n_real + total)

            def one(r, carry):
                _row_copy(zero_scr.at[pl.ds(r, 1)], x_hbm.at[pl.ds(base + r, 1)], zsem).start()
                return carry

            def wait_one(r, carry):
                _row_copy(zero_scr.at[pl.ds(0, 1)], x_hbm.at[pl.ds(0, 1)], zsem).wait()
                return carry

            lax.fori_loop(0, n_pad, one, 0)
            lax.fori_loop(0, n_pad, wait_one, 0)
            return total + n_pad

        total = lax.fori_loop(0, N_EXPERTS, per_expert, 0)
        first_free = start_ref[N_EXPERTS] // EXPERT_BLOCK
        n_blocks = x_hbm.shape[0] // EXPERT_BLOCK

        def tail(i, carry):
            zero_scr[:, half:] = ids_from(n_real + total + (i - first_free) * EXPERT_BLOCK)
            cp = _row_copy(zero_scr, x_hbm.at[pl.ds(pl.multiple_of(i * EXPERT_BLOCK, EXPERT_BLOCK), EXPERT_BLOCK)],
                           zsem)
            cp.start()
            cp.wait()
            return carry

        lax.fori_loop(first_free, n_blocks, tail, 0)

    h = h_ref[...]
    for k in range(TOP_K):
        src_scr[k][:, :half] = h
        src_scr[k][:, half:] = ids_from(k * n_tok + pl.program_id(0) * rows)

    def issue(r, carry):
        for k in range(TOP_K):
            _row_copy(src_scr[k].at[pl.ds(r, 1)], x_hbm.at[pl.ds(dest_ref[k, r], 1)], sem).start()
        return carry

    lax.fori_loop(0, rows, issue, 0, unroll=4)
    for k in range(TOP_K):
        _row_copy(src_scr[k], x_hbm.at[pl.ds(0, rows)], sem).wait()


def _dispatch(h2p, dest_tiles, counts, pad_start, n_blocks):
    n_tok, half = h2p.shape
    assert TM == EXPERT_BLOCK
    width = half + LANES
    return pl.pallas_call(
        functools.partial(_scatter_kernel, n_tok=n_tok),
        grid_spec=pltpu.PrefetchScalarGridSpec(
            num_scalar_prefetch=2,
            grid=(n_tok // TM,),
            in_specs=[
                pl.BlockSpec((None, TOP_K, TM), lambda i, c, s: (i, 0, 0), memory_space=pltpu.SMEM),
                pl.BlockSpec((TM, half), lambda i, c, s: (i, 0)),
            ],
            out_specs=pl.BlockSpec(memory_space=pl.ANY),
            scratch_shapes=[pltpu.VMEM((TM, width), jnp.uint32)] * TOP_K + [
                pltpu.VMEM((EXPERT_BLOCK, width), jnp.uint32), pltpu.SemaphoreType.DMA(()), pltpu.SemaphoreType.DMA(())],
        ),
        out_shape=jax.ShapeDtypeStruct((n_blocks * EXPERT_BLOCK, width), jnp.uint32),
        compiler_params=_cparams(("arbitrary",)),
        name="moe_dispatch",
    )(counts, pad_start, dest_tiles, h2p)


def _ffn_kernel(be_ref, ids_prev_ref, ids_last_ref, x_ref, w1_ref, b1_ref, w2_ref, b2_ref, y_hbm,
                buf0, buf1, sem0, sem1, *, d_ff, half):
    i = pl.program_id(0)
    n = pl.num_programs(0)
    rows = x_ref.shape[0]
    spare = y_hbm.shape[0] - 2 * rows

    def drain(src, sem, ids_ref, lo=0, hi=rows):
        for r in range(lo, hi):
            _row_copy(src.at[pl.ds(r, 1)], y_hbm.at[pl.ds(ids_ref[0, r], 1)], sem).start()

    def wait(src, sem):
        _row_copy(src, y_hbm.at[pl.ds(0, rows)], sem).wait()

    @pl.when(i == 0)
    def _():
        buf0[...] = jnp.zeros(buf0.shape, buf0.dtype)
        buf1[...] = jnp.zeros(buf1.shape, buf1.dtype)
        _row_copy(buf0, y_hbm.at[pl.ds(spare + rows, rows)], sem0).start()

    def step(src, ssem, dst, dsem):
        drain(src, ssem, ids_prev_ref, 0, rows // 2)
        x = _unpack_bf16_pairs(x_ref[:, :half]).astype(BF16)
        hh = jnp.dot(x, w1_ref[...], preferred_element_type=F32) + b1_ref[...]
        drain(src, ssem, ids_prev_ref, rows // 2, rows)
        glu = jnp.minimum(hh[:, :d_ff], SWIGLU_LIMIT)
        lin = jnp.clip(hh[:, d_ff:], -SWIGLU_LIMIT, SWIGLU_LIMIT)
        act = (glu * jax.nn.sigmoid(SWIGLU_ALPHA * glu) * (lin + 1.0)).astype(BF16)
        y = _pack_bf16_pairs(jnp.dot(act, w2_ref[...], preferred_element_type=F32) + b2_ref[...])
        wait(dst, dsem)
        dst[...] = y

    @pl.when(i % 2 == 0)
    def _():
        step(buf1, sem1, buf0, sem0)

    @pl.when(i % 2 == 1)
    def _():
        step(buf0, sem0, buf1, sem1)

    @pl.when(i == n - 1)
    def _():
        last, lsem, prev, psem = (buf0, sem0, buf1, sem1) if (y_hbm.shape[0] // rows) % 2 == 1 else (buf1, sem1, buf0, sem0)
        drain(last, lsem, ids_last_ref)
        wait(prev, psem)
        wait(last, lsem)


def _expert_ffn(x_disp, ids, block_e, w1, b1, w2, b2):
    rows, width = x_disp.shape
    half = width - LANES
    n_e, d, ff2 = w1.shape
    d_ff = ff2 // 2
    n_blocks = rows // EXPERT_BLOCK
    ids_prev = jnp.concatenate([rows + jnp.arange(EXPERT_BLOCK, dtype=jnp.int32), ids])
    ids_prev = ids_prev.reshape(n_blocks + 1, 1, EXPERT_BLOCK)
    return pl.pallas_call(
        functools.partial(_ffn_kernel, d_ff=d_ff, half=half),
        grid_spec=pltpu.PrefetchScalarGridSpec(
            num_scalar_prefetch=1,
            grid=(n_blocks,),
            in_specs=[
                pl.BlockSpec((None, 1, EXPERT_BLOCK), lambda i, be: (i, 0, 0), memory_space=pltpu.SMEM),
                pl.BlockSpec((None, 1, EXPERT_BLOCK), lambda i, be: (n_blocks, 0, 0), memory_space=pltpu.SMEM),
                pl.BlockSpec((EXPERT_BLOCK, width), lambda i, be: (i, 0)),
                pl.BlockSpec((None, d, ff2), lambda i, be: (be[i], 0, 0)),
                pl.BlockSpec((None, 1, ff2), lambda i, be: (be[i], 0, 0)),
                pl.BlockSpec((None, d_ff, d), lambda i, be: (be[i], 0, 0)),
                pl.BlockSpec((None, 1, d), lambda i, be: (be[i], 0, 0)),
            ],
            out_specs=pl.BlockSpec(memory_space=pl.ANY),
            scratch_shapes=[pltpu.VMEM((EXPERT_BLOCK, half), jnp.uint32), pltpu.VMEM((EXPERT_BLOCK, half), jnp.uint32),
                            pltpu.SemaphoreType.DMA(()), pltpu.SemaphoreType.DMA(())],
        ),
        out_shape=jax.ShapeDtypeStruct((rows + 2 * EXPERT_BLOCK, half), jnp.uint32),
        compiler_params=_cparams(("arbitrary",)),
        name="moe_expert_ffn",
    )(block_e, ids_prev, ids_prev, x_disp, w1, b1.reshape(n_e, 1, ff2), w2, b2.reshape(n_e, 1, d))


def _combine_kernel(y0_ref, y1_ref, y2_ref, y3_ref, gate_ref, x_ref, mod_ref, fg_ref, o_ref, *, d, final):
    gate = gate_ref[...]
    out = _unpack_bf16_pairs(y0_ref[...]) * gate[:, 0:1]
    for k, y_ref in enumerate((y1_ref, y2_ref, y3_ref), start=1):
        out = out + _unpack_bf16_pairs(y_ref[...]) * gate[:, k:k + 1]
    xn = x_ref[...] + mod_ref[:, 5 * d:6 * d] * out
    if final:
        xn = xn * lax.rsqrt(jnp.mean(xn * xn, axis=-1, keepdims=True) + EPS) * fg_ref[...]
    o_ref[...] = xn


def _combine(y_rows, gates, x, mod, final_g, n_lat_tiles, final):
    b, t, d = x.shape
    nt = t // TM
    n_tiles = b * nt
    nt_out = n_lat_tiles if final else nt
    y_spec = lambda k: pl.BlockSpec((TM, d // 2), lambda i, j: (k * n_tiles + i * nt + j, 0))
    return pl.pallas_call(
        functools.partial(_combine_kernel, d=d, final=final),
        grid=(b, nt_out),
        in_specs=[
            y_spec(0), y_spec(1), y_spec(2), y_spec(3),
            pl.BlockSpec((TM, TOP_K), lambda i, j: (i * nt + j, 0)),
            pl.BlockSpec((None, TM, d), lambda i, j: (i, j, 0)),
            pl.BlockSpec((None, 1, 6 * d), lambda i, j: (jnp.where(j >= n_lat_tiles, b, i), 0, 0)),
            pl.BlockSpec((1, d), lambda i, j: (0, 0)),
        ],
        out_specs=pl.BlockSpec((None, TM, d), lambda i, j: (i, j, 0)),
        out_shape=jax.ShapeDtypeStruct((b, nt_out * TM, d), F32),
        compiler_params=_cparams(("parallel", "parallel")),
        name="moe_combine",
    )(y_rows, y_rows, y_rows, y_rows, gates, x, mod, final_g.reshape(1, d))


def _moe(h2p, idx_t, gates_t, rank_t, cnt, x, mod, w1, b1, w2, b2, final_g, n_lat_tiles, final):
    b, t, d = x.shape
    n_tok = b * t
    n_assign = n_tok * TOP_K
    counts = cnt[:, 0]
    padded = (counts + EXPERT_BLOCK - 1) // EXPERT_BLOCK * EXPERT_BLOCK
    pad_end = jnp.cumsum(padded).astype(jnp.int32)
    pad_start = pad_end - padded
    experts = jnp.arange(N_EXPERTS, dtype=jnp.int32)
    dest_t = rank_t + jnp.sum(jnp.where(idx_t[..., None] == experts, pad_start, 0), axis=-1)
    n_blocks = -(-n_assign // EXPERT_BLOCK) + N_EXPERTS
    block_start = jnp.arange(n_blocks, dtype=jnp.int32) * EXPERT_BLOCK
    block_e = jnp.minimum(jnp.sum(pad_end[None, :] <= block_start[:, None], axis=1), N_EXPERTS - 1).astype(jnp.int32)
    dest_tiles = dest_t.reshape(TOP_K, n_tok // TM, TM).transpose(1, 0, 2)

    x_disp = _dispatch(h2p.reshape(n_tok, d // 2), dest_tiles, counts,
                       jnp.concatenate([pad_start, pad_end[-1:]]), n_blocks)
    ids = lax.bitcast_convert_type(x_disp[:, d // 2], jnp.int32)
    y_rows = _expert_ffn(x_disp, ids, block_e, w1, b1, w2, b2)
    return _combine(y_rows, gates_t.T, x, mod, final_g, n_lat_tiles, final)


def _rope_tables(n_lat, n_ctx):
    rows = n_lat // GRID_W
    row = jnp.repeat(jnp.arange(rows, dtype=F32), GRID_W)
    col = jnp.tile(jnp.arange(GRID_W, dtype=F32), rows)
    axis_dim = HEAD_DIM // 2
    inv = ROPE_THETA ** (-jnp.arange(0, axis_dim, 2, dtype=F32) / axis_dim)
    ang_r = row[:, None] * inv
    ang_c = col[:, None] * inv
    ang = jnp.concatenate([ang_r, ang_r, ang_c, ang_c], axis=-1)
    cos = jnp.concatenate([jnp.cos(ang), jnp.ones((n_ctx, HEAD_DIM), F32)], axis=0)
    sin = jnp.concatenate([jnp.sin(ang), jnp.zeros((n_ctx, HEAD_DIM), F32)], axis=0)
    even = (jnp.arange(HEAD_DIM) // 32) % 2 == 0
    return cos, jnp.where(even, -sin, 0.0), jnp.where(even, 0.0, sin)


def kernel(x, c, ctx, c_ctx, ada_w, ada_b, norm1_g, norm2_g, mix_w_out, router_w, router_b, exp_w1, exp_b1, exp_w2, exp_b2, attn_w_in, attn_q_gain, attn_k_gain, gmlp_w_in, gmlp_v_gain, gmlp_w_s, gmlp_b_s, hyena_w_in, hyena_conv_w, hyena_conv_b, hyena_f_w1, hyena_f_b1, hyena_f_w2, hyena_f_b2, hyena_f_w3, hyena_f_b3, hyena_f_w4, hyena_f_b4, hyena_freq, hyena_d, final_g):
    b, s, d = x.shape
    n_ctx = ctx.shape[1]
    depth = ada_w.shape[0]
    assert s % TM == 0 and n_ctx == TM and b < MOD_ROWS and s % (FFT_N2 * 8) == 0
    n_lat_tiles = s // TM

    xs = jnp.concatenate([x, ctx], axis=1)
    cc = jnp.concatenate([c, c_ctx[None, :], jnp.zeros((MOD_ROWS - b - 1, d), F32)], axis=0)
    mods = _modulation(cc, ada_w, ada_b).reshape(depth, MOD_ROWS, 1, 6 * d)
    rope = _rope_tables(s, n_ctx)

    for i in range(depth):
        kind, j = i % 3, i // 3
        last = i == depth - 1
        mod = mods[i]
        w_out = mix_w_out[i].astype(BF16)
        if kind == 0:
            q, k, v = _attn_proj(xs, mod, norm1_g[i], attn_w_in[j].astype(BF16), attn_q_gain[j], attn_k_gain[j],
                                 rope, n_lat_tiles)
            y = _attention(q, k, v, s)
        elif kind == 2:
            p = _inproj(xs, mod, norm1_g[i], hyena_w_in[j].astype(BF16), n_lat_tiles)
            fargs = (hyena_f_w1[j], hyena_f_b1[j], hyena_f_w2[j], hyena_f_b2[j], hyena_f_w3[j], hyena_f_b3[j],
                     hyena_f_w4[j], hyena_f_b4[j], hyena_freq[j])
            tables = _fft_tables(s)
            kf = _filter_spectrum(*_hyena_filters(s, *fargs), tables)
            y_lat = _hyena_conv(p, hyena_conv_w[j], hyena_conv_b[j], hyena_d[j], kf, tables, s)
            dfwd, dinv = _dense_dft_tables(n_ctx)
            kf_c = _filter_spectrum_dense(*_hyena_filters(n_ctx, *fargs), dfwd)
            y_ctx = _hyena_ctx_conv(p, hyena_conv_w[j], hyena_conv_b[j], hyena_d[j], kf_c, dfwd, dinv, s, n_ctx)
            y = jnp.concatenate([y_lat, y_ctx], axis=1)
        if kind == 1:
            xs, *routed = _gmlp_layer(
                xs, mod, norm1_g[i], gmlp_w_in[j].astype(BF16), gmlp_v_gain[j], gmlp_w_s[j].astype(BF16),
                gmlp_b_s[j], w_out, norm2_g[i], router_w[i], router_b[i], n_lat_tiles)
        else:
            xs, *routed = _post(y, xs, mod, w_out, norm2_g[i], router_w[i], router_b[i], n_lat_tiles)
        xs = _moe(*routed, xs, mod, exp_w1[i].astype(BF16), exp_b1[i], exp_w2[i].astype(BF16), exp_b2[i],
                  final_g, n_lat_tiles, last)
    return xs
```

```python
import functools
import math

import jax
import jax.numpy as jnp
import numpy as np
from jax import lax
from jax.experimental import pallas as pl
from jax.experimental.pallas import tpu as pltpu

F32 = jnp.float32
BF16 = jnp.bfloat16
HIGHEST = lax.Precision.HIGHEST

EPS = 1e-6
GRID_W = 64
ROPE_THETA = 10000.0
HEAD_DIM = 128
N_KV_HEADS = 2
Q_GROUP = 4
CHUNK = 128
GMLP_GROUPS = 8
N_EXPERTS = 32
TOP_K = 4
SWIGLU_ALPHA = 1.702
SWIGLU_LIMIT = 7.0
N_BANDS = 16
DECAY_TARGET = 1e-2
MAX_DECAY = math.log(DECAY_TARGET) / 0.3
MIN_DECAY = math.log(DECAY_TARGET) / 1.5

LANES = 128
TM = 256
ATTN_TK = 512
MOD_ROWS = 16
EXPERT_BLOCK = 256
FFT_N2 = 128
FFT_CB = 128
FFT_UNROLL = 4
VMEM_LIMIT = 56 * 1024 * 1024


def _cparams(sem):
    return pltpu.CompilerParams(dimension_semantics=sem, vmem_limit_bytes=VMEM_LIMIT)


def _norm_mod(x, g, scale, shift):
    y = x * lax.rsqrt(jnp.mean(x * x, axis=-1, keepdims=True) + EPS)
    return y * g * (1.0 + scale) + shift


def _resident(shape, index_map):
    return pl.BlockSpec(shape, index_map, pipeline_mode=pl.Buffered(1))


def _mod_kernel(c_ref, w_ref, b_ref, o_ref):
    c = c_ref[...]
    s = c * jax.nn.sigmoid(c)
    o_ref[...] = jnp.dot(s, w_ref[...], precision=HIGHEST, preferred_element_type=F32) + b_ref[...]


def _modulation(cc, ada_w, ada_b):
    depth, d, d6 = ada_w.shape
    tn = 1536
    return pl.pallas_call(
        _mod_kernel,
        grid=(depth, d6 // tn),
        in_specs=[
            pl.BlockSpec((MOD_ROWS, d), lambda l, j: (0, 0)),
            pl.BlockSpec((None, d, tn), lambda l, j: (l, 0, j)),
            pl.BlockSpec((None, 1, tn), lambda l, j: (l, 0, j)),
        ],
        out_specs=pl.BlockSpec((None, MOD_ROWS, tn), lambda l, j: (l, 0, j)),
        out_shape=jax.ShapeDtypeStruct((depth, MOD_ROWS, d6), F32),
        compiler_params=_cparams(("arbitrary", "arbitrary")),
        name="adaln_mod",
    )(cc, ada_w, ada_b.reshape(depth, 1, d6))


def _inproj_kernel(x_ref, mod_ref, g_ref, w_ref, o_ref, *, d):
    mod = mod_ref[...]
    h = _norm_mod(x_ref[...], g_ref[...], mod[:, d:2 * d], mod[:, 0:d]).astype(BF16)
    o_ref[...] = jnp.dot(h, w_ref[...], preferred_element_type=F32).astype(o_ref.dtype)


def _inproj(x, mod, g, w, n_lat_tiles):
    b, t, d = x.shape
    n_out = w.shape[1]
    return pl.pallas_call(
        functools.partial(_inproj_kernel, d=d),
        grid=(b, t // TM),
        in_specs=[
            pl.BlockSpec((None, TM, d), lambda i, j: (i, j, 0)),
            pl.BlockSpec((None, 1, 6 * d), lambda i, j: (jnp.where(j >= n_lat_tiles, b, i), 0, 0)),
            pl.BlockSpec((1, d), lambda i, j: (0, 0)),
            pl.BlockSpec((d, n_out), lambda i, j: (0, 0)),
        ],
        out_specs=pl.BlockSpec((None, TM, n_out), lambda i, j: (i, j, 0)),
        out_shape=jax.ShapeDtypeStruct((b, t, n_out), BF16),
        compiler_params=_cparams(("parallel", "parallel")),
        name="inproj",
    )(x, mod, g.reshape(1, d), w)


def _attn_proj_kernel(x_ref, mod_ref, g_ref, w_ref, qg_ref, kg_ref, cos_ref, sa_ref, sb_ref,
                      q_ref, k_ref, v_ref, *, d):
    mod = mod_ref[...]
    h = _norm_mod(x_ref[...], g_ref[...], mod[:, d:2 * d], mod[:, 0:d]).astype(BF16)
    p = jnp.dot(h, w_ref[...], preferred_element_type=F32)
    cos, sin_a, sin_b = cos_ref[...], sa_ref[...], sb_ref[...]
    n_q = Q_GROUP * N_KV_HEADS

    def head(ph, gain):
        y = ph * lax.rsqrt(jnp.mean(ph * ph, axis=-1, keepdims=True) + EPS) * gain
        return y * cos + pltpu.roll(y, LANES - 32, 1) * sin_a + pltpu.roll(y, 32, 1) * sin_b

    qg = qg_ref[...] * (HEAD_DIM ** -0.5 * math.log2(math.e))
    kg = kg_ref[...]
    for i in range(n_q):
        q_ref[:, i * HEAD_DIM:(i + 1) * HEAD_DIM] = head(p[:, i * HEAD_DIM:(i + 1) * HEAD_DIM], qg).astype(BF16)
    k0 = n_q * HEAD_DIM
    for i in range(N_KV_HEADS):
        k_ref[:, i * HEAD_DIM:(i + 1) * HEAD_DIM] = head(
            p[:, k0 + i * HEAD_DIM:k0 + (i + 1) * HEAD_DIM], kg).astype(BF16)
    v0 = k0 + N_KV_HEADS * HEAD_DIM
    for i in range(N_KV_HEADS):
        v_ref[:, 2 * i * HEAD_DIM:(2 * i + 1) * HEAD_DIM] = p[:, v0 + i * HEAD_DIM:v0 + (i + 1) * HEAD_DIM].astype(BF16)
        v_ref[:, (2 * i + 1) * HEAD_DIM:(2 * i + 2) * HEAD_DIM] = jnp.ones((p.shape[0], HEAD_DIM), BF16)


def _attn_proj(x, mod, g, w, q_gain, k_gain, rope, n_lat_tiles):
    b, t, d = x.shape
    n_out = w.shape[1]
    qw = Q_GROUP * N_KV_HEADS * HEAD_DIM
    kw = N_KV_HEADS * HEAD_DIM
    row = lambda i, j: (i, j, 0)
    tab = pl.BlockSpec((TM, HEAD_DIM), lambda i, j: (j, 0))
    return pl.pallas_call(
        functools.partial(_attn_proj_kernel, d=d),
        grid=(b, t // TM),
        in_specs=[
            pl.BlockSpec((None, TM, d), row),
            pl.BlockSpec((None, 1, 6 * d), lambda i, j: (jnp.where(j >= n_lat_tiles, b, i), 0, 0)),
            pl.BlockSpec((1, d), lambda i, j: (0, 0)),
            pl.BlockSpec((d, n_out), lambda i, j: (0, 0)),
            pl.BlockSpec((1, HEAD_DIM), lambda i, j: (0, 0)),
            pl.BlockSpec((1, HEAD_DIM), lambda i, j: (0, 0)),
            tab, tab, tab,
        ],
        out_specs=[
            pl.BlockSpec((None, TM, qw), row),
            pl.BlockSpec((None, TM, kw), row),
            pl.BlockSpec((None, TM, 2 * kw), row),
        ],
        out_shape=[
            jax.ShapeDtypeStruct((b, t, qw), BF16),
            jax.ShapeDtypeStruct((b, t, kw), BF16),
            jax.ShapeDtypeStruct((b, t, 2 * kw), BF16),
        ],
        compiler_params=_cparams(("parallel", "parallel")),
        name="attn_proj",
    )(x, mod, g.reshape(1, d), w, q_gain.reshape(1, HEAD_DIM), k_gain.reshape(1, HEAD_DIM), *rope)


def _attn_kernel(q_ref, k_ref, v_ref, o_ref, q_scr, s0, s1, p0, p1, a0, a1, m_scr, acc_scr, *, n_lat, n_ctx, tk):
    is_ctx = pl.program_id(2) * TM >= n_lat
    s_b, p_b, a_b = (s0, s1), (p0, p1), (a0, a1)
    for g in range(Q_GROUP):
        q_scr[g * TM:(g + 1) * TM, :] = q_ref[:, g * HEAD_DIM:(g + 1) * HEAD_DIM]
    m_scr[...] = jnp.full(m_scr.shape, -jnp.inf, F32)
    acc_scr[...] = jnp.zeros(acc_scr.shape, F32)

    def scores(start, size, slot):
        k = k_ref[pl.ds(start, size), :]
        s_b[slot][:, :size] = lax.dot_general(q_scr[...], k, (((1,), (1,)), ((), ())), preferred_element_type=F32)

    def softmax(size, slot):
        nb = size // LANES
        s = s_b[slot][:, :size]
        mx = s[:, 0:LANES]
        for j in range(1, nb):
            mx = jnp.maximum(mx, s[:, j * LANES:(j + 1) * LANES])
        m_prev = m_scr[...]
        m_new = jnp.maximum(m_prev, jnp.max(mx, axis=-1, keepdims=True))
        a_b[slot][...] = jnp.exp2(m_prev - m_new)
        m_scr[...] = m_new
        m_rep = jnp.concatenate([m_new] * nb, axis=1) if nb > 1 else m_new
        p_b[slot][:, :size] = jnp.exp2(s - m_rep).astype(BF16)

    def values(start, size, slot):
        a = a_b[slot][...]
        acc_scr[...] = jnp.concatenate([a, a], axis=1) * acc_scr[...] + jnp.dot(
            p_b[slot][:, :size], v_ref[pl.ds(start, size), :], preferred_element_type=F32)

    scores(n_lat, n_ctx, 0)
    softmax(n_ctx, 0)
    values(n_lat, n_ctx, 0)

    n = n_lat // tk

    @pl.when(jnp.logical_not(is_ctx))
    def _():
        scores(0, tk, 0)
        softmax(tk, 0)
        scores(tk, tk, 1)

        def pair(i, carry):
            j = 2 + 2 * i
            values(pl.multiple_of((j - 2) * tk, tk), tk, 0)
            softmax(tk, 1)
            scores(pl.multiple_of(j * tk, tk), tk, 0)
            values(pl.multiple_of((j - 1) * tk, tk), tk, 1)
            softmax(tk, 0)
            scores(pl.multiple_of((j + 1) * tk, tk), tk, 1)
            return carry

        lax.fori_loop(0, (n - 2) // 2, pair, 0)
        values((n - 2) * tk, tk, 0)
        softmax(tk, 1)
        values((n - 1) * tk, tk, 1)

    out = acc_scr[:, :HEAD_DIM] / acc_scr[:, HEAD_DIM:]
    for g in range(Q_GROUP):
        o_ref[:, g * HEAD_DIM:(g + 1) * HEAD_DIM] = out[g * TM:(g + 1) * TM].astype(o_ref.dtype)


def _attention(q, k, v, n_lat):
    b, t, _ = q.shape
    n_ctx = t - n_lat
    gw = Q_GROUP * HEAD_DIM
    tk = ATTN_TK
    assert n_lat % (2 * tk) == 0 and n_ctx <= tk and n_ctx % LANES == 0
    m = Q_GROUP * TM
    return pl.pallas_call(
        functools.partial(_attn_kernel, n_lat=n_lat, n_ctx=n_ctx, tk=tk),
        grid=(b, N_KV_HEADS, t // TM),
        in_specs=[
            pl.BlockSpec((None, TM, gw), lambda i, h, j: (i, j, h)),
            pl.BlockSpec((None, t, HEAD_DIM), lambda i, h, j: (i, 0, h)),
            pl.BlockSpec((None, t, 2 * HEAD_DIM), lambda i, h, j: (i, 0, h)),
        ],
        out_specs=pl.BlockSpec((None, TM, gw), lambda i, h, j: (i, j, h)),
        out_shape=jax.ShapeDtypeStruct(q.shape, BF16),
        scratch_shapes=[
            pltpu.VMEM((m, HEAD_DIM), BF16),
            pltpu.VMEM((m, tk), F32), pltpu.VMEM((m, tk), F32),
            pltpu.VMEM((m, tk), BF16), pltpu.VMEM((m, tk), BF16),
            pltpu.VMEM((m, LANES), F32), pltpu.VMEM((m, LANES), F32),
            pltpu.VMEM((m, LANES), F32),
            pltpu.VMEM((m, 2 * HEAD_DIM), F32),
        ],
        compiler_params=_cparams(("parallel", "parallel", "parallel")),
        name="attention",
    )(q, k, v)


def _pack_bf16_pairs(a):
    n = a.shape[1] // 2
    bits = pltpu.bitcast(a.astype(BF16).astype(F32), jnp.uint32)
    return (bits[:, :n] >> 16) | (bits[:, n:] & jnp.uint32(0xFFFF0000))


def _unpack_bf16_pairs(w):
    lo = pltpu.bitcast(w << 16, F32)
    hi = pltpu.bitcast(w & jnp.uint32(0xFFFF0000), F32)
    return jnp.concatenate([lo, hi], axis=1)


def _residual_router(y, x, mod, w_out, g2n, rwt, rb, outs, cnt_scr, *, d):
    xo_ref, h2_ref, idx_ref, gate_ref, rank_ref, cnt_ref = outs
    first = jnp.logical_and(pl.program_id(0) == 0, pl.program_id(1) == 0)

    @pl.when(first)
    def _():
        cnt_scr[...] = jnp.zeros(cnt_scr.shape, F32)

    xn = x + mod[:, 2 * d:3 * d] * jnp.dot(y, w_out, preferred_element_type=F32)
    xo_ref[...] = xn
    h2 = _norm_mod(xn, g2n, mod[:, 4 * d:5 * d], mod[:, 3 * d:4 * d])
    h2_ref[...] = _pack_bf16_pairs(h2)
    lg = lax.dot_general(rwt, h2, (((1,), (1,)), ((), ())), precision=HIGHEST,
                         preferred_element_type=F32) + rb
    row = lax.broadcasted_iota(jnp.int32, lg.shape, 0)
    vals, idxs = [], []
    for _ in range(TOP_K):
        m = jnp.max(lg, axis=0, keepdims=True)
        i = jnp.min(jnp.where(lg == m, row, N_EXPERTS), axis=0, keepdims=True)
        vals.append(m)
        idxs.append(i)
        lg = jnp.where(row == i, -jnp.inf, lg)
    es = [jnp.exp(vv - vals[0]) for vv in vals]
    tot = es[0] + es[1] + es[2] + es[3]
    idx_ref[...] = jnp.concatenate(idxs, axis=0)
    gate_ref[...] = jnp.concatenate([e / tot for e in es], axis=0)

    tm = lg.shape[1]
    earlier = (lax.broadcasted_iota(jnp.int32, (tm, tm), 0) < lax.broadcasted_iota(jnp.int32, (tm, tm), 1)).astype(BF16)
    run = cnt_scr[...]
    ranks = []
    for i in idxs:
        hit = row == i
        before = jnp.dot(hit.astype(BF16), earlier, preferred_element_type=F32)
        ranks.append(jnp.sum(jnp.where(hit, run + before, 0.0), axis=0, keepdims=True))
        run = run + jnp.sum(hit.astype(F32), axis=1, keepdims=True)
    cnt_scr[...] = run
    rank_ref[...] = jnp.concatenate(ranks, axis=0).astype(jnp.int32)
    cnt_ref[...] = run.astype(jnp.int32)


def _post_kernel(y_ref, x_ref, mod_ref, w_ref, g_ref, rwt_ref, rb_ref, *rest, d):
    _residual_router(y_ref[...], x_ref[...], mod_ref[...], w_ref[...], g_ref[...], rwt_ref[...], rb_ref[...],
                     rest[:-1], rest[-1], d=d)


def _post_specs(b, t, d, n_lat_tiles):
    nt = t // TM
    row = lambda i, j: (i, j, 0)
    const2 = lambda i, j: (0, 0)
    in_tail = [
        pl.BlockSpec((d, d), const2),
        pl.BlockSpec((1, d), const2),
        pl.BlockSpec((N_EXPERTS, d), const2),
        pl.BlockSpec((N_EXPERTS, 1), const2),
    ]
    per_token = pl.BlockSpec((TOP_K, TM), lambda i, j: (0, i * nt + j))
    out_specs = [
        pl.BlockSpec((None, TM, d), row),
        pl.BlockSpec((None, TM, d // 2), row),
        per_token, per_token, per_token,
        pl.BlockSpec((N_EXPERTS, 1), const2),
    ]
    out_shape = [
        jax.ShapeDtypeStruct((b, t, d), F32),
        jax.ShapeDtypeStruct((b, t, d // 2), jnp.uint32),
        jax.ShapeDtypeStruct((TOP_K, b * t), jnp.int32),
        jax.ShapeDtypeStruct((TOP_K, b * t), F32),
        jax.ShapeDtypeStruct((TOP_K, b * t), jnp.int32),
        jax.ShapeDtypeStruct((N_EXPERTS, 1), jnp.int32),
    ]
    mod_spec = pl.BlockSpec((None, 1, 6 * d), lambda i, j: (jnp.where(j >= n_lat_tiles, b, i), 0, 0))
    return mod_spec, in_tail, out_specs, out_shape


def _post(y, x, mod, w_out, g2n, router_w, router_b, n_lat_tiles):
    b, t, d = x.shape
    mod_spec, in_tail, out_specs, out_shape = _post_specs(b, t, d, n_lat_tiles)
    row = lambda i, j: (i, j, 0)
    return pl.pallas_call(
        functools.partial(_post_kernel, d=d),
        grid=(b, t // TM),
        in_specs=[pl.BlockSpec((None, TM, d), row), pl.BlockSpec((None, TM, d), row), mod_spec] + in_tail,
        out_specs=out_specs,
        out_shape=out_shape,
        scratch_shapes=[pltpu.VMEM((N_EXPERTS, 1), F32)],
        compiler_params=_cparams(("arbitrary", "arbitrary")),
        name="outproj_router",
    )(y, x, mod, w_out, g2n.reshape(1, d), router_w.T, router_b.reshape(N_EXPERTS, 1))


def _gmlp_kernel(x_ref, mod_ref, g1_ref, win_ref, vg_ref, ws_ref, bs_ref, w_ref, g_ref, rwt_ref, rb_ref,
                 *rest, d):
    outs, cnt_scr, y_scr = rest[:-2], rest[-2], rest[-1]
    mod = mod_ref[...]
    x = x_ref[...]
    h = _norm_mod(x, g1_ref[...], mod[:, d:2 * d], mod[:, 0:d]).astype(BF16)
    z = jnp.dot(h, win_ref[...], preferred_element_type=F32)
    z = 0.5 * z * (1.0 + lax.erf(z * (2.0 ** -0.5)))
    width = z.shape[1] // 2
    u, v = z[:, :width], z[:, width:]
    v = (v * lax.rsqrt(jnp.mean(v * v, axis=-1, keepdims=True) + EPS) * vg_ref[...]).astype(BF16)
    gd = width // GMLP_GROUPS
    for n in range(TM // CHUNK):
        r = slice(n * CHUNK, (n + 1) * CHUNK)
        for g in range(GMLP_GROUPS):
            cs = slice(g * gd, (g + 1) * gd)
            mixed = jnp.dot(ws_ref[g], v[r, cs], preferred_element_type=F32) + bs_ref[:, g:g + 1]
            y_scr[r, cs] = (u[r, cs] * mixed).astype(BF16)
    _residual_router(y_scr[...], x, mod, w_ref[...], g_ref[...], rwt_ref[...], rb_ref[...], outs, cnt_scr, d=d)


def _gmlp_layer(x, mod, g1n, w_in, v_gain, w_s, b_s, w_out, g2n, router_w, router_b, n_lat_tiles):
    b, t, d = x.shape
    width = w_in.shape[1] // 2
    mod_spec, in_tail, out_specs, out_shape = _post_specs(b, t, d, n_lat_tiles)
    row = lambda i, j: (i, j, 0)
    const2 = lambda i, j: (0, 0)
    return pl.pallas_call(
        functools.partial(_gmlp_kernel, d=d),
        grid=(b, t // TM),
        in_specs=[
            pl.BlockSpec((None, TM, d), row),
            mod_spec,
            pl.BlockSpec((1, d), const2),
            pl.BlockSpec((d, 2 * width), const2),
            pl.BlockSpec((1, width), const2),
            pl.BlockSpec((GMLP_GROUPS, CHUNK, CHUNK), lambda i, j: (0, 0, 0)),
            pl.BlockSpec((CHUNK, GMLP_GROUPS), const2),
        ] + in_tail,
        out_specs=out_specs,
        out_shape=out_shape,
        scratch_shapes=[pltpu.VMEM((N_EXPERTS, 1), F32), pltpu.VMEM((TM, width), BF16)],
        compiler_params=_cparams(("arbitrary", "arbitrary")),
        name="gmlp_layer",
    )(x, mod, g1n.reshape(1, d), w_in, v_gain.reshape(1, width), w_s, b_s.T, w_out, g2n.reshape(1, d),
      router_w.T, router_b.reshape(N_EXPERTS, 1))


def _filter_mlp_kernel(z_ref, t_ref, w1, b1, w2, b2, w3, b3, w4, b4, fr, dl, kf_ref, kb_ref, *, width, tl):
    dot = functools.partial(jnp.dot, precision=HIGHEST, preferred_element_type=F32)
    f = fr[...]
    a = jnp.sin(f * (dot(z_ref[...], w1[...]) + b1[...]))
    a = jnp.sin(f * (dot(a, w2[...]) + b2[...]))
    a = jnp.sin(f * (dot(a, w3[...]) + b3[...]))
    k = dot(a, w4[...]) + b4[...]
    window = jnp.exp(-t_ref[...] * dl[...])
    kf_ref[...] = k[:, :width] * window
    pos = pl.program_id(0) * tl + lax.broadcasted_iota(jnp.int32, (tl, 1), 0)
    kb_ref[...] = jnp.where(pos == 0, 0.0, k[:, width:] * window)


def _hyena_filters(length, f_w1, f_b1, f_w2, f_b2, f_w3, f_b3, f_w4, f_b4, freq):
    width = f_w4.shape[1] // 2
    hid = f_w1.shape[1]
    emb = 2 * N_BANDS + 1
    t = jnp.linspace(0.0, 1.0, length, dtype=F32)[:, None]
    w = 2.0 * math.pi * jnp.arange(length, dtype=F32)[:, None] / length
    f = jnp.linspace(1e-4, N_BANDS - 1, N_BANDS, dtype=F32)[None, :]
    z = jnp.concatenate([t, jnp.cos(f * w), -jnp.sin(f * w), jnp.zeros((length, hid - emb), F32)], axis=-1)
    w1p = jnp.concatenate([f_w1, jnp.zeros((hid - emb, hid), F32)], axis=0)
    deltas = jnp.abs(jnp.linspace(MIN_DECAY, MAX_DECAY, width, dtype=F32))[None, :]
    tl = min(length, 512)
    full = lambda shape: pl.BlockSpec(shape, lambda i: (0, 0))
    return pl.pallas_call(
        functools.partial(_filter_mlp_kernel, width=width, tl=tl),
        grid=(length // tl,),
        in_specs=[
            pl.BlockSpec((tl, hid), lambda i: (i, 0)),
            pl.BlockSpec((tl, 1), lambda i: (i, 0)),
            full((hid, hid)), full((1, hid)), full((hid, hid)), full((1, hid)), full((hid, hid)), full((1, hid)),
            full((hid, 2 * width)), full((1, 2 * width)), full((1, hid)), full((1, width)),
        ],
        out_specs=[pl.BlockSpec((tl, width), lambda i: (i, 0)), pl.BlockSpec((tl, width), lambda i: (i, 0))],
        out_shape=[jax.ShapeDtypeStruct((length, width), F32), jax.ShapeDtypeStruct((length, width), F32)],
        compiler_params=_cparams(("parallel",)),
        name="hyena_filter_mlp",
    )(z, t, w1p, f_b1.reshape(1, hid), f_w2, f_b2.reshape(1, hid), f_w3, f_b3.reshape(1, hid),
      f_w4, f_b4.reshape(1, 2 * width), freq.reshape(1, hid), deltas)


def _fft_plan(length):
    n = 2 * length
    n1 = n // FFT_N2
    nz = n1 // 2
    ku = -(-(nz + 1) // 8) * 8
    return n, n1, nz, ku


def _fft_tables(length):
    n, n1, nz, ku = _fft_plan(length)
    i2 = np.arange(FFT_N2)[:, None, None]
    k1 = np.arange(ku)[None, :, None]
    i1 = np.arange(nz)[None, None, :]
    phi = 2.0 * np.pi * (((FFT_N2 * i1 + i2) * k1) % n) / n
    fwd = np.concatenate([np.cos(phi), -np.sin(phi)], axis=1)
    wgt = np.where((k1 == 0) | (k1 == nz), 1.0, np.where(k1 < nz, 2.0, 0.0)) / n
    inv = np.concatenate([np.cos(phi) * wgt, -np.sin(phi) * wgt], axis=1).transpose(0, 2, 1)
    th = 2.0 * np.pi * ((np.arange(FFT_N2)[:, None] * np.arange(FFT_N2)[None, :]) % FFT_N2) / FFT_N2
    c, s = np.cos(th), np.sin(th)
    f2 = np.block([[c, s], [-s, c]])
    f2i = np.block([[c, -s], [s, c]])
    as_bf = lambda a: jnp.asarray(a, F32).astype(BF16)
    return as_bf(fwd), as_bf(inv), as_bf(f2), as_bf(f2i)


def _fft_stage1(src_ref, fwd_ref, a_ref, *, nz, ku):
    slab = 2 * FFT_N2

    def body(i2, carry):
        rows = src_ref[pl.ds(i2, nz, stride=FFT_N2), :].astype(BF16)
        r = jnp.dot(fwd_ref[i2], rows, preferred_element_type=F32)
        a_ref[pl.ds(i2, ku, stride=slab), :] = r[:ku]
        a_ref[pl.ds(FFT_N2 + i2, ku, stride=slab), :] = r[ku:]
        return carry

    lax.fori_loop(0, FFT_N2, body, 0, unroll=FFT_UNROLL)


def _short_conv(p_ref, w_ref, b_ref, pad_ref, emit, *, length):
    step = min(length, 512)
    cb = p_ref.shape[-1]
    pad_ref[pl.ds(0, 8), :] = jnp.zeros((8, cb), F32)
    pad_ref[pl.ds(length + 8, 8), :] = jnp.zeros((8, cb), F32)
    for j in range(length // step):
        pad_ref[pl.ds(8 + j * step, step), :] = p_ref[pl.ds(j * step, step), :].astype(F32)
    w = w_ref[...]
    for j in range(length // step):
        r0 = j * step
        val = (pad_ref[pl.ds(r0 + 7, step), :] * w[0:1] + pad_ref[pl.ds(r0 + 8, step), :] * w[1:2]
               + pad_ref[pl.ds(r0 + 9, step), :] * w[2:3] + b_ref[...])
        emit(r0, step, val)


def _kf_kernel(kf_ref, kb_ref, fwd_ref, f2_ref, o_ref, a_ref, *, nz, ku):
    slab = 2 * FFT_N2
    for src, sign in ((kf_ref, 1.0), (kb_ref, -1.0)):
        _fft_stage1(src, fwd_ref, a_ref, nz=nz, ku=ku)

        def body(k1, carry, sign=sign, first=(src is kf_ref)):
            a = a_ref[pl.ds(pl.multiple_of(k1 * slab, slab), slab), :].astype(BF16)
            xk = jnp.dot(f2_ref[...], a, preferred_element_type=F32)
            if first:
                o_ref[k1] = xk
            else:
                o_ref[k1, :FFT_N2, :] = o_ref[k1, :FFT_N2, :] + xk[:FFT_N2]
                o_ref[k1, FFT_N2:, :] = o_ref[k1, FFT_N2:, :] - xk[FFT_N2:]
            return carry

        lax.fori_loop(0, ku, body, 0)


def _filter_spectrum(k_fwd, k_bwd, tables):
    length, width = k_fwd.shape
    _, _, nz, ku = _fft_plan(length)
    fwd, _, f2, _ = tables
    blk = pl.BlockSpec((length, FFT_CB), lambda c: (0, c))
    return pl.pallas_call(
        functools.partial(_kf_kernel, nz=nz, ku=ku),
        grid=(width // FFT_CB,),
        in_specs=[
            blk, blk,
            pl.BlockSpec(fwd.shape, lambda c: (0, 0, 0)),
            pl.BlockSpec(f2.shape, lambda c: (0, 0)),
        ],
        out_specs=pl.BlockSpec((ku, 2 * FFT_N2, FFT_CB), lambda c: (0, 0, c)),
        out_shape=jax.ShapeDtypeStruct((ku, 2 * FFT_N2, width), F32),
        scratch_shapes=[pltpu.VMEM((ku * 2 * FFT_N2, FFT_CB), F32)],
        compiler_params=_cparams(("parallel",)),
        name="hyena_filter_spectrum",
    )(k_fwd, k_bwd, fwd, f2)


def _hyena_conv_kernel(px0_ref, px1_ref, pv_ref, w0_ref, w1_ref, wv_ref, b0_ref, b1_ref, bv_ref, dsk_ref,
                       kf_ref, fwd_ref, inv_ref, f2_ref, f2i_ref, o_ref, a_ref, vv_ref, *, length, nz, ku):
    slab = 2 * FFT_N2

    def set_vv(r0, rows, val):
        vv_ref[pl.ds(r0, rows), :] = val

    def mul_vv(r0, rows, val):
        vv_ref[pl.ds(r0, rows), :] = vv_ref[pl.ds(r0, rows), :] * val

    _short_conv(pv_ref, wv_ref, bv_ref, a_ref, set_vv, length=length)
    _short_conv(px1_ref, w1_ref, b1_ref, a_ref, mul_vv, length=length)
    _fft_stage1(vv_ref, fwd_ref, a_ref, nz=nz, ku=ku)

    def freq_body(k1, carry):
        rows = pl.ds(k1 * slab if isinstance(k1, int) else pl.multiple_of(k1 * slab, slab), slab)
        xk = jnp.dot(f2_ref[...], a_ref[rows, :].astype(BF16), preferred_element_type=F32)
        kf = kf_ref[k1]
        xr, xi = xk[:FFT_N2], xk[FFT_N2:]
        kr, ki = kf[:FFT_N2], kf[FFT_N2:]
        prod = jnp.concatenate([xr * kr - xi * ki, xr * ki + xi * kr], axis=0).astype(BF16)
        a_ref[rows, :] = jnp.dot(f2i_ref[...], prod, preferred_element_type=F32)
        return carry

    lax.fori_loop(0, nz, freq_body, 0, unroll=2)
    freq_body(nz, 0)

    def time_body(i2, carry):
        re = a_ref[pl.ds(i2, ku, stride=slab), :]
        im = a_ref[pl.ds(FFT_N2 + i2, ku, stride=slab), :]
        q = jnp.concatenate([re, im], axis=0).astype(BF16)
        y = jnp.dot(inv_ref[i2], q, preferred_element_type=F32)
        rows = pl.ds(i2, nz, stride=FFT_N2)
        vv_ref[rows, :] = y + vv_ref[rows, :] * dsk_ref[...]
        return carry

    lax.fori_loop(0, FFT_N2, time_body, 0, unroll=FFT_UNROLL)

    def emit_out(r0, rows, val):
        o_ref[pl.ds(r0, rows), :] = (vv_ref[pl.ds(r0, rows), :] * val).astype(o_ref.dtype)

    _short_conv(px0_ref, w0_ref, b0_ref, a_ref, emit_out, length=length)


def _hyena_conv(p, conv_w, conv_b, d_skip, kf, tables, length):
    b, _, w3 = p.shape
    width = w3 // 3
    ncb = width // FFT_CB
    _, _, nz, ku = _fft_plan(length)
    fwd, inv, f2, f2i = tables
    pblk = lambda part: _resident((None, length, FFT_CB), lambda c, i: (i, 0, part * ncb + c))
    wblk = lambda part: pl.BlockSpec((3, FFT_CB), lambda c, i: (0, part * ncb + c))
    bblk = lambda part: pl.BlockSpec((1, FFT_CB), lambda c, i: (0, part * ncb + c))
    return pl.pallas_call(
        functools.partial(_hyena_conv_kernel, length=length, nz=nz, ku=ku),
        grid=(ncb, b),
        in_specs=[
            pblk(0), pblk(1), pblk(2), wblk(0), wblk(1), wblk(2), bblk(0), bblk(1), bblk(2),
            pl.BlockSpec((1, FFT_CB), lambda c, i: (0, c)),
            _resident((ku, 2 * FFT_N2, FFT_CB), lambda c, i: (0, 0, c)),
            _resident(fwd.shape, lambda c, i: (0, 0, 0)),
            _resident(inv.shape, lambda c, i: (0, 0, 0)),
            _resident(f2.shape, lambda c, i: (0, 0)),
            _resident(f2i.shape, lambda c, i: (0, 0)),
        ],
        out_specs=pl.BlockSpec((None, length, FFT_CB), lambda c, i: (i, 0, c)),
        out_shape=jax.ShapeDtypeStruct((b, length, width), BF16),
        scratch_shapes=[
            pltpu.VMEM((max(ku * 2 * FFT_N2, length + 16), FFT_CB), F32),
            pltpu.VMEM((length, FFT_CB), F32),
        ],
        compiler_params=_cparams(("arbitrary", "arbitrary")),
        name="hyena_long_conv",
    )(p, p, p, conv_w, conv_w, conv_w, conv_b.reshape(1, w3), conv_b.reshape(1, w3), conv_b.reshape(1, w3),
      d_skip.reshape(1, width), kf, fwd, inv, f2, f2i)


def _dense_dft_tables(length):
    n = 2 * length
    th = 2.0 * np.pi * ((np.arange(n)[:, None] * np.arange(length)[None, :]) % n) / n
    fwd = np.concatenate([np.cos(th), -np.sin(th)], axis=0)
    inv = np.concatenate([np.cos(th), -np.sin(th)], axis=0).T / n
    return jnp.asarray(fwd, F32).astype(BF16), jnp.asarray(inv, F32).astype(BF16)


def _kf_dense_kernel(kf_ref, kb_ref, fwd_ref, o_ref, *, n):
    xf = jnp.dot(fwd_ref[...], kf_ref[...].astype(BF16), preferred_element_type=F32)
    xb = jnp.dot(fwd_ref[...], kb_ref[...].astype(BF16), preferred_element_type=F32)
    o_ref[:n, :] = xf[:n] + xb[:n]
    o_ref[n:, :] = xf[n:] - xb[n:]


def _filter_spectrum_dense(k_fwd, k_bwd, fwd):
    length, width = k_fwd.shape
    n = 2 * length
    blk = pl.BlockSpec((length, FFT_CB), lambda c: (0, c))
    return pl.pallas_call(
        functools.partial(_kf_dense_kernel, n=n),
        grid=(width // FFT_CB,),
        in_specs=[blk, blk, pl.BlockSpec(fwd.shape, lambda c: (0, 0))],
        out_specs=pl.BlockSpec((2 * n, FFT_CB), lambda c: (0, c)),
        out_shape=jax.ShapeDtypeStruct((2 * n, width), F32),
        compiler_params=_cparams(("parallel",)),
        name="hyena_ctx_filter_spectrum",
    )(k_fwd, k_bwd, fwd)


def _hyena_ctx_kernel(px0_ref, px1_ref, pv_ref, w0_ref, w1_ref, wv_ref, b0_ref, b1_ref, bv_ref, dsk_ref,
                      kf_ref, fwd_ref, inv_ref, o_ref, pad_ref, vv_ref, *, length):
    n = 2 * length

    def set_vv(r0, rows, val):
        vv_ref[pl.ds(r0, rows), :] = val

    def mul_vv(r0, rows, val):
        vv_ref[pl.ds(r0, rows), :] = vv_ref[pl.ds(r0, rows), :] * val

    _short_conv(pv_ref, wv_ref, bv_ref, pad_ref, set_vv, length=length)
    _short_conv(px1_ref, w1_ref, b1_ref, pad_ref, mul_vv, length=length)
    vv = vv_ref[...]
    xk = jnp.dot(fwd_ref[...], vv.astype(BF16), preferred_element_type=F32)
    kf = kf_ref[...]
    xr, xi, kr, ki = xk[:n], xk[n:], kf[:n], kf[n:]
    prod = jnp.concatenate([xr * kr - xi * ki, xr * ki + xi * kr], axis=0).astype(BF16)
    vv_ref[...] = jnp.dot(inv_ref[...], prod, preferred_element_type=F32) + vv * dsk_ref[...]

    def emit_out(r0, rows, val):
        o_ref[pl.ds(r0, rows), :] = (vv_ref[pl.ds(r0, rows), :] * val).astype(o_ref.dtype)

    _short_conv(px0_ref, w0_ref, b0_ref, pad_ref, emit_out, length=length)


def _hyena_ctx_conv(p, conv_w, conv_b, d_skip, kf, fwd, inv, n_lat, length):
    b, _, w3 = p.shape
    width = w3 // 3
    ncb = width // FFT_CB
    rb = n_lat // length
    pblk = lambda part: pl.BlockSpec((None, length, FFT_CB), lambda c, i: (i, rb, part * ncb + c))
    wblk = lambda part: pl.BlockSpec((3, FFT_CB), lambda c, i: (0, part * ncb + c))
    bblk = lambda part: pl.BlockSpec((1, FFT_CB), lambda c, i: (0, part * ncb + c))
    return pl.pallas_call(
        functools.partial(_hyena_ctx_kernel, length=length),
        grid=(ncb, b),
        in_specs=[
            pblk(0), pblk(1), pblk(2), wblk(0), wblk(1), wblk(2), bblk(0), bblk(1), bblk(2),
            pl.BlockSpec((1, FFT_CB), lambda c, i: (0, c)),
            pl.BlockSpec((4 * length, FFT_CB), lambda c, i: (0, c)),
            pl.BlockSpec(fwd.shape, lambda c, i: (0, 0)),
            pl.BlockSpec(inv.shape, lambda c, i: (0, 0)),
        ],
        out_specs=pl.BlockSpec((None, length, FFT_CB), lambda c, i: (i, 0, c)),
        out_shape=jax.ShapeDtypeStruct((b, length, width), BF16),
        scratch_shapes=[pltpu.VMEM((length + 16, FFT_CB), F32), pltpu.VMEM((length, FFT_CB), F32)],
        compiler_params=_cparams(("arbitrary", "arbitrary")),
        name="hyena_ctx_conv",
    )(p, p, p, conv_w, conv_w, conv_w, conv_b.reshape(1, w3), conv_b.reshape(1, w3), conv_b.reshape(1, w3),
      d_skip.reshape(1, width), kf, fwd, inv)


def _row_copy(src, dst, sem):
    return pltpu.make_async_copy(src, dst, sem)


def _scatter_kernel(cnt_ref, start_ref, dest_ref, h_ref, x_hbm, src0, src1, src2, src3, zero_scr, sem, zsem, *, n_tok):
    rows, half = h_ref.shape
    src_scr = (src0, src1, src2, src3)
    n_real = TOP_K * n_tok
    row_iota = lax.broadcasted_iota(jnp.int32, (rows, LANES), 0)

    def ids_from(base):
        return pltpu.bitcast(base + row_iota, jnp.uint32)

    @pl.when(pl.program_id(0) == 0)
    def _():
        zero_scr[:, :half] = jnp.zeros((rows, half), jnp.uint32)

        def per_expert(e, total):
            cnt = cnt_ref[e]
            n_pad = (EXPERT_BLOCK - cnt % EXPERT_BLOCK) % EXPERT_BLOCK
            base = start_ref[e] + cnt
            zero_scr[:, half:] = ids_from(n_real + total)

            def one(r, carry):
                _row_copy(zero_scr.at[pl.ds(r, 1)], x_hbm.at[pl.ds(base + r, 1)], zsem).start()
                return carry

            def wait_one(r, carry):
                _row_copy(zero_scr.at[pl.ds(0, 1)], x_hbm.at[pl.ds(0, 1)], zsem).wait()
                return carry

            lax.fori_loop(0, n_pad, one, 0)
            lax.fori_loop(0, n_pad, wait_one, 0)
            return total + n_pad

        total = lax.fori_loop(0, N_EXPERTS, per_expert, 0)
        first_free = start_ref[N_EXPERTS] // EXPERT_BLOCK
        n_blocks = x_hbm.shape[0] // EXPERT_BLOCK

        def tail(i, carry):
            zero_scr[:, half:] = ids_from(n_real + total + (i - first_free) * EXPERT_BLOCK)
            cp = _row_copy(zero_scr, x_hbm.at[pl.ds(pl.multiple_of(i * EXPERT_BLOCK, EXPERT_BLOCK), EXPERT_BLOCK)],
                           zsem)
            cp.start()
            cp.wait()
            return carry

        lax.fori_loop(first_free, n_blocks, tail, 0)

    h = h_ref[...]
    for k in range(TOP_K):
        src_scr[k][:, :half] = h
        src_scr[k][:, half:] = ids_from(k * n_tok + pl.program_id(0) * rows)

    def issue(r, carry):
        for k in range(TOP_K):
            _row_copy(src_scr[k].at[pl.ds(r, 1)], x_hbm.at[pl.ds(dest_ref[k, r], 1)], sem).start()
        return carry

    lax.fori_loop(0, rows, issue, 0, unroll=4)
    for k in range(TOP_K):
        _row_copy(src_scr[k], x_hbm.at[pl.ds(0, rows)], sem).wait()


def _dispatch(h2p, dest_tiles, counts, pad_start, n_blocks):
    n_tok, half = h2p.shape
    assert TM == EXPERT_BLOCK
    width = half + LANES
    return pl.pallas_call(
        functools.partial(_scatter_kernel, n_tok=n_tok),
        grid_spec=pltpu.PrefetchScalarGridSpec(
            num_scalar_prefetch=2,
            grid=(n_tok // TM,),
            in_specs=[
                pl.BlockSpec((None, TOP_K, TM), lambda i, c, s: (i, 0, 0), memory_space=pltpu.SMEM),
                pl.BlockSpec((TM, half), lambda i, c, s: (i, 0)),
            ],
            out_specs=pl.BlockSpec(memory_space=pl.ANY),
            scratch_shapes=[pltpu.VMEM((TM, width), jnp.uint32)] * TOP_K + [
                pltpu.VMEM((EXPERT_BLOCK, width), jnp.uint32), pltpu.SemaphoreType.DMA(()), pltpu.SemaphoreType.DMA(())],
        ),
        out_shape=jax.ShapeDtypeStruct((n_blocks * EXPERT_BLOCK, width), jnp.uint32),
        compiler_params=_cparams(("arbitrary",)),
        name="moe_dispatch",
    )(counts, pad_start, dest_tiles, h2p)


def _ffn_kernel(be_ref, ids_prev_ref, ids_last_ref, x_ref, w1_ref, b1_ref, w2_ref, b2_ref, y_hbm,
                buf0, buf1, w1_bf, w2_bf, sem0, sem1, *, d_ff, half):
    i = pl.program_id(0)
    n = pl.num_programs(0)
    rows = x_ref.shape[0]
    spare = y_hbm.shape[0] - 2 * rows

    def drain(src, sem, ids_ref):
        for r in range(rows):
            _row_copy(src.at[pl.ds(r, 1)], y_hbm.at[pl.ds(ids_ref[0, r], 1)], sem).start()

    def wait(src, sem):
        _row_copy(src, y_hbm.at[pl.ds(0, rows)], sem).wait()

    @pl.when(i == 0)
    def _():
        buf0[...] = jnp.zeros(buf0.shape, buf0.dtype)
        buf1[...] = jnp.zeros(buf1.shape, buf1.dtype)
        _row_copy(buf0, y_hbm.at[pl.ds(spare + rows, rows)], sem0).start()

    @pl.when(jnp.logical_or(i == 0, be_ref[i] != be_ref[jnp.maximum(i - 1, 0)]))
    def _():
        w1_bf[...] = w1_ref[...].astype(BF16)
        w2_bf[...] = w2_ref[...].astype(BF16)

    def step(src, ssem, dst, dsem):
        drain(src, ssem, ids_prev_ref)
        x = _unpack_bf16_pairs(x_ref[:, :half]).astype(BF16)
        hh = jnp.dot(x, w1_bf[...], preferred_element_type=F32) + b1_ref[...]
        glu = jnp.minimum(hh[:, :d_ff], SWIGLU_LIMIT)
        lin = jnp.clip(hh[:, d_ff:], -SWIGLU_LIMIT, SWIGLU_LIMIT)
        act = (glu * jax.nn.sigmoid(SWIGLU_ALPHA * glu) * (lin + 1.0)).astype(BF16)
        y = _pack_bf16_pairs(jnp.dot(act, w2_bf[...], preferred_element_type=F32) + b2_ref[...])
        wait(dst, dsem)
        dst[...] = y

    @pl.when(i % 2 == 0)
    def _():
        step(buf1, sem1, buf0, sem0)

    @pl.when(i % 2 == 1)
    def _():
        step(buf0, sem0, buf1, sem1)

    @pl.when(i == n - 1)
    def _():
        last_even = (y_hbm.shape[0] // rows - 3) % 2 == 0
        last, lsem, prev, psem = (buf0, sem0, buf1, sem1) if last_even else (buf1, sem1, buf0, sem0)
        drain(last, lsem, ids_last_ref)
        wait(prev, psem)
        wait(last, lsem)


def _expert_ffn(x_disp, ids, block_e, layer, w1, b1, w2, b2):
    rows, width = x_disp.shape
    half = width - LANES
    depth, n_e, d, ff2 = w1.shape
    d_ff = ff2 // 2
    n_blocks = rows // EXPERT_BLOCK
    ids_prev = jnp.concatenate([rows + jnp.arange(EXPERT_BLOCK, dtype=jnp.int32), ids])
    ids_prev = ids_prev.reshape(n_blocks + 1, 1, EXPERT_BLOCK)
    return pl.pallas_call(
        functools.partial(_ffn_kernel, d_ff=d_ff, half=half),
        grid_spec=pltpu.PrefetchScalarGridSpec(
            num_scalar_prefetch=1,
            grid=(n_blocks,),
            in_specs=[
                pl.BlockSpec((None, 1, EXPERT_BLOCK), lambda i, be: (i, 0, 0), memory_space=pltpu.SMEM),
                pl.BlockSpec((None, 1, EXPERT_BLOCK), lambda i, be: (n_blocks, 0, 0), memory_space=pltpu.SMEM),
                pl.BlockSpec((EXPERT_BLOCK, width), lambda i, be: (i, 0)),
                pl.BlockSpec((None, None, d, ff2), lambda i, be: (layer, be[i], 0, 0)),
                pl.BlockSpec((None, None, 1, ff2), lambda i, be: (layer, be[i], 0, 0)),
                pl.BlockSpec((None, None, d_ff, d), lambda i, be: (layer, be[i], 0, 0)),
                pl.BlockSpec((None, None, 1, d), lambda i, be: (layer, be[i], 0, 0)),
            ],
            out_specs=pl.BlockSpec(memory_space=pl.ANY),
            scratch_shapes=[pltpu.VMEM((EXPERT_BLOCK, half), jnp.uint32), pltpu.VMEM((EXPERT_BLOCK, half), jnp.uint32),
                            pltpu.VMEM((d, ff2), BF16), pltpu.VMEM((d_ff, d), BF16),
                            pltpu.SemaphoreType.DMA(()), pltpu.SemaphoreType.DMA(())],
        ),
        out_shape=jax.ShapeDtypeStruct((rows + 2 * EXPERT_BLOCK, half), jnp.uint32),
        compiler_params=_cparams(("arbitrary",)),
        name="moe_expert_ffn",
    )(block_e, ids_prev, ids_prev, x_disp, w1, b1.reshape(depth, n_e, 1, ff2), w2, b2.reshape(depth, n_e, 1, d))


def _combine_kernel(y0_ref, y1_ref, y2_ref, y3_ref, gate_ref, x_ref, mod_ref, fg_ref, o_ref, *, d, final):
    gate = gate_ref[...]
    out = _unpack_bf16_pairs(y0_ref[...]) * gate[:, 0:1]
    for k, y_ref in enumerate((y1_ref, y2_ref, y3_ref), start=1):
        out = out + _unpack_bf16_pairs(y_ref[...]) * gate[:, k:k + 1]
    xn = x_ref[...] + mod_ref[:, 5 * d:6 * d] * out
    if final:
        xn = xn * lax.rsqrt(jnp.mean(xn * xn, axis=-1, keepdims=True) + EPS) * fg_ref[...]
    o_ref[...] = xn


def _combine(y_rows, gates, x, mod, final_g, n_lat_tiles, final):
    b, t, d = x.shape
    nt = t // TM
    n_tiles = b * nt
    nt_out = n_lat_tiles if final else nt
    y_spec = lambda k: pl.BlockSpec((TM, d // 2), lambda i, j: (k * n_tiles + i * nt + j, 0))
    return pl.pallas_call(
        functools.partial(_combine_kernel, d=d, final=final),
        grid=(b, nt_out),
        in_specs=[
            y_spec(0), y_spec(1), y_spec(2), y_spec(3),
            pl.BlockSpec((TM, TOP_K), lambda i, j: (i * nt + j, 0)),
            pl.BlockSpec((None, TM, d), lambda i, j: (i, j, 0)),
            pl.BlockSpec((None, 1, 6 * d), lambda i, j: (jnp.where(j >= n_lat_tiles, b, i), 0, 0)),
            pl.BlockSpec((1, d), lambda i, j: (0, 0)),
        ],
        out_specs=pl.BlockSpec((None, TM, d), lambda i, j: (i, j, 0)),
        out_shape=jax.ShapeDtypeStruct((b, nt_out * TM, d), F32),
        compiler_params=_cparams(("parallel", "parallel")),
        name="moe_combine",
    )(y_rows, y_rows, y_rows, y_rows, gates, x, mod, final_g.reshape(1, d))


def _moe(h2p, idx_t, gates_t, rank_t, cnt, x, mod, layer, w1, b1, w2, b2, final_g, n_lat_tiles, final):
    b, t, d = x.shape
    n_tok = b * t
    n_assign = n_tok * TOP_K
    counts = cnt[:, 0]
    padded = (counts + EXPERT_BLOCK - 1) // EXPERT_BLOCK * EXPERT_BLOCK
    pad_end = jnp.cumsum(padded).astype(jnp.int32)
    pad_start = pad_end - padded
    experts = jnp.arange(N_EXPERTS, dtype=jnp.int32)
    dest_t = rank_t + jnp.sum(jnp.where(idx_t[..., None] == experts, pad_start, 0), axis=-1)
    n_blocks = -(-n_assign // EXPERT_BLOCK) + N_EXPERTS
    block_start = jnp.arange(n_blocks, dtype=jnp.int32) * EXPERT_BLOCK
    block_e = jnp.minimum(jnp.sum(pad_end[None, :] <= block_start[:, None], axis=1), N_EXPERTS - 1).astype(jnp.int32)
    dest_tiles = dest_t.reshape(TOP_K, n_tok // TM, TM).transpose(1, 0, 2)

    x_disp = _dispatch(h2p.reshape(n_tok, d // 2), dest_tiles, counts,
                       jnp.concatenate([pad_start, pad_end[-1:]]), n_blocks)
    ids = lax.bitcast_convert_type(x_disp[:, d // 2], jnp.int32)
    y_rows = _expert_ffn(x_disp, ids, block_e, layer, w1, b1, w2, b2)
    return _combine(y_rows, gates_t.T, x, mod, final_g, n_lat_tiles, final)


def _rope_tables(n_lat, n_ctx):
    rows = n_lat // GRID_W
    row = jnp.repeat(jnp.arange(rows, dtype=F32), GRID_W)
    col = jnp.tile(jnp.arange(GRID_W, dtype=F32), rows)
    axis_dim = HEAD_DIM // 2
    inv = ROPE_THETA ** (-jnp.arange(0, axis_dim, 2, dtype=F32) / axis_dim)
    ang_r = row[:, None] * inv
    ang_c = col[:, None] * inv
    ang = jnp.concatenate([ang_r, ang_r, ang_c, ang_c], axis=-1)
    cos = jnp.concatenate([jnp.cos(ang), jnp.ones((n_ctx, HEAD_DIM), F32)], axis=0)
    sin = jnp.concatenate([jnp.sin(ang), jnp.zeros((n_ctx, HEAD_DIM), F32)], axis=0)
    even = (jnp.arange(HEAD_DIM) // 32) % 2 == 0
    return cos, jnp.where(even, -sin, 0.0), jnp.where(even, 0.0, sin)


def kernel(x, c, ctx, c_ctx, ada_w, ada_b, norm1_g, norm2_g, mix_w_out, router_w, router_b, exp_w1, exp_b1, exp_w2, exp_b2, attn_w_in, attn_q_gain, attn_k_gain, gmlp_w_in, gmlp_v_gain, gmlp_w_s, gmlp_b_s, hyena_w_in, hyena_conv_w, hyena_conv_b, hyena_f_w1, hyena_f_b1, hyena_f_w2, hyena_f_b2, hyena_f_w3, hyena_f_b3, hyena_f_w4, hyena_f_b4, hyena_freq, hyena_d, final_g):
    b, s, d = x.shape
    n_ctx = ctx.shape[1]
    depth = ada_w.shape[0]
    assert s % TM == 0 and n_ctx == TM and b < MOD_ROWS and s % (FFT_N2 * 8) == 0
    n_lat_tiles = s // TM

    xs = jnp.concatenate([x, ctx], axis=1)
    cc = jnp.concatenate([c, c_ctx[None, :], jnp.zeros((MOD_ROWS - b - 1, d), F32)], axis=0)
    mods = _modulation(cc, ada_w, ada_b).reshape(depth, MOD_ROWS, 1, 6 * d)
    rope = _rope_tables(s, n_ctx)

    for i in range(depth):
        kind, j = i % 3, i // 3
        last = i == depth - 1
        mod = mods[i]
        w_out = mix_w_out[i].astype(BF16)
        if kind == 0:
            q, k, v = _attn_proj(xs, mod, norm1_g[i], attn_w_in[j].astype(BF16), attn_q_gain[j], attn_k_gain[j],
                                 rope, n_lat_tiles)
            y = _attention(q, k, v, s)
        elif kind == 2:
            p = _inproj(xs, mod, norm1_g[i], hyena_w_in[j].astype(BF16), n_lat_tiles)
            fargs = (hyena_f_w1[j], hyena_f_b1[j], hyena_f_w2[j], hyena_f_b2[j], hyena_f_w3[j], hyena_f_b3[j],
                     hyena_f_w4[j], hyena_f_b4[j], hyena_freq[j])
            tables = _fft_tables(s)
            kf = _filter_spectrum(*_hyena_filters(s, *fargs), tables)
            y_lat = _hyena_conv(p, hyena_conv_w[j], hyena_conv_b[j], hyena_d[j], kf, tables, s)
            dfwd, dinv = _dense_dft_tables(n_ctx)
            kf_c = _filter_spectrum_dense(*_hyena_filters(n_ctx, *fargs), dfwd)
            y_ctx = _hyena_ctx_conv(p, hyena_conv_w[j], hyena_conv_b[j], hyena_d[j], kf_c, dfwd, dinv, s, n_ctx)
            y = jnp.concatenate([y_lat, y_ctx], axis=1)
        if kind == 1:
            xs, *routed = _gmlp_layer(
                xs, mod, norm1_g[i], gmlp_w_in[j].astype(BF16), gmlp_v_gain[j], gmlp_w_s[j].astype(BF16),
                gmlp_b_s[j], w_out, norm2_g[i], router_w[i], router_b[i], n_lat_tiles)
        else:
            xs, *routed = _post(y, xs, mod, w_out, norm2_g[i], router_w[i], router_b[i], n_lat_tiles)
        xs = _moe(*routed, xs, mod, i, exp_w1, exp_b1, exp_w2, exp_b2, final_g, n_lat_tiles, last)
    return xs
```

```python
import functools
import math

import jax
import jax.numpy as jnp
import numpy as np
from jax import lax
from jax.experimental import pallas as pl
from jax.experimental.pallas import tpu as pltpu

F32 = jnp.float32
BF16 = jnp.bfloat16
HIGHEST = lax.Precision.HIGHEST

EPS = 1e-6
GRID_W = 64
ROPE_THETA = 10000.0
HEAD_DIM = 128
N_KV_HEADS = 2
Q_GROUP = 4
CHUNK = 128
GMLP_GROUPS = 8
N_EXPERTS = 32
TOP_K = 4
SWIGLU_ALPHA = 1.702
SWIGLU_LIMIT = 7.0
N_BANDS = 16
DECAY_TARGET = 1e-2
MAX_DECAY = math.log(DECAY_TARGET) / 0.3
MIN_DECAY = math.log(DECAY_TARGET) / 1.5

LANES = 128
TM = 256
ATTN_TK = 512
MOD_ROWS = 16
EXPERT_BLOCK = 256
FFT_N2 = 128
FFT_CB = 128
FFT_UNROLL = 8
VMEM_LIMIT = 56 * 1024 * 1024


def _cparams(sem):
    return pltpu.CompilerParams(dimension_semantics=sem, vmem_limit_bytes=VMEM_LIMIT)


def _norm_mod(x, g, scale, shift):
    y = x * lax.rsqrt(jnp.mean(x * x, axis=-1, keepdims=True) + EPS)
    return y * g * (1.0 + scale) + shift


def _resident(shape, index_map):
    return pl.BlockSpec(shape, index_map, pipeline_mode=pl.Buffered(1))


def _mod_kernel(c_ref, w_ref, b_ref, o_ref):
    c = c_ref[...]
    s = c * jax.nn.sigmoid(c)
    o_ref[...] = jnp.dot(s, w_ref[...], precision=HIGHEST, preferred_element_type=F32) + b_ref[...]


def _modulation(cc, ada_w, ada_b):
    depth, d, d6 = ada_w.shape
    tn = 1536
    return pl.pallas_call(
        _mod_kernel,
        grid=(depth, d6 // tn),
        in_specs=[
            pl.BlockSpec((MOD_ROWS, d), lambda l, j: (0, 0)),
            pl.BlockSpec((None, d, tn), lambda l, j: (l, 0, j)),
            pl.BlockSpec((None, 1, tn), lambda l, j: (l, 0, j)),
        ],
        out_specs=pl.BlockSpec((None, MOD_ROWS, tn), lambda l, j: (l, 0, j)),
        out_shape=jax.ShapeDtypeStruct((depth, MOD_ROWS, d6), F32),
        compiler_params=_cparams(("arbitrary", "arbitrary")),
        name="adaln_mod",
    )(cc, ada_w, ada_b.reshape(depth, 1, d6))


def _inproj_kernel(x_ref, mod_ref, g_ref, w_ref, o_ref, *, d):
    mod = mod_ref[...]
    h = _norm_mod(x_ref[...], g_ref[...], mod[:, d:2 * d], mod[:, 0:d]).astype(BF16)
    o_ref[...] = jnp.dot(h, w_ref[...], preferred_element_type=F32).astype(o_ref.dtype)


def _inproj(x, mod, g, w, n_lat_tiles):
    b, t, d = x.shape
    n_out = w.shape[1]
    return pl.pallas_call(
        functools.partial(_inproj_kernel, d=d),
        grid=(b, t // TM),
        in_specs=[
            pl.BlockSpec((None, TM, d), lambda i, j: (i, j, 0)),
            pl.BlockSpec((None, 1, 6 * d), lambda i, j: (jnp.where(j >= n_lat_tiles, b, i), 0, 0)),
            pl.BlockSpec((1, d), lambda i, j: (0, 0)),
            pl.BlockSpec((d, n_out), lambda i, j: (0, 0)),
        ],
        out_specs=pl.BlockSpec((None, TM, n_out), lambda i, j: (i, j, 0)),
        out_shape=jax.ShapeDtypeStruct((b, t, n_out), BF16),
        compiler_params=_cparams(("parallel", "parallel")),
        name="inproj",
    )(x, mod, g.reshape(1, d), w)


def _attn_proj_kernel(x_ref, mod_ref, g_ref, w_ref, qg_ref, kg_ref, cos_ref, sa_ref, sb_ref,
                      q_ref, k_ref, v_ref, *, d):
    mod = mod_ref[...]
    h = _norm_mod(x_ref[...], g_ref[...], mod[:, d:2 * d], mod[:, 0:d]).astype(BF16)
    p = jnp.dot(h, w_ref[...], preferred_element_type=F32)
    cos, sin_a, sin_b = cos_ref[...], sa_ref[...], sb_ref[...]
    n_q = Q_GROUP * N_KV_HEADS

    def head(ph, gain):
        y = ph * lax.rsqrt(jnp.mean(ph * ph, axis=-1, keepdims=True) + EPS) * gain
        return y * cos + pltpu.roll(y, LANES - 32, 1) * sin_a + pltpu.roll(y, 32, 1) * sin_b

    qg = qg_ref[...] * (HEAD_DIM ** -0.5 * math.log2(math.e))
    kg = kg_ref[...]
    for i in range(n_q):
        q_ref[:, i * HEAD_DIM:(i + 1) * HEAD_DIM] = head(p[:, i * HEAD_DIM:(i + 1) * HEAD_DIM], qg).astype(BF16)
    k0 = n_q * HEAD_DIM
    for i in range(N_KV_HEADS):
        k_ref[:, i * HEAD_DIM:(i + 1) * HEAD_DIM] = head(
            p[:, k0 + i * HEAD_DIM:k0 + (i + 1) * HEAD_DIM], kg).astype(BF16)
    v0 = k0 + N_KV_HEADS * HEAD_DIM
    for i in range(N_KV_HEADS):
        v_ref[:, 2 * i * HEAD_DIM:(2 * i + 1) * HEAD_DIM] = p[:, v0 + i * HEAD_DIM:v0 + (i + 1) * HEAD_DIM].astype(BF16)
        v_ref[:, (2 * i + 1) * HEAD_DIM:(2 * i + 2) * HEAD_DIM] = jnp.ones((p.shape[0], HEAD_DIM), BF16)


def _attn_proj(x, mod, g, w, q_gain, k_gain, rope, n_lat_tiles):
    b, t, d = x.shape
    n_out = w.shape[1]
    qw = Q_GROUP * N_KV_HEADS * HEAD_DIM
    kw = N_KV_HEADS * HEAD_DIM
    row = lambda i, j: (i, j, 0)
    tab = pl.BlockSpec((TM, HEAD_DIM), lambda i, j: (j, 0))
    return pl.pallas_call(
        functools.partial(_attn_proj_kernel, d=d),
        grid=(b, t // TM),
        in_specs=[
            pl.BlockSpec((None, TM, d), row),
            pl.BlockSpec((None, 1, 6 * d), lambda i, j: (jnp.where(j >= n_lat_tiles, b, i), 0, 0)),
            pl.BlockSpec((1, d), lambda i, j: (0, 0)),
            pl.BlockSpec((d, n_out), lambda i, j: (0, 0)),
            pl.BlockSpec((1, HEAD_DIM), lambda i, j: (0, 0)),
            pl.BlockSpec((1, HEAD_DIM), lambda i, j: (0, 0)),
            tab, tab, tab,
        ],
        out_specs=[
            pl.BlockSpec((None, TM, qw), row),
            pl.BlockSpec((None, TM, kw), row),
            pl.BlockSpec((None, TM, 2 * kw), row),
        ],
        out_shape=[
            jax.ShapeDtypeStruct((b, t, qw), BF16),
            jax.ShapeDtypeStruct((b, t, kw), BF16),
            jax.ShapeDtypeStruct((b, t, 2 * kw), BF16),
        ],
        compiler_params=_cparams(("parallel", "parallel")),
        name="attn_proj",
    )(x, mod, g.reshape(1, d), w, q_gain.reshape(1, HEAD_DIM), k_gain.reshape(1, HEAD_DIM), *rope)


def _attn_kernel(q_ref, k_ref, v_ref, o_ref, q_scr, s0, s1, p0, p1, a0, a1, m_scr, acc_scr, *, n_lat, n_ctx, tk):
    is_ctx = pl.program_id(2) * TM >= n_lat
    s_b, p_b, a_b = (s0, s1), (p0, p1), (a0, a1)
    for g in range(Q_GROUP):
        q_scr[g * TM:(g + 1) * TM, :] = q_ref[:, g * HEAD_DIM:(g + 1) * HEAD_DIM]
    m_scr[...] = jnp.full(m_scr.shape, -jnp.inf, F32)
    acc_scr[...] = jnp.zeros(acc_scr.shape, F32)

    def scores(start, size, slot):
        k = k_ref[pl.ds(start, size), :]
        s_b[slot][:, :size] = lax.dot_general(q_scr[...], k, (((1,), (1,)), ((), ())), preferred_element_type=F32)

    def softmax(size, slot):
        nb = size // LANES
        s = s_b[slot][:, :size]
        mx = s[:, 0:LANES]
        for j in range(1, nb):
            mx = jnp.maximum(mx, s[:, j * LANES:(j + 1) * LANES])
        m_prev = m_scr[...]
        m_new = jnp.maximum(m_prev, jnp.max(mx, axis=-1, keepdims=True))
        a_b[slot][...] = jnp.exp2(m_prev - m_new)
        m_scr[...] = m_new
        m_rep = jnp.concatenate([m_new] * nb, axis=1) if nb > 1 else m_new
        p_b[slot][:, :size] = jnp.exp2(s - m_rep).astype(BF16)

    def values(start, size, slot):
        a = a_b[slot][...]
        acc_scr[...] = jnp.concatenate([a, a], axis=1) * acc_scr[...] + jnp.dot(
            p_b[slot][:, :size], v_ref[pl.ds(start, size), :], preferred_element_type=F32)

    scores(n_lat, n_ctx, 0)
    softmax(n_ctx, 0)
    values(n_lat, n_ctx, 0)

    n = n_lat // tk

    @pl.when(jnp.logical_not(is_ctx))
    def _():
        scores(0, tk, 0)
        softmax(tk, 0)
        scores(tk, tk, 1)

        def pair(i, carry):
            j = 2 + 2 * i
            values(pl.multiple_of((j - 2) * tk, tk), tk, 0)
            softmax(tk, 1)
            scores(pl.multiple_of(j * tk, tk), tk, 0)
            values(pl.multiple_of((j - 1) * tk, tk), tk, 1)
            softmax(tk, 0)
            scores(pl.multiple_of((j + 1) * tk, tk), tk, 1)
            return carry

        lax.fori_loop(0, (n - 2) // 2, pair, 0)
        values((n - 2) * tk, tk, 0)
        softmax(tk, 1)
        values((n - 1) * tk, tk, 1)

    out = acc_scr[:, :HEAD_DIM] / acc_scr[:, HEAD_DIM:]
    for g in range(Q_GROUP):
        o_ref[:, g * HEAD_DIM:(g + 1) * HEAD_DIM] = out[g * TM:(g + 1) * TM].astype(o_ref.dtype)


def _attention(q, k, v, n_lat):
    b, t, _ = q.shape
    n_ctx = t - n_lat
    gw = Q_GROUP * HEAD_DIM
    tk = ATTN_TK
    assert n_lat % (2 * tk) == 0 and n_ctx <= tk and n_ctx % LANES == 0
    m = Q_GROUP * TM
    return pl.pallas_call(
        functools.partial(_attn_kernel, n_lat=n_lat, n_ctx=n_ctx, tk=tk),
        grid=(b, N_KV_HEADS, t // TM),
        in_specs=[
            pl.BlockSpec((None, TM, gw), lambda i, h, j: (i, j, h)),
            pl.BlockSpec((None, t, HEAD_DIM), lambda i, h, j: (i, 0, h)),
            pl.BlockSpec((None, t, 2 * HEAD_DIM), lambda i, h, j: (i, 0, h)),
        ],
        out_specs=pl.BlockSpec((None, TM, gw), lambda i, h, j: (i, j, h)),
        out_shape=jax.ShapeDtypeStruct(q.shape, BF16),
        scratch_shapes=[
            pltpu.VMEM((m, HEAD_DIM), BF16),
            pltpu.VMEM((m, tk), F32), pltpu.VMEM((m, tk), F32),
            pltpu.VMEM((m, tk), BF16), pltpu.VMEM((m, tk), BF16),
            pltpu.VMEM((m, LANES), F32), pltpu.VMEM((m, LANES), F32),
            pltpu.VMEM((m, LANES), F32),
            pltpu.VMEM((m, 2 * HEAD_DIM), F32),
        ],
        compiler_params=_cparams(("parallel", "parallel", "parallel")),
        name="attention",
    )(q, k, v)


def _pack_bf16_pairs(a):
    n = a.shape[1] // 2
    bits = pltpu.bitcast(a.astype(BF16).astype(F32), jnp.uint32)
    return (bits[:, :n] >> 16) | (bits[:, n:] & jnp.uint32(0xFFFF0000))


def _unpack_bf16_pairs(w):
    lo = pltpu.bitcast(w << 16, F32)
    hi = pltpu.bitcast(w & jnp.uint32(0xFFFF0000), F32)
    return jnp.concatenate([lo, hi], axis=1)


def _residual_router(y, x, mod, w_out, g2n, rwt, rb, outs, cnt_scr, *, d):
    xo_ref, h2_ref, idx_ref, gate_ref, rank_ref, cnt_ref = outs
    first = jnp.logical_and(pl.program_id(0) == 0, pl.program_id(1) == 0)

    @pl.when(first)
    def _():
        cnt_scr[...] = jnp.zeros(cnt_scr.shape, F32)

    xn = x + mod[:, 2 * d:3 * d] * jnp.dot(y, w_out, preferred_element_type=F32)
    xo_ref[...] = xn
    h2 = _norm_mod(xn, g2n, mod[:, 4 * d:5 * d], mod[:, 3 * d:4 * d])
    h2_ref[...] = _pack_bf16_pairs(h2)
    lg = lax.dot_general(rwt, h2, (((1,), (1,)), ((), ())), precision=HIGHEST,
                         preferred_element_type=F32) + rb
    row = lax.broadcasted_iota(jnp.int32, lg.shape, 0)
    vals, idxs = [], []
    for _ in range(TOP_K):
        m = jnp.max(lg, axis=0, keepdims=True)
        i = jnp.min(jnp.where(lg == m, row, N_EXPERTS), axis=0, keepdims=True)
        vals.append(m)
        idxs.append(i)
        lg = jnp.where(row == i, -jnp.inf, lg)
    es = [jnp.exp(vv - vals[0]) for vv in vals]
    tot = es[0] + es[1] + es[2] + es[3]
    idx_ref[...] = jnp.concatenate(idxs, axis=0)
    gate_ref[...] = jnp.concatenate([e / tot for e in es], axis=0)

    tm = lg.shape[1]
    earlier = (lax.broadcasted_iota(jnp.int32, (tm, tm), 0) < lax.broadcasted_iota(jnp.int32, (tm, tm), 1)).astype(BF16)
    run = cnt_scr[...]
    ranks = []
    for i in idxs:
        hit = row == i
        before = jnp.dot(hit.astype(BF16), earlier, preferred_element_type=F32)
        ranks.append(jnp.sum(jnp.where(hit, run + before, 0.0), axis=0, keepdims=True))
        run = run + jnp.sum(hit.astype(F32), axis=1, keepdims=True)
    cnt_scr[...] = run
    rank_ref[...] = jnp.concatenate(ranks, axis=0).astype(jnp.int32)
    cnt_ref[...] = run.astype(jnp.int32)


def _post_kernel(y_ref, x_ref, mod_ref, w_ref, g_ref, rwt_ref, rb_ref, *rest, d):
    _residual_router(y_ref[...], x_ref[...], mod_ref[...], w_ref[...], g_ref[...], rwt_ref[...], rb_ref[...],
                     rest[:-1], rest[-1], d=d)


def _post_specs(b, t, d, n_lat_tiles):
    nt = t // TM
    row = lambda i, j: (i, j, 0)
    const2 = lambda i, j: (0, 0)
    in_tail = [
        pl.BlockSpec((d, d), const2),
        pl.BlockSpec((1, d), const2),
        pl.BlockSpec((N_EXPERTS, d), const2),
        pl.BlockSpec((N_EXPERTS, 1), const2),
    ]
    per_token = pl.BlockSpec((TOP_K, TM), lambda i, j: (0, i * nt + j))
    out_specs = [
        pl.BlockSpec((None, TM, d), row),
        pl.BlockSpec((None, TM, d // 2), row),
        per_token, per_token, per_token,
        pl.BlockSpec((N_EXPERTS, 1), const2),
    ]
    out_shape = [
        jax.ShapeDtypeStruct((b, t, d), F32),
        jax.ShapeDtypeStruct((b, t, d // 2), jnp.uint32),
        jax.ShapeDtypeStruct((TOP_K, b * t), jnp.int32),
        jax.ShapeDtypeStruct((TOP_K, b * t), F32),
        jax.ShapeDtypeStruct((TOP_K, b * t), jnp.int32),
        jax.ShapeDtypeStruct((N_EXPERTS, 1), jnp.int32),
    ]
    mod_spec = pl.BlockSpec((None, 1, 6 * d), lambda i, j: (jnp.where(j >= n_lat_tiles, b, i), 0, 0))
    return mod_spec, in_tail, out_specs, out_shape


def _post(y, x, mod, w_out, g2n, router_w, router_b, n_lat_tiles):
    b, t, d = x.shape
    mod_spec, in_tail, out_specs, out_shape = _post_specs(b, t, d, n_lat_tiles)
    row = lambda i, j: (i, j, 0)
    return pl.pallas_call(
        functools.partial(_post_kernel, d=d),
        grid=(b, t // TM),
        in_specs=[pl.BlockSpec((None, TM, d), row), pl.BlockSpec((None, TM, d), row), mod_spec] + in_tail,
        out_specs=out_specs,
        out_shape=out_shape,
        scratch_shapes=[pltpu.VMEM((N_EXPERTS, 1), F32)],
        compiler_params=_cparams(("arbitrary", "arbitrary")),
        name="outproj_router",
    )(y, x, mod, w_out, g2n.reshape(1, d), router_w.T, router_b.reshape(N_EXPERTS, 1))


def _gmlp_kernel(x_ref, mod_ref, g1_ref, win_ref, vg_ref, ws_ref, bs_ref, w_ref, g_ref, rwt_ref, rb_ref,
                 *rest, d):
    outs, cnt_scr, y_scr = rest[:-2], rest[-2], rest[-1]
    mod = mod_ref[...]
    x = x_ref[...]
    h = _norm_mod(x, g1_ref[...], mod[:, d:2 * d], mod[:, 0:d]).astype(BF16)
    z = jnp.dot(h, win_ref[...], preferred_element_type=F32)
    z = 0.5 * z * (1.0 + lax.erf(z * (2.0 ** -0.5)))
    width = z.shape[1] // 2
    u, v = z[:, :width], z[:, width:]
    v = (v * lax.rsqrt(jnp.mean(v * v, axis=-1, keepdims=True) + EPS) * vg_ref[...]).astype(BF16)
    gd = width // GMLP_GROUPS
    for n in range(TM // CHUNK):
        r = slice(n * CHUNK, (n + 1) * CHUNK)
        for g in range(GMLP_GROUPS):
            cs = slice(g * gd, (g + 1) * gd)
            mixed = jnp.dot(ws_ref[g], v[r, cs], preferred_element_type=F32) + bs_ref[:, g:g + 1]
            y_scr[r, cs] = (u[r, cs] * mixed).astype(BF16)
    _residual_router(y_scr[...], x, mod, w_ref[...], g_ref[...], rwt_ref[...], rb_ref[...], outs, cnt_scr, d=d)


def _gmlp_layer(x, mod, g1n, w_in, v_gain, w_s, b_s, w_out, g2n, router_w, router_b, n_lat_tiles):
    b, t, d = x.shape
    width = w_in.shape[1] // 2
    mod_spec, in_tail, out_specs, out_shape = _post_specs(b, t, d, n_lat_tiles)
    row = lambda i, j: (i, j, 0)
    const2 = lambda i, j: (0, 0)
    return pl.pallas_call(
        functools.partial(_gmlp_kernel, d=d),
        grid=(b, t // TM),
        in_specs=[
            pl.BlockSpec((None, TM, d), row),
            mod_spec,
            pl.BlockSpec((1, d), const2),
            pl.BlockSpec((d, 2 * width), const2),
            pl.BlockSpec((1, width), const2),
            pl.BlockSpec((GMLP_GROUPS, CHUNK, CHUNK), lambda i, j: (0, 0, 0)),
            pl.BlockSpec((CHUNK, GMLP_GROUPS), const2),
        ] + in_tail,
        out_specs=out_specs,
        out_shape=out_shape,
        scratch_shapes=[pltpu.VMEM((N_EXPERTS, 1), F32), pltpu.VMEM((TM, width), BF16)],
        compiler_params=_cparams(("arbitrary", "arbitrary")),
        name="gmlp_layer",
    )(x, mod, g1n.reshape(1, d), w_in, v_gain.reshape(1, width), w_s, b_s.T, w_out, g2n.reshape(1, d),
      router_w.T, router_b.reshape(N_EXPERTS, 1))


def _filter_mlp_kernel(z_ref, t_ref, w1, b1, w2, b2, w3, b3, w4, b4, fr, dl, kf_ref, kb_ref, *, width, tl):
    dot = functools.partial(jnp.dot, precision=HIGHEST, preferred_element_type=F32)
    f = fr[...]
    a = jnp.sin(f * (dot(z_ref[...], w1[...]) + b1[...]))
    a = jnp.sin(f * (dot(a, w2[...]) + b2[...]))
    a = jnp.sin(f * (dot(a, w3[...]) + b3[...]))
    k = dot(a, w4[...]) + b4[...]
    window = jnp.exp(-t_ref[...] * dl[...])
    kf_ref[...] = k[:, :width] * window
    pos = pl.program_id(0) * tl + lax.broadcasted_iota(jnp.int32, (tl, 1), 0)
    kb_ref[...] = jnp.where(pos == 0, 0.0, k[:, width:] * window)


def _hyena_filters(length, f_w1, f_b1, f_w2, f_b2, f_w3, f_b3, f_w4, f_b4, freq):
    width = f_w4.shape[1] // 2
    hid = f_w1.shape[1]
    emb = 2 * N_BANDS + 1
    t = jnp.linspace(0.0, 1.0, length, dtype=F32)[:, None]
    w = 2.0 * math.pi * jnp.arange(length, dtype=F32)[:, None] / length
    f = jnp.linspace(1e-4, N_BANDS - 1, N_BANDS, dtype=F32)[None, :]
    z = jnp.concatenate([t, jnp.cos(f * w), -jnp.sin(f * w), jnp.zeros((length, hid - emb), F32)], axis=-1)
    w1p = jnp.concatenate([f_w1, jnp.zeros((hid - emb, hid), F32)], axis=0)
    deltas = jnp.abs(jnp.linspace(MIN_DECAY, MAX_DECAY, width, dtype=F32))[None, :]
    tl = min(length, 512)
    full = lambda shape: pl.BlockSpec(shape, lambda i: (0, 0))
    return pl.pallas_call(
        functools.partial(_filter_mlp_kernel, width=width, tl=tl),
        grid=(length // tl,),
        in_specs=[
            pl.BlockSpec((tl, hid), lambda i: (i, 0)),
            pl.BlockSpec((tl, 1), lambda i: (i, 0)),
            full((hid, hid)), full((1, hid)), full((hid, hid)), full((1, hid)), full((hid, hid)), full((1, hid)),
            full((hid, 2 * width)), full((1, 2 * width)), full((1, hid)), full((1, width)),
        ],
        out_specs=[pl.BlockSpec((tl, width), lambda i: (i, 0)), pl.BlockSpec((tl, width), lambda i: (i, 0))],
        out_shape=[jax.ShapeDtypeStruct((length, width), F32), jax.ShapeDtypeStruct((length, width), F32)],
        compiler_params=_cparams(("parallel",)),
        name="hyena_filter_mlp",
    )(z, t, w1p, f_b1.reshape(1, hid), f_w2, f_b2.reshape(1, hid), f_w3, f_b3.reshape(1, hid),
      f_w4, f_b4.reshape(1, 2 * width), freq.reshape(1, hid), deltas)


def _fft_plan(length):
    n = 2 * length
    n1 = n // FFT_N2
    nz = n1 // 2
    ku = -(-(nz + 1) // 8) * 8
    return n, n1, nz, ku


def _fft_tables(length):
    n, n1, nz, ku = _fft_plan(length)
    i2 = np.arange(FFT_N2)[:, None, None]
    k1 = np.arange(ku)[None, :, None]
    i1 = np.arange(nz)[None, None, :]
    phi = 2.0 * np.pi * (((FFT_N2 * i1 + i2) * k1) % n) / n
    fwd = np.concatenate([np.cos(phi), -np.sin(phi)], axis=1)
    wgt = np.where((k1 == 0) | (k1 == nz), 1.0, np.where(k1 < nz, 2.0, 0.0)) / n
    inv = np.concatenate([np.cos(phi) * wgt, -np.sin(phi) * wgt], axis=1).transpose(0, 2, 1)
    th = 2.0 * np.pi * ((np.arange(FFT_N2)[:, None] * np.arange(FFT_N2)[None, :]) % FFT_N2) / FFT_N2
    c, s = np.cos(th), np.sin(th)
    f2 = np.block([[c, s], [-s, c]])
    f2i = np.block([[c, -s], [s, c]])
    as_bf = lambda a: jnp.asarray(a, F32).astype(BF16)
    return as_bf(fwd), as_bf(inv), as_bf(f2), as_bf(f2i)


def _fft_stage1(src_ref, fwd_ref, a_ref, *, nz, ku):
    slab = 2 * FFT_N2

    def body(i2, carry):
        rows = src_ref[pl.ds(i2, nz, stride=FFT_N2), :].astype(BF16)
        r = jnp.dot(fwd_ref[i2], rows, preferred_element_type=F32)
        a_ref[pl.ds(i2, ku, stride=slab), :] = r[:ku]
        a_ref[pl.ds(FFT_N2 + i2, ku, stride=slab), :] = r[ku:]
        return carry

    lax.fori_loop(0, FFT_N2, body, 0, unroll=FFT_UNROLL)


def _short_conv(p_ref, w_ref, b_ref, pad_ref, emit, *, length):
    step = min(length, 512)
    cb = p_ref.shape[-1]
    pad_ref[pl.ds(0, 8), :] = jnp.zeros((8, cb), F32)
    pad_ref[pl.ds(length + 8, 8), :] = jnp.zeros((8, cb), F32)
    for j in range(length // step):
        pad_ref[pl.ds(8 + j * step, step), :] = p_ref[pl.ds(j * step, step), :].astype(F32)
    w = w_ref[...]
    for j in range(length // step):
        r0 = j * step
        val = (pad_ref[pl.ds(r0 + 7, step), :] * w[0:1] + pad_ref[pl.ds(r0 + 8, step), :] * w[1:2]
               + pad_ref[pl.ds(r0 + 9, step), :] * w[2:3] + b_ref[...])
        emit(r0, step, val)


def _kf_kernel(kf_ref, kb_ref, fwd_ref, f2_ref, o_ref, a_ref, *, nz, ku):
    slab = 2 * FFT_N2
    for src, sign in ((kf_ref, 1.0), (kb_ref, -1.0)):
        _fft_stage1(src, fwd_ref, a_ref, nz=nz, ku=ku)

        def body(k1, carry, sign=sign, first=(src is kf_ref)):
            a = a_ref[pl.ds(pl.multiple_of(k1 * slab, slab), slab), :].astype(BF16)
            xk = jnp.dot(f2_ref[...], a, preferred_element_type=F32)
            if first:
                o_ref[k1] = xk
            else:
                o_ref[k1, :FFT_N2, :] = o_ref[k1, :FFT_N2, :] + xk[:FFT_N2]
                o_ref[k1, FFT_N2:, :] = o_ref[k1, FFT_N2:, :] - xk[FFT_N2:]
            return carry

        lax.fori_loop(0, ku, body, 0, unroll=FFT_UNROLL)


def _filter_spectrum(k_fwd, k_bwd, tables):
    length, width = k_fwd.shape
    _, _, nz, ku = _fft_plan(length)
    fwd, _, f2, _ = tables
    blk = pl.BlockSpec((length, FFT_CB), lambda c: (0, c))
    return pl.pallas_call(
        functools.partial(_kf_kernel, nz=nz, ku=ku),
        grid=(width // FFT_CB,),
        in_specs=[
            blk, blk,
            pl.BlockSpec(fwd.shape, lambda c: (0, 0, 0)),
            pl.BlockSpec(f2.shape, lambda c: (0, 0)),
        ],
        out_specs=pl.BlockSpec((ku, 2 * FFT_N2, FFT_CB), lambda c: (0, 0, c)),
        out_shape=jax.ShapeDtypeStruct((ku, 2 * FFT_N2, width), F32),
        scratch_shapes=[pltpu.VMEM((ku * 2 * FFT_N2, FFT_CB), F32)],
        compiler_params=_cparams(("parallel",)),
        name="hyena_filter_spectrum",
    )(k_fwd, k_bwd, fwd, f2)


def _hyena_conv_kernel(px0_ref, px1_ref, pv_ref, w0_ref, w1_ref, wv_ref, b0_ref, b1_ref, bv_ref, dsk_ref,
                       kf_ref, fwd_ref, inv_ref, f2_ref, f2i_ref, o_ref, a_ref, vv_ref, *, length, nz, ku):
    slab = 2 * FFT_N2

    def set_vv(r0, rows, val):
        vv_ref[pl.ds(r0, rows), :] = val

    def mul_vv(r0, rows, val):
        vv_ref[pl.ds(r0, rows), :] = vv_ref[pl.ds(r0, rows), :] * val

    _short_conv(pv_ref, wv_ref, bv_ref, a_ref, set_vv, length=length)
    _short_conv(px1_ref, w1_ref, b1_ref, a_ref, mul_vv, length=length)
    _fft_stage1(vv_ref, fwd_ref, a_ref, nz=nz, ku=ku)

    def freq_body(k1, carry):
        rows = pl.ds(k1 * slab if isinstance(k1, int) else pl.multiple_of(k1 * slab, slab), slab)
        xk = jnp.dot(f2_ref[...], a_ref[rows, :].astype(BF16), preferred_element_type=F32)
        kf = kf_ref[k1]
        xr, xi = xk[:FFT_N2], xk[FFT_N2:]
        kr, ki = kf[:FFT_N2], kf[FFT_N2:]
        prod = jnp.concatenate([xr * kr - xi * ki, xr * ki + xi * kr], axis=0).astype(BF16)
        a_ref[rows, :] = jnp.dot(f2i_ref[...], prod, preferred_element_type=F32)
        return carry

    lax.fori_loop(0, nz, freq_body, 0, unroll=FFT_UNROLL)
    freq_body(nz, 0)

    def time_body(i2, carry):
        re = a_ref[pl.ds(i2, ku, stride=slab), :]
        im = a_ref[pl.ds(FFT_N2 + i2, ku, stride=slab), :]
        q = jnp.concatenate([re, im], axis=0).astype(BF16)
        y = jnp.dot(inv_ref[i2], q, preferred_element_type=F32)
        rows = pl.ds(i2, nz, stride=FFT_N2)
        vv_ref[rows, :] = y + vv_ref[rows, :] * dsk_ref[...]
        return carry

    lax.fori_loop(0, FFT_N2, time_body, 0, unroll=FFT_UNROLL)

    def emit_out(r0, rows, val):
        o_ref[pl.ds(r0, rows), :] = (vv_ref[pl.ds(r0, rows), :] * val).astype(o_ref.dtype)

    _short_conv(px0_ref, w0_ref, b0_ref, a_ref, emit_out, length=length)


def _hyena_conv(p, conv_w, conv_b, d_skip, kf, tables, length):
    b, _, w3 = p.shape
    width = w3 // 3
    ncb = width // FFT_CB
    _, _, nz, ku = _fft_plan(length)
    fwd, inv, f2, f2i = tables
    pblk = lambda part: _resident((None, length, FFT_CB), lambda c, i: (i, 0, part * ncb + c))
    wblk = lambda part: pl.BlockSpec((3, FFT_CB), lambda c, i: (0, part * ncb + c))
    bblk = lambda part: pl.BlockSpec((1, FFT_CB), lambda c, i: (0, part * ncb + c))
    return pl.pallas_call(
        functools.partial(_hyena_conv_kernel, length=length, nz=nz, ku=ku),
        grid=(ncb, b),
        in_specs=[
            pblk(0), pblk(1), pblk(2), wblk(0), wblk(1), wblk(2), bblk(0), bblk(1), bblk(2),
            pl.BlockSpec((1, FFT_CB), lambda c, i: (0, c)),
            _resident((ku, 2 * FFT_N2, FFT_CB), lambda c, i: (0, 0, c)),
            _resident(fwd.shape, lambda c, i: (0, 0, 0)),
            _resident(inv.shape, lambda c, i: (0, 0, 0)),
            _resident(f2.shape, lambda c, i: (0, 0)),
            _resident(f2i.shape, lambda c, i: (0, 0)),
        ],
        out_specs=pl.BlockSpec((None, length, FFT_CB), lambda c, i: (i, 0, c)),
        out_shape=jax.ShapeDtypeStruct((b, length, width), BF16),
        scratch_shapes=[
            pltpu.VMEM((max(ku * 2 * FFT_N2, length + 16), FFT_CB), F32),
            pltpu.VMEM((length, FFT_CB), F32),
        ],
        compiler_params=_cparams(("arbitrary", "arbitrary")),
        name="hyena_long_conv",
    )(p, p, p, conv_w, conv_w, conv_w, conv_b.reshape(1, w3), conv_b.reshape(1, w3), conv_b.reshape(1, w3),
      d_skip.reshape(1, width), kf, fwd, inv, f2, f2i)


def _dense_dft_tables(length):
    n = 2 * length
    th = 2.0 * np.pi * ((np.arange(n)[:, None] * np.arange(length)[None, :]) % n) / n
    fwd = np.concatenate([np.cos(th), -np.sin(th)], axis=0)
    inv = np.concatenate([np.cos(th), -np.sin(th)], axis=0).T / n
    return jnp.asarray(fwd, F32).astype(BF16), jnp.asarray(inv, F32).astype(BF16)


def _kf_dense_kernel(kf_ref, kb_ref, fwd_ref, o_ref, *, n):
    xf = jnp.dot(fwd_ref[...], kf_ref[...].astype(BF16), preferred_element_type=F32)
    xb = jnp.dot(fwd_ref[...], kb_ref[...].astype(BF16), preferred_element_type=F32)
    o_ref[:n, :] = xf[:n] + xb[:n]
    o_ref[n:, :] = xf[n:] - xb[n:]


def _filter_spectrum_dense(k_fwd, k_bwd, fwd):
    length, width = k_fwd.shape
    n = 2 * length
    blk = pl.BlockSpec((length, FFT_CB), lambda c: (0, c))
    return pl.pallas_call(
        functools.partial(_kf_dense_kernel, n=n),
        grid=(width // FFT_CB,),
        in_specs=[blk, blk, pl.BlockSpec(fwd.shape, lambda c: (0, 0))],
        out_specs=pl.BlockSpec((2 * n, FFT_CB), lambda c: (0, c)),
        out_shape=jax.ShapeDtypeStruct((2 * n, width), F32),
        compiler_params=_cparams(("parallel",)),
        name="hyena_ctx_filter_spectrum",
    )(k_fwd, k_bwd, fwd)


def _hyena_ctx_kernel(px0_ref, px1_ref, pv_ref, w0_ref, w1_ref, wv_ref, b0_ref, b1_ref, bv_ref, dsk_ref,
                      kf_ref, fwd_ref, inv_ref, o_ref, pad_ref, vv_ref, *, length):
    n = 2 * length

    def set_vv(r0, rows, val):
        vv_ref[pl.ds(r0, rows), :] = val

    def mul_vv(r0, rows, val):
        vv_ref[pl.ds(r0, rows), :] = vv_ref[pl.ds(r0, rows), :] * val

    _short_conv(pv_ref, wv_ref, bv_ref, pad_ref, set_vv, length=length)
    _short_conv(px1_ref, w1_ref, b1_ref, pad_ref, mul_vv, length=length)
    vv = vv_ref[...]
    xk = jnp.dot(fwd_ref[...], vv.astype(BF16), preferred_element_type=F32)
    kf = kf_ref[...]
    xr, xi, kr, ki = xk[:n], xk[n:], kf[:n], kf[n:]
    prod = jnp.concatenate([xr * kr - xi * ki, xr * ki + xi * kr], axis=0).astype(BF16)
    vv_ref[...] = jnp.dot(inv_ref[...], prod, preferred_element_type=F32) + vv * dsk_ref[...]

    def emit_out(r0, rows, val):
        o_ref[pl.ds(r0, rows), :] = (vv_ref[pl.ds(r0, rows), :] * val).astype(o_ref.dtype)

    _short_conv(px0_ref, w0_ref, b0_ref, pad_ref, emit_out, length=length)


def _hyena_ctx_conv(p, conv_w, conv_b, d_skip, kf, fwd, inv, n_lat, length):
    b, _, w3 = p.shape
    width = w3 // 3
    ncb = width // FFT_CB
    rb = n_lat // length
    pblk = lambda part: pl.BlockSpec((None, length, FFT_CB), lambda c, i: (i, rb, part * ncb + c))
    wblk = lambda part: pl.BlockSpec((3, FFT_CB), lambda c, i: (0, part * ncb + c))
    bblk = lambda part: pl.BlockSpec((1, FFT_CB), lambda c, i: (0, part * ncb + c))
    return pl.pallas_call(
        functools.partial(_hyena_ctx_kernel, length=length),
        grid=(ncb, b),
        in_specs=[
            pblk(0), pblk(1), pblk(2), wblk(0), wblk(1), wblk(2), bblk(0), bblk(1), bblk(2),
            pl.BlockSpec((1, FFT_CB), lambda c, i: (0, c)),
            pl.BlockSpec((4 * length, FFT_CB), lambda c, i: (0, c)),
            pl.BlockSpec(fwd.shape, lambda c, i: (0, 0)),
            pl.BlockSpec(inv.shape, lambda c, i: (0, 0)),
        ],
        out_specs=pl.BlockSpec((None, length, FFT_CB), lambda c, i: (i, 0, c)),
        out_shape=jax.ShapeDtypeStruct((b, length, width), BF16),
        scratch_shapes=[pltpu.VMEM((length + 16, FFT_CB), F32), pltpu.VMEM((length, FFT_CB), F32)],
        compiler_params=_cparams(("arbitrary", "arbitrary")),
        name="hyena_ctx_conv",
    )(p, p, p, conv_w, conv_w, conv_w, conv_b.reshape(1, w3), conv_b.reshape(1, w3), conv_b.reshape(1, w3),
      d_skip.reshape(1, width), kf, fwd, inv)


def _row_copy(src, dst, sem):
    return pltpu.make_async_copy(src, dst, sem)


def _scatter_kernel(cnt_ref, start_ref, dest_ref, h_ref, x_hbm, src0, src1, src2, src3, zero_scr, sem, zsem, *, n_tok):
    rows, half = h_ref.shape
    src_scr = (src0, src1, src2, src3)
    n_real = TOP_K * n_tok
    row_iota = lax.broadcasted_iota(jnp.int32, (rows, LANES), 0)

    def ids_from(base):
        return pltpu.bitcast(base + row_iota, jnp.uint32)

    @pl.when(pl.program_id(0) == 0)
    def _():
        zero_scr[:, :half] = jnp.zeros((rows, half), jnp.uint32)

        def per_expert(e, total):
            cnt = cnt_ref[e]
            n_pad = (EXPERT_BLOCK - cnt % EXPERT_BLOCK) % EXPERT_BLOCK
            base = start_ref[e] + cnt
            zero_scr[:, half:] = ids_from(n_real + total)

            def one(r, carry):
                _row_copy(zero_scr.at[pl.ds(r, 1)], x_hbm.at[pl.ds(base + r, 1)], zsem).start()
                return carry

            def wait_one(r, carry):
                _row_copy(zero_scr.at[pl.ds(0, 1)], x_hbm.at[pl.ds(0, 1)], zsem).wait()
                return carry

            lax.fori_loop(0, n_pad, one, 0)
            lax.fori_loop(0, n_pad, wait_one, 0)
            return total + n_pad

        total = lax.fori_loop(0, N_EXPERTS, per_expert, 0)
        first_free = start_ref[N_EXPERTS] // EXPERT_BLOCK
        n_blocks = x_hbm.shape[0] // EXPERT_BLOCK

        def tail(i, carry):
            zero_scr[:, half:] = ids_from(n_real + total + (i - first_free) * EXPERT_BLOCK)
            cp = _row_copy(zero_scr, x_hbm.at[pl.ds(pl.multiple_of(i * EXPERT_BLOCK, EXPERT_BLOCK), EXPERT_BLOCK)],
                           zsem)
            cp.start()
            cp.wait()
            return carry

        lax.fori_loop(first_free, n_blocks, tail, 0)

    h = h_ref[...]
    for k in range(TOP_K):
        src_scr[k][:, :half] = h
        src_scr[k][:, half:] = ids_from(k * n_tok + pl.program_id(0) * rows)

    def issue(r, carry):
        for k in range(TOP_K):
            _row_copy(src_scr[k].at[pl.ds(r, 1)], x_hbm.at[pl.ds(dest_ref[0, r * TOP_K + k], 1)], sem).start()
        return carry

    lax.fori_loop(0, rows, issue, 0, unroll=4)
    for k in range(TOP_K):
        _row_copy(src_scr[k], x_hbm.at[pl.ds(0, rows)], sem).wait()


def _dispatch(h2p, dest_tiles, counts, pad_start, n_blocks):
    n_tok, half = h2p.shape
    assert TM == EXPERT_BLOCK
    width = half + LANES
    return pl.pallas_call(
        functools.partial(_scatter_kernel, n_tok=n_tok),
        grid_spec=pltpu.PrefetchScalarGridSpec(
            num_scalar_prefetch=2,
            grid=(n_tok // TM,),
            in_specs=[
                pl.BlockSpec((None, 1, TOP_K * TM), lambda i, c, s: (i, 0, 0), memory_space=pltpu.SMEM),
                pl.BlockSpec((TM, half), lambda i, c, s: (i, 0)),
            ],
            out_specs=pl.BlockSpec(memory_space=pl.ANY),
            scratch_shapes=[pltpu.VMEM((TM, width), jnp.uint32)] * TOP_K + [
                pltpu.VMEM((EXPERT_BLOCK, width), jnp.uint32), pltpu.SemaphoreType.DMA(()), pltpu.SemaphoreType.DMA(())],
        ),
        out_shape=jax.ShapeDtypeStruct((n_blocks * EXPERT_BLOCK, width), jnp.uint32),
        compiler_params=_cparams(("arbitrary",)),
        name="moe_dispatch",
    )(counts, pad_start, dest_tiles, h2p)


def _ffn_kernel(be_ref, ids_prev_ref, ids_last_ref, x_ref, w1_ref, b1_ref, w2_ref, b2_ref, y_hbm,
                buf0, buf1, w1_bf, w2_bf, sem0, sem1, *, d_ff, half):
    i = pl.program_id(0)
    n = pl.num_programs(0)
    rows = x_ref.shape[0]
    spare = y_hbm.shape[0] - 2 * rows

    def drain(src, sem, ids_ref):
        for r in range(rows):
            _row_copy(src.at[pl.ds(r, 1)], y_hbm.at[pl.ds(ids_ref[0, r], 1)], sem).start()

    def wait(src, sem):
        _row_copy(src, y_hbm.at[pl.ds(0, rows)], sem).wait()

    @pl.when(i == 0)
    def _():
        buf0[...] = jnp.zeros(buf0.shape, buf0.dtype)
        buf1[...] = jnp.zeros(buf1.shape, buf1.dtype)
        _row_copy(buf0, y_hbm.at[pl.ds(spare + rows, rows)], sem0).start()

    @pl.when(jnp.logical_or(i == 0, be_ref[i] != be_ref[jnp.maximum(i - 1, 0)]))
    def _():
        w1_bf[...] = w1_ref[...].astype(BF16)
        w2_bf[...] = w2_ref[...].astype(BF16)

    def step(src, ssem, dst, dsem):
        drain(src, ssem, ids_prev_ref)
        x = _unpack_bf16_pairs(x_ref[:, :half]).astype(BF16)
        hh = jnp.dot(x, w1_bf[...], preferred_element_type=F32) + b1_ref[...]
        glu = jnp.minimum(hh[:, :d_ff], SWIGLU_LIMIT)
        lin = jnp.clip(hh[:, d_ff:], -SWIGLU_LIMIT, SWIGLU_LIMIT)
        act = (glu * jax.nn.sigmoid(SWIGLU_ALPHA * glu) * (lin + 1.0)).astype(BF16)
        y = _pack_bf16_pairs(jnp.dot(act, w2_bf[...], preferred_element_type=F32) + b2_ref[...])
        wait(dst, dsem)
        dst[...] = y

    @pl.when(i % 2 == 0)
    def _():
        step(buf1, sem1, buf0, sem0)

    @pl.when(i % 2 == 1)
    def _():
        step(buf0, sem0, buf1, sem1)

    @pl.when(i == n - 1)
    def _():
        last_even = (y_hbm.shape[0] // rows - 3) % 2 == 0
        last, lsem, prev, psem = (buf0, sem0, buf1, sem1) if last_even else (buf1, sem1, buf0, sem0)
        drain(last, lsem, ids_last_ref)
        wait(prev, psem)
        wait(last, lsem)


def _expert_ffn(x_disp, ids, block_e, layer, w1, b1, w2, b2):
    rows, width = x_disp.shape
    half = width - LANES
    depth, n_e, d, ff2 = w1.shape
    d_ff = ff2 // 2
    n_blocks = rows // EXPERT_BLOCK
    ids_prev = jnp.concatenate([rows + jnp.arange(EXPERT_BLOCK, dtype=jnp.int32), ids])
    ids_prev = ids_prev.reshape(n_blocks + 1, 1, EXPERT_BLOCK)
    return pl.pallas_call(
        functools.partial(_ffn_kernel, d_ff=d_ff, half=half),
        grid_spec=pltpu.PrefetchScalarGridSpec(
            num_scalar_prefetch=1,
            grid=(n_blocks,),
            in_specs=[
                pl.BlockSpec((None, 1, EXPERT_BLOCK), lambda i, be: (i, 0, 0), memory_space=pltpu.SMEM),
                pl.BlockSpec((None, 1, EXPERT_BLOCK), lambda i, be: (n_blocks, 0, 0), memory_space=pltpu.SMEM),
                pl.BlockSpec((EXPERT_BLOCK, width), lambda i, be: (i, 0)),
                pl.BlockSpec((None, None, d, ff2), lambda i, be: (layer, be[i], 0, 0)),
                pl.BlockSpec((None, None, 1, ff2), lambda i, be: (layer, be[i], 0, 0)),
                pl.BlockSpec((None, None, d_ff, d), lambda i, be: (layer, be[i], 0, 0)),
                pl.BlockSpec((None, None, 1, d), lambda i, be: (layer, be[i], 0, 0)),
            ],
            out_specs=pl.BlockSpec(memory_space=pl.ANY),
            scratch_shapes=[pltpu.VMEM((EXPERT_BLOCK, half), jnp.uint32), pltpu.VMEM((EXPERT_BLOCK, half), jnp.uint32),
                            pltpu.VMEM((d, ff2), BF16), pltpu.VMEM((d_ff, d), BF16),
                            pltpu.SemaphoreType.DMA(()), pltpu.SemaphoreType.DMA(())],
        ),
        out_shape=jax.ShapeDtypeStruct((rows + 2 * EXPERT_BLOCK, half), jnp.uint32),
        compiler_params=_cparams(("arbitrary",)),
        name="moe_expert_ffn",
    )(block_e, ids_prev, ids_prev, x_disp, w1, b1.reshape(depth, n_e, 1, ff2), w2, b2.reshape(depth, n_e, 1, d))


def _combine_kernel(y0_ref, y1_ref, y2_ref, y3_ref, gate_ref, x_ref, mod_ref, fg_ref, o_ref, *, d, final):
    gate = gate_ref[...]
    out = _unpack_bf16_pairs(y0_ref[...]) * gate[:, 0:1]
    for k, y_ref in enumerate((y1_ref, y2_ref, y3_ref), start=1):
        out = out + _unpack_bf16_pairs(y_ref[...]) * gate[:, k:k + 1]
    xn = x_ref[...] + mod_ref[:, 5 * d:6 * d] * out
    if final:
        xn = xn * lax.rsqrt(jnp.mean(xn * xn, axis=-1, keepdims=True) + EPS) * fg_ref[...]
    o_ref[...] = xn


def _combine(y_rows, gates, x, mod, final_g, n_lat_tiles, final):
    b, t, d = x.shape
    nt = t // TM
    n_tiles = b * nt
    nt_out = n_lat_tiles if final else nt
    y_spec = lambda k: pl.BlockSpec((TM, d // 2), lambda i, j: (k * n_tiles + i * nt + j, 0))
    return pl.pallas_call(
        functools.partial(_combine_kernel, d=d, final=final),
        grid=(b, nt_out),
        in_specs=[
            y_spec(0), y_spec(1), y_spec(2), y_spec(3),
            pl.BlockSpec((TM, TOP_K), lambda i, j: (i * nt + j, 0)),
            pl.BlockSpec((None, TM, d), lambda i, j: (i, j, 0)),
            pl.BlockSpec((None, 1, 6 * d), lambda i, j: (jnp.where(j >= n_lat_tiles, b, i), 0, 0)),
            pl.BlockSpec((1, d), lambda i, j: (0, 0)),
        ],
        out_specs=pl.BlockSpec((None, TM, d), lambda i, j: (i, j, 0)),
        out_shape=jax.ShapeDtypeStruct((b, nt_out * TM, d), F32),
        compiler_params=_cparams(("parallel", "parallel")),
        name="moe_combine",
    )(y_rows, y_rows, y_rows, y_rows, gates, x, mod, final_g.reshape(1, d))


def _moe(h2p, idx_t, gates_t, rank_t, cnt, x, mod, layer, w1, b1, w2, b2, final_g, n_lat_tiles, final):
    b, t, d = x.shape
    n_tok = b * t
    n_assign = n_tok * TOP_K
    counts = cnt[:, 0]
    padded = (counts + EXPERT_BLOCK - 1) // EXPERT_BLOCK * EXPERT_BLOCK
    pad_end = jnp.cumsum(padded).astype(jnp.int32)
    pad_start = pad_end - padded
    experts = jnp.arange(N_EXPERTS, dtype=jnp.int32)
    dest_t = rank_t + jnp.sum(jnp.where(idx_t[..., None] == experts, pad_start, 0), axis=-1)
    n_blocks = -(-n_assign // EXPERT_BLOCK) + N_EXPERTS
    block_start = jnp.arange(n_blocks, dtype=jnp.int32) * EXPERT_BLOCK
    block_e = jnp.minimum(jnp.sum(pad_end[None, :] <= block_start[:, None], axis=1), N_EXPERTS - 1).astype(jnp.int32)
    dest_tiles = dest_t.reshape(TOP_K, n_tok // TM, TM).transpose(1, 2, 0).reshape(n_tok // TM, 1, TOP_K * TM)

    x_disp = _dispatch(h2p.reshape(n_tok, d // 2), dest_tiles, counts,
                       jnp.concatenate([pad_start, pad_end[-1:]]), n_blocks)
    ids = lax.bitcast_convert_type(x_disp[:, d // 2], jnp.int32)
    y_rows = _expert_ffn(x_disp, ids, block_e, layer, w1, b1, w2, b2)
    return _combine(y_rows, gates_t.T, x, mod, final_g, n_lat_tiles, final)


def _rope_tables(n_lat, n_ctx):
    rows = n_lat // GRID_W
    row = jnp.repeat(jnp.arange(rows, dtype=F32), GRID_W)
    col = jnp.tile(jnp.arange(GRID_W, dtype=F32), rows)
    axis_dim = HEAD_DIM // 2
    inv = ROPE_THETA ** (-jnp.arange(0, axis_dim, 2, dtype=F32) / axis_dim)
    ang_r = row[:, None] * inv
    ang_c = col[:, None] * inv
    ang = jnp.concatenate([ang_r, ang_r, ang_c, ang_c], axis=-1)
    cos = jnp.concatenate([jnp.cos(ang), jnp.ones((n_ctx, HEAD_DIM), F32)], axis=0)
    sin = jnp.concatenate([jnp.sin(ang), jnp.zeros((n_ctx, HEAD_DIM), F32)], axis=0)
    even = (jnp.arange(HEAD_DIM) // 32) % 2 == 0
    return cos, jnp.where(even, -sin, 0.0), jnp.where(even, 0.0, sin)


def kernel(x, c, ctx, c_ctx, ada_w, ada_b, norm1_g, norm2_g, mix_w_out, router_w, router_b, exp_w1, exp_b1, exp_w2, exp_b2, attn_w_in, attn_q_gain, attn_k_gain, gmlp_w_in, gmlp_v_gain, gmlp_w_s, gmlp_b_s, hyena_w_in, hyena_conv_w, hyena_conv_b, hyena_f_w1, hyena_f_b1, hyena_f_w2, hyena_f_b2, hyena_f_w3, hyena_f_b3, hyena_f_w4, hyena_f_b4, hyena_freq, hyena_d, final_g):
    b, s, d = x.shape
    n_ctx = ctx.shape[1]
    depth = ada_w.shape[0]
    assert s % TM == 0 and n_ctx == TM and b < MOD_ROWS and s % (FFT_N2 * 8) == 0
    n_lat_tiles = s // TM

    xs = jnp.concatenate([x, ctx], axis=1)
    cc = jnp.concatenate([c, c_ctx[None, :], jnp.zeros((MOD_ROWS - b - 1, d), F32)], axis=0)
    mods = _modulation(cc, ada_w, ada_b).reshape(depth, MOD_ROWS, 1, 6 * d)
    rope = _rope_tables(s, n_ctx)

    for i in range(depth):
        kind, j = i % 3, i // 3
        last = i == depth - 1
        mod = mods[i]
        w_out = mix_w_out[i].astype(BF16)
        if kind == 0:
            q, k, v = _attn_proj(xs, mod, norm1_g[i], attn_w_in[j].astype(BF16), attn_q_gain[j], attn_k_gain[j],
                                 rope, n_lat_tiles)
            y = _attention(q, k, v, s)
        elif kind == 2:
            p = _inproj(xs, mod, norm1_g[i], hyena_w_in[j].astype(BF16), n_lat_tiles)
            fargs = (hyena_f_w1[j], hyena_f_b1[j], hyena_f_w2[j], hyena_f_b2[j], hyena_f_w3[j], hyena_f_b3[j],
                     hyena_f_w4[j], hyena_f_b4[j], hyena_freq[j])
            tables = _fft_tables(s)
            kf = _filter_spectrum(*_hyena_filters(s, *fargs), tables)
            y_lat = _hyena_conv(p, hyena_conv_w[j], hyena_conv_b[j], hyena_d[j], kf, tables, s)
            dfwd, dinv = _dense_dft_tables(n_ctx)
            kf_c = _filter_spectrum_dense(*_hyena_filters(n_ctx, *fargs), dfwd)
            y_ctx = _hyena_ctx_conv(p, hyena_conv_w[j], hyena_conv_b[j], hyena_d[j], kf_c, dfwd, dinv, s, n_ctx)
            y = jnp.concatenate([y_lat, y_ctx], axis=1)
        if kind == 1:
            xs, *routed = _gmlp_layer(
                xs, mod, norm1_g[i], gmlp_w_in[j].astype(BF16), gmlp_v_gain[j], gmlp_w_s[j].astype(BF16),
                gmlp_b_s[j], w_out, norm2_g[i], router_w[i], router_b[i], n_lat_tiles)
        else:
            xs, *routed = _post(y, xs, mod, w_out, norm2_g[i], router_w[i], router_b[i], n_lat_tiles)
        xs = _moe(*routed, xs, mod, i, exp_w1, exp_b1, exp_w2, exp_b2, final_g, n_lat_tiles, last)
    return xs
```

```python
import functools
import math

import jax
import jax.numpy as jnp
import numpy as np
from jax import lax
from jax.experimental import pallas as pl
from jax.experimental.pallas import tpu as pltpu

F32 = jnp.float32
BF16 = jnp.bfloat16
HIGHEST = lax.Precision.HIGHEST

EPS = 1e-6
GRID_W = 64
ROPE_THETA = 10000.0
HEAD_DIM = 128
N_KV_HEADS = 2
Q_GROUP = 4
CHUNK = 128
GMLP_GROUPS = 8
N_EXPERTS = 32
TOP_K = 4
SWIGLU_ALPHA = 1.702
SWIGLU_LIMIT = 7.0
N_BANDS = 16
DECAY_TARGET = 1e-2
MAX_DECAY = math.log(DECAY_TARGET) / 0.3
MIN_DECAY = math.log(DECAY_TARGET) / 1.5

LANES = 128
TM = 256
ATTN_TK = 512
MOD_ROWS = 16
EXPERT_BLOCK = 256
FFT_N2 = 128
FFT_CB = 128
FFT_UNROLL = 16
VMEM_LIMIT = 56 * 1024 * 1024


def _cparams(sem):
    return pltpu.CompilerParams(dimension_semantics=sem, vmem_limit_bytes=VMEM_LIMIT)


def _norm_mod(x, g, scale, shift):
    y = x * lax.rsqrt(jnp.mean(x * x, axis=-1, keepdims=True) + EPS)
    return y * g * (1.0 + scale) + shift


def _resident(shape, index_map):
    return pl.BlockSpec(shape, index_map, pipeline_mode=pl.Buffered(1))


def _mod_kernel(c_ref, w_ref, b_ref, o_ref):
    c = c_ref[...]
    s = c * jax.nn.sigmoid(c)
    o_ref[...] = jnp.dot(s, w_ref[...], precision=HIGHEST, preferred_element_type=F32) + b_ref[...]


def _modulation(cc, ada_w, ada_b):
    depth, d, d6 = ada_w.shape
    tn = 1536
    return pl.pallas_call(
        _mod_kernel,
        grid=(depth, d6 // tn),
        in_specs=[
            pl.BlockSpec((MOD_ROWS, d), lambda l, j: (0, 0)),
            pl.BlockSpec((None, d, tn), lambda l, j: (l, 0, j)),
            pl.BlockSpec((None, 1, tn), lambda l, j: (l, 0, j)),
        ],
        out_specs=pl.BlockSpec((None, MOD_ROWS, tn), lambda l, j: (l, 0, j)),
        out_shape=jax.ShapeDtypeStruct((depth, MOD_ROWS, d6), F32),
        compiler_params=_cparams(("arbitrary", "arbitrary")),
        name="adaln_mod",
    )(cc, ada_w, ada_b.reshape(depth, 1, d6))


def _inproj_kernel(x_ref, mod_ref, g_ref, w_ref, o_ref, *, d):
    mod = mod_ref[...]
    h = _norm_mod(x_ref[...], g_ref[...], mod[:, d:2 * d], mod[:, 0:d]).astype(BF16)
    o_ref[...] = jnp.dot(h, w_ref[...], preferred_element_type=F32).astype(o_ref.dtype)


def _inproj(x, mod, g, w, n_lat_tiles):
    b, t, d = x.shape
    n_out = w.shape[1]
    return pl.pallas_call(
        functools.partial(_inproj_kernel, d=d),
        grid=(b, t // TM),
        in_specs=[
            pl.BlockSpec((None, TM, d), lambda i, j: (i, j, 0)),
            pl.BlockSpec((None, 1, 6 * d), lambda i, j: (jnp.where(j >= n_lat_tiles, b, i), 0, 0)),
            pl.BlockSpec((1, d), lambda i, j: (0, 0)),
            pl.BlockSpec((d, n_out), lambda i, j: (0, 0)),
        ],
        out_specs=pl.BlockSpec((None, TM, n_out), lambda i, j: (i, j, 0)),
        out_shape=jax.ShapeDtypeStruct((b, t, n_out), BF16),
        compiler_params=_cparams(("parallel", "parallel")),
        name="inproj",
    )(x, mod, g.reshape(1, d), w)


def _attn_proj_kernel(x_ref, mod_ref, g_ref, w_ref, qg_ref, kg_ref, cos_ref, sa_ref, sb_ref,
                      q_ref, k_ref, v_ref, *, d):
    mod = mod_ref[...]
    h = _norm_mod(x_ref[...], g_ref[...], mod[:, d:2 * d], mod[:, 0:d]).astype(BF16)
    p = jnp.dot(h, w_ref[...], preferred_element_type=F32)
    cos, sin_a, sin_b = cos_ref[...], sa_ref[...], sb_ref[...]
    n_q = Q_GROUP * N_KV_HEADS

    def head(ph, gain):
        y = ph * lax.rsqrt(jnp.mean(ph * ph, axis=-1, keepdims=True) + EPS) * gain
        return y * cos + pltpu.roll(y, LANES - 32, 1) * sin_a + pltpu.roll(y, 32, 1) * sin_b

    qg = qg_ref[...] * (HEAD_DIM ** -0.5 * math.log2(math.e))
    kg = kg_ref[...]
    for i in range(n_q):
        q_ref[:, i * HEAD_DIM:(i + 1) * HEAD_DIM] = head(p[:, i * HEAD_DIM:(i + 1) * HEAD_DIM], qg).astype(BF16)
    k0 = n_q * HEAD_DIM
    for i in range(N_KV_HEADS):
        k_ref[:, i * HEAD_DIM:(i + 1) * HEAD_DIM] = head(
            p[:, k0 + i * HEAD_DIM:k0 + (i + 1) * HEAD_DIM], kg).astype(BF16)
    v0 = k0 + N_KV_HEADS * HEAD_DIM
    for i in range(N_KV_HEADS):
        v_ref[:, 2 * i * HEAD_DIM:(2 * i + 1) * HEAD_DIM] = p[:, v0 + i * HEAD_DIM:v0 + (i + 1) * HEAD_DIM].astype(BF16)
        v_ref[:, (2 * i + 1) * HEAD_DIM:(2 * i + 2) * HEAD_DIM] = jnp.ones((p.shape[0], HEAD_DIM), BF16)


def _attn_proj(x, mod, g, w, q_gain, k_gain, rope, n_lat_tiles):
    b, t, d = x.shape
    n_out = w.shape[1]
    qw = Q_GROUP * N_KV_HEADS * HEAD_DIM
    kw = N_KV_HEADS * HEAD_DIM
    row = lambda i, j: (i, j, 0)
    tab = pl.BlockSpec((TM, HEAD_DIM), lambda i, j: (j, 0))
    return pl.pallas_call(
        functools.partial(_attn_proj_kernel, d=d),
        grid=(b, t // TM),
        in_specs=[
            pl.BlockSpec((None, TM, d), row),
            pl.BlockSpec((None, 1, 6 * d), lambda i, j: (jnp.where(j >= n_lat_tiles, b, i), 0, 0)),
            pl.BlockSpec((1, d), lambda i, j: (0, 0)),
            pl.BlockSpec((d, n_out), lambda i, j: (0, 0)),
            pl.BlockSpec((1, HEAD_DIM), lambda i, j: (0, 0)),
            pl.BlockSpec((1, HEAD_DIM), lambda i, j: (0, 0)),
            tab, tab, tab,
        ],
        out_specs=[
            pl.BlockSpec((None, TM, qw), row),
            pl.BlockSpec((None, TM, kw), row),
            pl.BlockSpec((None, TM, 2 * kw), row),
        ],
        out_shape=[
            jax.ShapeDtypeStruct((b, t, qw), BF16),
            jax.ShapeDtypeStruct((b, t, kw), BF16),
            jax.ShapeDtypeStruct((b, t, 2 * kw), BF16),
        ],
        compiler_params=_cparams(("parallel", "parallel")),
        name="attn_proj",
    )(x, mod, g.reshape(1, d), w, q_gain.reshape(1, HEAD_DIM), k_gain.reshape(1, HEAD_DIM), *rope)


def _attn_kernel(q_ref, k_ref, v_ref, o_ref, q_scr, s0, s1, p0, p1, a0, a1, m_scr, acc_scr, *, n_lat, n_ctx, tk):
    is_ctx = pl.program_id(2) * TM >= n_lat
    s_b, p_b, a_b = (s0, s1), (p0, p1), (a0, a1)
    for g in range(Q_GROUP):
        q_scr[g * TM:(g + 1) * TM, :] = q_ref[:, g * HEAD_DIM:(g + 1) * HEAD_DIM]
    m_scr[...] = jnp.full(m_scr.shape, -jnp.inf, F32)
    acc_scr[...] = jnp.zeros(acc_scr.shape, F32)

    def scores(start, size, slot):
        k = k_ref[pl.ds(start, size), :]
        s_b[slot][:, :size] = lax.dot_general(q_scr[...], k, (((1,), (1,)), ((), ())), preferred_element_type=F32)

    def softmax(size, slot):
        nb = size // LANES
        s = s_b[slot][:, :size]
        mx = s[:, 0:LANES]
        for j in range(1, nb):
            mx = jnp.maximum(mx, s[:, j * LANES:(j + 1) * LANES])
        m_prev = m_scr[...]
        m_new = jnp.maximum(m_prev, jnp.max(mx, axis=-1, keepdims=True))
        a_b[slot][...] = jnp.exp2(m_prev - m_new)
        m_scr[...] = m_new
        m_rep = jnp.concatenate([m_new] * nb, axis=1) if nb > 1 else m_new
        p_b[slot][:, :size] = jnp.exp2(s - m_rep).astype(BF16)

    def values(start, size, slot):
        a = a_b[slot][...]
        acc_scr[...] = jnp.concatenate([a, a], axis=1) * acc_scr[...] + jnp.dot(
            p_b[slot][:, :size], v_ref[pl.ds(start, size), :], preferred_element_type=F32)

    n = n_lat // tk

    @pl.when(is_ctx)
    def _():
        scores(n_lat, n_ctx, 0)
        softmax(n_ctx, 0)
        values(n_lat, n_ctx, 0)

    @pl.when(jnp.logical_not(is_ctx))
    def _():
        scores(0, tk, 0)
        softmax(tk, 0)
        scores(tk, tk, 1)

        def pair(i, carry):
            j = 2 + 2 * i
            values(pl.multiple_of((j - 2) * tk, tk), tk, 0)
            softmax(tk, 1)
            scores(pl.multiple_of(j * tk, tk), tk, 0)
            values(pl.multiple_of((j - 1) * tk, tk), tk, 1)
            softmax(tk, 0)
            scores(pl.multiple_of((j + 1) * tk, tk), tk, 1)
            return carry

        lax.fori_loop(0, (n - 2) // 2, pair, 0)
        values((n - 2) * tk, tk, 0)
        softmax(tk, 1)
        scores(n_lat, n_ctx, 0)
        values((n - 1) * tk, tk, 1)
        softmax(n_ctx, 0)
        values(n_lat, n_ctx, 0)

    out = acc_scr[:, :HEAD_DIM] / acc_scr[:, HEAD_DIM:]
    for g in range(Q_GROUP):
        o_ref[:, g * HEAD_DIM:(g + 1) * HEAD_DIM] = out[g * TM:(g + 1) * TM].astype(o_ref.dtype)


def _attention(q, k, v, n_lat):
    b, t, _ = q.shape
    n_ctx = t - n_lat
    gw = Q_GROUP * HEAD_DIM
    tk = ATTN_TK
    assert n_lat % (2 * tk) == 0 and n_ctx <= tk and n_ctx % LANES == 0
    m = Q_GROUP * TM
    return pl.pallas_call(
        functools.partial(_attn_kernel, n_lat=n_lat, n_ctx=n_ctx, tk=tk),
        grid=(b, N_KV_HEADS, t // TM),
        in_specs=[
            pl.BlockSpec((None, TM, gw), lambda i, h, j: (i, j, h)),
            pl.BlockSpec((None, t, HEAD_DIM), lambda i, h, j: (i, 0, h)),
            pl.BlockSpec((None, t, 2 * HEAD_DIM), lambda i, h, j: (i, 0, h)),
        ],
        out_specs=pl.BlockSpec((None, TM, gw), lambda i, h, j: (i, j, h)),
        out_shape=jax.ShapeDtypeStruct(q.shape, BF16),
        scratch_shapes=[
            pltpu.VMEM((m, HEAD_DIM), BF16),
            pltpu.VMEM((m, tk), F32), pltpu.VMEM((m, tk), F32),
            pltpu.VMEM((m, tk), BF16), pltpu.VMEM((m, tk), BF16),
            pltpu.VMEM((m, LANES), F32), pltpu.VMEM((m, LANES), F32),
            pltpu.VMEM((m, LANES), F32),
            pltpu.VMEM((m, 2 * HEAD_DIM), F32),
        ],
        compiler_params=_cparams(("parallel", "parallel", "parallel")),
        name="attention",
    )(q, k, v)


def _pack_bf16_pairs(a):
    n = a.shape[1] // 2
    bits = pltpu.bitcast(a.astype(BF16).astype(F32), jnp.uint32)
    return (bits[:, :n] >> 16) | (bits[:, n:] & jnp.uint32(0xFFFF0000))


def _unpack_bf16_pairs(w):
    lo = pltpu.bitcast(w << 16, F32)
    hi = pltpu.bitcast(w & jnp.uint32(0xFFFF0000), F32)
    return jnp.concatenate([lo, hi], axis=1)


def _residual_router(y, x, mod, w_out, g2n, rwt, rb, outs, cnt_scr, *, d):
    xo_ref, h2_ref, idx_ref, gate_ref, rank_ref, cnt_ref = outs
    first = jnp.logical_and(pl.program_id(0) == 0, pl.program_id(1) == 0)

    @pl.when(first)
    def _():
        cnt_scr[...] = jnp.zeros(cnt_scr.shape, F32)

    xn = x + mod[:, 2 * d:3 * d] * jnp.dot(y, w_out, preferred_element_type=F32)
    xo_ref[...] = xn
    h2 = _norm_mod(xn, g2n, mod[:, 4 * d:5 * d], mod[:, 3 * d:4 * d])
    h2_ref[...] = _pack_bf16_pairs(h2)
    lg = lax.dot_general(rwt, h2, (((1,), (1,)), ((), ())), precision=HIGHEST,
                         preferred_element_type=F32) + rb
    row = lax.broadcasted_iota(jnp.int32, lg.shape, 0)
    vals, idxs = [], []
    for _ in range(TOP_K):
        m = jnp.max(lg, axis=0, keepdims=True)
        i = jnp.min(jnp.where(lg == m, row, N_EXPERTS), axis=0, keepdims=True)
        vals.append(m)
        idxs.append(i)
        lg = jnp.where(row == i, -jnp.inf, lg)
    es = [jnp.exp(vv - vals[0]) for vv in vals]
    tot = es[0] + es[1] + es[2] + es[3]
    idx_ref[...] = jnp.concatenate(idxs, axis=0)
    gate_ref[...] = jnp.concatenate([e / tot for e in es], axis=0)

    tm = lg.shape[1]
    earlier = (lax.broadcasted_iota(jnp.int32, (tm, tm), 0) < lax.broadcasted_iota(jnp.int32, (tm, tm), 1)).astype(BF16)
    run = cnt_scr[...]
    ranks = []
    for i in idxs:
        hit = row == i
        before = jnp.dot(hit.astype(BF16), earlier, preferred_element_type=F32)
        ranks.append(jnp.sum(jnp.where(hit, run + before, 0.0), axis=0, keepdims=True))
        run = run + jnp.sum(hit.astype(F32), axis=1, keepdims=True)
    cnt_scr[...] = run
    rank_ref[...] = jnp.concatenate(ranks, axis=0).astype(jnp.int32)
    cnt_ref[...] = run.astype(jnp.int32)


def _post_kernel(y_ref, x_ref, mod_ref, w_ref, g_ref, rwt_ref, rb_ref, *rest, d):
    _residual_router(y_ref[...], x_ref[...], mod_ref[...], w_ref[...], g_ref[...], rwt_ref[...], rb_ref[...],
                     rest[:-1], rest[-1], d=d)


def _post_specs(b, t, d, n_lat_tiles):
    nt = t // TM
    row = lambda i, j: (i, j, 0)
    const2 = lambda i, j: (0, 0)
    in_tail = [
        pl.BlockSpec((d, d), const2),
        pl.BlockSpec((1, d), const2),
        pl.BlockSpec((N_EXPERTS, d), const2),
        pl.BlockSpec((N_EXPERTS, 1), const2),
    ]
    per_token = pl.BlockSpec((TOP_K, TM), lambda i, j: (0, i * nt + j))
    out_specs = [
        pl.BlockSpec((None, TM, d), row),
        pl.BlockSpec((None, TM, d // 2), row),
        per_token, per_token, per_token,
        pl.BlockSpec((N_EXPERTS, 1), const2),
    ]
    out_shape = [
        jax.ShapeDtypeStruct((b, t, d), F32),
        jax.ShapeDtypeStruct((b, t, d // 2), jnp.uint32),
        jax.ShapeDtypeStruct((TOP_K, b * t), jnp.int32),
        jax.ShapeDtypeStruct((TOP_K, b * t), F32),
        jax.ShapeDtypeStruct((TOP_K, b * t), jnp.int32),
        jax.ShapeDtypeStruct((N_EXPERTS, 1), jnp.int32),
    ]
    mod_spec = pl.BlockSpec((None, 1, 6 * d), lambda i, j: (jnp.where(j >= n_lat_tiles, b, i), 0, 0))
    return mod_spec, in_tail, out_specs, out_shape


def _post(y, x, mod, w_out, g2n, router_w, router_b, n_lat_tiles):
    b, t, d = x.shape
    mod_spec, in_tail, out_specs, out_shape = _post_specs(b, t, d, n_lat_tiles)
    row = lambda i, j: (i, j, 0)
    return pl.pallas_call(
        functools.partial(_post_kernel, d=d),
        grid=(b, t // TM),
        in_specs=[pl.BlockSpec((None, TM, d), row), pl.BlockSpec((None, TM, d), row), mod_spec] + in_tail,
        out_specs=out_specs,
        out_shape=out_shape,
        scratch_shapes=[pltpu.VMEM((N_EXPERTS, 1), F32)],
        compiler_params=_cparams(("arbitrary", "arbitrary")),
        name="outproj_router",
    )(y, x, mod, w_out, g2n.reshape(1, d), router_w.T, router_b.reshape(N_EXPERTS, 1))


def _gmlp_kernel(x_ref, mod_ref, g1_ref, win_ref, vg_ref, ws_ref, bs_ref, w_ref, g_ref, rwt_ref, rb_ref,
                 *rest, d):
    outs, cnt_scr, y_scr = rest[:-2], rest[-2], rest[-1]
    mod = mod_ref[...]
    x = x_ref[...]
    h = _norm_mod(x, g1_ref[...], mod[:, d:2 * d], mod[:, 0:d]).astype(BF16)
    z = jnp.dot(h, win_ref[...], preferred_element_type=F32)
    z = 0.5 * z * (1.0 + lax.erf(z * (2.0 ** -0.5)))
    width = z.shape[1] // 2
    u, v = z[:, :width], z[:, width:]
    v = (v * lax.rsqrt(jnp.mean(v * v, axis=-1, keepdims=True) + EPS) * vg_ref[...]).astype(BF16)
    gd = width // GMLP_GROUPS
    for n in range(TM // CHUNK):
        r = slice(n * CHUNK, (n + 1) * CHUNK)
        for g in range(GMLP_GROUPS):
            cs = slice(g * gd, (g + 1) * gd)
            mixed = jnp.dot(ws_ref[g], v[r, cs], preferred_element_type=F32) + bs_ref[:, g:g + 1]
            y_scr[r, cs] = (u[r, cs] * mixed).astype(BF16)
    _residual_router(y_scr[...], x, mod, w_ref[...], g_ref[...], rwt_ref[...], rb_ref[...], outs, cnt_scr, d=d)


def _gmlp_layer(x, mod, g1n, w_in, v_gain, w_s, b_s, w_out, g2n, router_w, router_b, n_lat_tiles):
    b, t, d = x.shape
    width = w_in.shape[1] // 2
    mod_spec, in_tail, out_specs, out_shape = _post_specs(b, t, d, n_lat_tiles)
    row = lambda i, j: (i, j, 0)
    const2 = lambda i, j: (0, 0)
    return pl.pallas_call(
        functools.partial(_gmlp_kernel, d=d),
        grid=(b, t // TM),
        in_specs=[
            pl.BlockSpec((None, TM, d), row),
            mod_spec,
            pl.BlockSpec((1, d), const2),
            pl.BlockSpec((d, 2 * width), const2),
            pl.BlockSpec((1, width), const2),
            pl.BlockSpec((GMLP_GROUPS, CHUNK, CHUNK), lambda i, j: (0, 0, 0)),
            pl.BlockSpec((CHUNK, GMLP_GROUPS), const2),
        ] + in_tail,
        out_specs=out_specs,
        out_shape=out_shape,
        scratch_shapes=[pltpu.VMEM((N_EXPERTS, 1), F32), pltpu.VMEM((TM, width), BF16)],
        compiler_params=_cparams(("arbitrary", "arbitrary")),
        name="gmlp_layer",
    )(x, mod, g1n.reshape(1, d), w_in, v_gain.reshape(1, width), w_s, b_s.T, w_out, g2n.reshape(1, d),
      router_w.T, router_b.reshape(N_EXPERTS, 1))


def _filter_mlp_kernel(z_ref, t_ref, w1, b1, w2, b2, w3, b3, w4, b4, fr, dl, kf_ref, kb_ref, *, width, tl):
    dot = functools.partial(jnp.dot, precision=HIGHEST, preferred_element_type=F32)
    f = fr[...]
    a = jnp.sin(f * (dot(z_ref[...], w1[...]) + b1[...]))
    a = jnp.sin(f * (dot(a, w2[...]) + b2[...]))
    a = jnp.sin(f * (dot(a, w3[...]) + b3[...]))
    k = dot(a, w4[...]) + b4[...]
    window = jnp.exp(-t_ref[...] * dl[...])
    kf_ref[...] = k[:, :width] * window
    pos = pl.program_id(0) * tl + lax.broadcasted_iota(jnp.int32, (tl, 1), 0)
    kb_ref[...] = jnp.where(pos == 0, 0.0, k[:, width:] * window)


def _hyena_filters(length, f_w1, f_b1, f_w2, f_b2, f_w3, f_b3, f_w4, f_b4, freq):
    width = f_w4.shape[1] // 2
    hid = f_w1.shape[1]
    emb = 2 * N_BANDS + 1
    t = jnp.linspace(0.0, 1.0, length, dtype=F32)[:, None]
    w = 2.0 * math.pi * jnp.arange(length, dtype=F32)[:, None] / length
    f = jnp.linspace(1e-4, N_BANDS - 1, N_BANDS, dtype=F32)[None, :]
    z = jnp.concatenate([t, jnp.cos(f * w), -jnp.sin(f * w), jnp.zeros((length, hid - emb), F32)], axis=-1)
    w1p = jnp.concatenate([f_w1, jnp.zeros((hid - emb, hid), F32)], axis=0)
    deltas = jnp.abs(jnp.linspace(MIN_DECAY, MAX_DECAY, width, dtype=F32))[None, :]
    tl = min(length, 512)
    full = lambda shape: pl.BlockSpec(shape, lambda i: (0, 0))
    return pl.pallas_call(
        functools.partial(_filter_mlp_kernel, width=width, tl=tl),
        grid=(length // tl,),
        in_specs=[
            pl.BlockSpec((tl, hid), lambda i: (i, 0)),
            pl.BlockSpec((tl, 1), lambda i: (i, 0)),
            full((hid, hid)), full((1, hid)), full((hid, hid)), full((1, hid)), full((hid, hid)), full((1, hid)),
            full((hid, 2 * width)), full((1, 2 * width)), full((1, hid)), full((1, width)),
        ],
        out_specs=[pl.BlockSpec((tl, width), lambda i: (i, 0)), pl.BlockSpec((tl, width), lambda i: (i, 0))],
        out_shape=[jax.ShapeDtypeStruct((length, width), F32), jax.ShapeDtypeStruct((length, width), F32)],
        compiler_params=_cparams(("parallel",)),
        name="hyena_filter_mlp",
    )(z, t, w1p, f_b1.reshape(1, hid), f_w2, f_b2.reshape(1, hid), f_w3, f_b3.reshape(1, hid),
      f_w4, f_b4.reshape(1, 2 * width), freq.reshape(1, hid), deltas)


def _fft_plan(length):
    n = 2 * length
    n1 = n // FFT_N2
    nz = n1 // 2
    ku = -(-(nz + 1) // 8) * 8
    return n, n1, nz, ku


def _fft_tables(length):
    n, n1, nz, ku = _fft_plan(length)
    i2 = np.arange(FFT_N2)[:, None, None]
    k1 = np.arange(ku)[None, :, None]
    i1 = np.arange(nz)[None, None, :]
    phi = 2.0 * np.pi * (((FFT_N2 * i1 + i2) * k1) % n) / n
    fwd = np.concatenate([np.cos(phi), -np.sin(phi)], axis=1)
    wgt = np.where((k1 == 0) | (k1 == nz), 1.0, np.where(k1 < nz, 2.0, 0.0)) / n
    inv = np.concatenate([np.cos(phi) * wgt, -np.sin(phi) * wgt], axis=1).transpose(0, 2, 1)
    th = 2.0 * np.pi * ((np.arange(FFT_N2)[:, None] * np.arange(FFT_N2)[None, :]) % FFT_N2) / FFT_N2
    c, s = np.cos(th), np.sin(th)
    f2 = np.block([[c, s], [-s, c]])
    f2i = np.block([[c, -s], [s, c]])
    as_bf = lambda a: jnp.asarray(a, F32).astype(BF16)
    return as_bf(fwd), as_bf(inv), as_bf(f2), as_bf(f2i)


def _fft_stage1(src_ref, fwd_ref, a_ref, *, nz, ku):
    slab = 2 * FFT_N2

    def body(i2, carry):
        rows = src_ref[pl.ds(i2, nz, stride=FFT_N2), :].astype(BF16)
        r = jnp.dot(fwd_ref[i2], rows, preferred_element_type=F32)
        a_ref[pl.ds(i2, ku, stride=slab), :] = r[:ku]
        a_ref[pl.ds(FFT_N2 + i2, ku, stride=slab), :] = r[ku:]
        return carry

    lax.fori_loop(0, FFT_N2, body, 0, unroll=FFT_UNROLL)


def _short_conv(p_ref, w_ref, b_ref, pad_ref, emit, *, length):
    step = min(length, 512)
    cb = p_ref.shape[-1]
    pad_ref[pl.ds(0, 8), :] = jnp.zeros((8, cb), F32)
    pad_ref[pl.ds(length + 8, 8), :] = jnp.zeros((8, cb), F32)
    for j in range(length // step):
        pad_ref[pl.ds(8 + j * step, step), :] = p_ref[pl.ds(j * step, step), :].astype(F32)
    w = w_ref[...]
    for j in range(length // step):
        r0 = j * step
        val = (pad_ref[pl.ds(r0 + 7, step), :] * w[0:1] + pad_ref[pl.ds(r0 + 8, step), :] * w[1:2]
               + pad_ref[pl.ds(r0 + 9, step), :] * w[2:3] + b_ref[...])
        emit(r0, step, val)


def _kf_kernel(kf_ref, kb_ref, fwd_ref, f2_ref, o_ref, a_ref, *, nz, ku):
    slab = 2 * FFT_N2
    for src, sign in ((kf_ref, 1.0), (kb_ref, -1.0)):
        _fft_stage1(src, fwd_ref, a_ref, nz=nz, ku=ku)

        def body(k1, carry, sign=sign, first=(src is kf_ref)):
            a = a_ref[pl.ds(pl.multiple_of(k1 * slab, slab), slab), :].astype(BF16)
            xk = jnp.dot(f2_ref[...], a, preferred_element_type=F32)
            if first:
                o_ref[k1] = xk
            else:
                o_ref[k1, :FFT_N2, :] = o_ref[k1, :FFT_N2, :] + xk[:FFT_N2]
                o_ref[k1, FFT_N2:, :] = o_ref[k1, FFT_N2:, :] - xk[FFT_N2:]
            return carry

        lax.fori_loop(0, ku, body, 0, unroll=math.gcd(ku, FFT_UNROLL))


def _filter_spectrum(k_fwd, k_bwd, tables):
    length, width = k_fwd.shape
    _, _, nz, ku = _fft_plan(length)
    fwd, _, f2, _ = tables
    blk = pl.BlockSpec((length, FFT_CB), lambda c: (0, c))
    return pl.pallas_call(
        functools.partial(_kf_kernel, nz=nz, ku=ku),
        grid=(width // FFT_CB,),
        in_specs=[
            blk, blk,
            pl.BlockSpec(fwd.shape, lambda c: (0, 0, 0)),
            pl.BlockSpec(f2.shape, lambda c: (0, 0)),
        ],
        out_specs=pl.BlockSpec((ku, 2 * FFT_N2, FFT_CB), lambda c: (0, 0, c)),
        out_shape=jax.ShapeDtypeStruct((ku, 2 * FFT_N2, width), F32),
        scratch_shapes=[pltpu.VMEM((ku * 2 * FFT_N2, FFT_CB), F32)],
        compiler_params=_cparams(("parallel",)),
        name="hyena_filter_spectrum",
    )(k_fwd, k_bwd, fwd, f2)


def _hyena_conv_kernel(px0_ref, px1_ref, pv_ref, w0_ref, w1_ref, wv_ref, b0_ref, b1_ref, bv_ref, dsk_ref,
                       kf_ref, fwd_ref, inv_ref, f2_ref, f2i_ref, o_ref, a_ref, vv_ref, *, length, nz, ku):
    slab = 2 * FFT_N2

    def set_vv(r0, rows, val):
        vv_ref[pl.ds(r0, rows), :] = val

    def mul_vv(r0, rows, val):
        vv_ref[pl.ds(r0, rows), :] = vv_ref[pl.ds(r0, rows), :] * val

    _short_conv(pv_ref, wv_ref, bv_ref, a_ref, set_vv, length=length)
    _short_conv(px1_ref, w1_ref, b1_ref, a_ref, mul_vv, length=length)
    _fft_stage1(vv_ref, fwd_ref, a_ref, nz=nz, ku=ku)

    def freq_body(k1, carry):
        rows = pl.ds(k1 * slab if isinstance(k1, int) else pl.multiple_of(k1 * slab, slab), slab)
        xk = jnp.dot(f2_ref[...], a_ref[rows, :].astype(BF16), preferred_element_type=F32)
        kf = kf_ref[k1]
        xr, xi = xk[:FFT_N2], xk[FFT_N2:]
        kr, ki = kf[:FFT_N2], kf[FFT_N2:]
        prod = jnp.concatenate([xr * kr - xi * ki, xr * ki + xi * kr], axis=0).astype(BF16)
        a_ref[rows, :] = jnp.dot(f2i_ref[...], prod, preferred_element_type=F32)
        return carry

    lax.fori_loop(0, nz, freq_body, 0, unroll=math.gcd(nz, FFT_UNROLL))
    freq_body(nz, 0)

    def time_body(i2, carry):
        re = a_ref[pl.ds(i2, ku, stride=slab), :]
        im = a_ref[pl.ds(FFT_N2 + i2, ku, stride=slab), :]
        q = jnp.concatenate([re, im], axis=0).astype(BF16)
        y = jnp.dot(inv_ref[i2], q, preferred_element_type=F32)
        rows = pl.ds(i2, nz, stride=FFT_N2)
        vv_ref[rows, :] = y + vv_ref[rows, :] * dsk_ref[...]
        return carry

    lax.fori_loop(0, FFT_N2, time_body, 0, unroll=FFT_UNROLL)

    def emit_out(r0, rows, val):
        o_ref[pl.ds(r0, rows), :] = (vv_ref[pl.ds(r0, rows), :] * val).astype(o_ref.dtype)

    _short_conv(px0_ref, w0_ref, b0_ref, a_ref, emit_out, length=length)


def _hyena_conv(p, conv_w, conv_b, d_skip, kf, tables, length):
    b, _, w3 = p.shape
    width = w3 // 3
    ncb = width // FFT_CB
    _, _, nz, ku = _fft_plan(length)
    fwd, inv, f2, f2i = tables
    pblk = lambda part: _resident((None, length, FFT_CB), lambda c, i: (i, 0, part * ncb + c))
    wblk = lambda part: pl.BlockSpec((3, FFT_CB), lambda c, i: (0, part * ncb + c))
    bblk = lambda part: pl.BlockSpec((1, FFT_CB), lambda c, i: (0, part * ncb + c))
    return pl.pallas_call(
        functools.partial(_hyena_conv_kernel, length=length, nz=nz, ku=ku),
        grid=(ncb, b),
        in_specs=[
            pblk(0), pblk(1), pblk(2), wblk(0), wblk(1), wblk(2), bblk(0), bblk(1), bblk(2),
            pl.BlockSpec((1, FFT_CB), lambda c, i: (0, c)),
            _resident((ku, 2 * FFT_N2, FFT_CB), lambda c, i: (0, 0, c)),
            _resident(fwd.shape, lambda c, i: (0, 0, 0)),
            _resident(inv.shape, lambda c, i: (0, 0, 0)),
            _resident(f2.shape, lambda c, i: (0, 0)),
            _resident(f2i.shape, lambda c, i: (0, 0)),
        ],
        out_specs=pl.BlockSpec((None, length, FFT_CB), lambda c, i: (i, 0, c)),
        out_shape=jax.ShapeDtypeStruct((b, length, width), BF16),
        scratch_shapes=[
            pltpu.VMEM((max(ku * 2 * FFT_N2, length + 16), FFT_CB), F32),
            pltpu.VMEM((length, FFT_CB), F32),
        ],
        compiler_params=_cparams(("arbitrary", "arbitrary")),
        name="hyena_long_conv",
    )(p, p, p, conv_w, conv_w, conv_w, conv_b.reshape(1, w3), conv_b.reshape(1, w3), conv_b.reshape(1, w3),
      d_skip.reshape(1, width), kf, fwd, inv, f2, f2i)


def _dense_dft_tables(length):
    n = 2 * length
    th = 2.0 * np.pi * ((np.arange(n)[:, None] * np.arange(length)[None, :]) % n) / n
    fwd = np.concatenate([np.cos(th), -np.sin(th)], axis=0)
    inv = np.concatenate([np.cos(th), -np.sin(th)], axis=0).T / n
    return jnp.asarray(fwd, F32).astype(BF16), jnp.asarray(inv, F32).astype(BF16)


def _kf_dense_kernel(kf_ref, kb_ref, fwd_ref, o_ref, *, n):
    xf = jnp.dot(fwd_ref[...], kf_ref[...].astype(BF16), preferred_element_type=F32)
    xb = jnp.dot(fwd_ref[...], kb_ref[...].astype(BF16), preferred_element_type=F32)
    o_ref[:n, :] = xf[:n] + xb[:n]
    o_ref[n:, :] = xf[n:] - xb[n:]


def _filter_spectrum_dense(k_fwd, k_bwd, fwd):
    length, width = k_fwd.shape
    n = 2 * length
    blk = pl.BlockSpec((length, FFT_CB), lambda c: (0, c))
    return pl.pallas_call(
        functools.partial(_kf_dense_kernel, n=n),
        grid=(width // FFT_CB,),
        in_specs=[blk, blk, pl.BlockSpec(fwd.shape, lambda c: (0, 0))],
        out_specs=pl.BlockSpec((2 * n, FFT_CB), lambda c: (0, c)),
        out_shape=jax.ShapeDtypeStruct((2 * n, width), F32),
        compiler_params=_cparams(("parallel",)),
        name="hyena_ctx_filter_spectrum",
    )(k_fwd, k_bwd, fwd)


def _hyena_ctx_kernel(px0_ref, px1_ref, pv_ref, w0_ref, w1_ref, wv_ref, b0_ref, b1_ref, bv_ref, dsk_ref,
                      kf_ref, fwd_ref, inv_ref, o_ref, pad_ref, vv_ref, *, length):
    n = 2 * length

    def set_vv(r0, rows, val):
        vv_ref[pl.ds(r0, rows), :] = val

    def mul_vv(r0, rows, val):
        vv_ref[pl.ds(r0, rows), :] = vv_ref[pl.ds(r0, rows), :] * val

    _short_conv(pv_ref, wv_ref, bv_ref, pad_ref, set_vv, length=length)
    _short_conv(px1_ref, w1_ref, b1_ref, pad_ref, mul_vv, length=length)
    vv = vv_ref[...]
    xk = jnp.dot(fwd_ref[...], vv.astype(BF16), preferred_element_type=F32)
    kf = kf_ref[...]
    xr, xi, kr, ki = xk[:n], xk[n:], kf[:n], kf[n:]
    prod = jnp.concatenate([xr * kr - xi * ki, xr * ki + xi * kr], axis=0).astype(BF16)
    vv_ref[...] = jnp.dot(inv_ref[...], prod, preferred_element_type=F32) + vv * dsk_ref[...]

    def emit_out(r0, rows, val):
        o_ref[pl.ds(r0, rows), :] = (vv_ref[pl.ds(r0, rows), :] * val).astype(o_ref.dtype)

    _short_conv(px0_ref, w0_ref, b0_ref, pad_ref, emit_out, length=length)


def _hyena_ctx_conv(p, conv_w, conv_b, d_skip, kf, fwd, inv, n_lat, length):
    b, _, w3 = p.shape
    width = w3 // 3
    ncb = width // FFT_CB
    rb = n_lat // length
    pblk = lambda part: pl.BlockSpec((None, length, FFT_CB), lambda c, i: (i, rb, part * ncb + c))
    wblk = lambda part: pl.BlockSpec((3, FFT_CB), lambda c, i: (0, part * ncb + c))
    bblk = lambda part: pl.BlockSpec((1, FFT_CB), lambda c, i: (0, part * ncb + c))
    return pl.pallas_call(
        functools.partial(_hyena_ctx_kernel, length=length),
        grid=(ncb, b),
        in_specs=[
            pblk(0), pblk(1), pblk(2), wblk(0), wblk(1), wblk(2), bblk(0), bblk(1), bblk(2),
            pl.BlockSpec((1, FFT_CB), lambda c, i: (0, c)),
            pl.BlockSpec((4 * length, FFT_CB), lambda c, i: (0, c)),
            pl.BlockSpec(fwd.shape, lambda c, i: (0, 0)),
            pl.BlockSpec(inv.shape, lambda c, i: (0, 0)),
        ],
        out_specs=pl.BlockSpec((None, length, FFT_CB), lambda c, i: (i, 0, c)),
        out_shape=jax.ShapeDtypeStruct((b, length, width), BF16),
        scratch_shapes=[pltpu.VMEM((length + 16, FFT_CB), F32), pltpu.VMEM((length, FFT_CB), F32)],
        compiler_params=_cparams(("arbitrary", "arbitrary")),
        name="hyena_ctx_conv",
    )(p, p, p, conv_w, conv_w, conv_w, conv_b.reshape(1, w3), conv_b.reshape(1, w3), conv_b.reshape(1, w3),
      d_skip.reshape(1, width), kf, fwd, inv)


def _row_copy(src, dst, sem):
    return pltpu.make_async_copy(src, dst, sem)


def _scatter_kernel(cnt_ref, start_ref, dest_ref, h_ref, x_hbm, src0, src1, src2, src3, zero_scr, sem, zsem, *, n_tok):
    rows, half = h_ref.shape
    src_scr = (src0, src1, src2, src3)
    n_real = TOP_K * n_tok
    row_iota = lax.broadcasted_iota(jnp.int32, (rows, LANES), 0)

    def ids_from(base):
        return pltpu.bitcast(base + row_iota, jnp.uint32)

    @pl.when(pl.program_id(0) == 0)
    def _():
        zero_scr[:, :half] = jnp.zeros((rows, half), jnp.uint32)

        def per_expert(e, total):
            cnt = cnt_ref[e]
            n_pad = (EXPERT_BLOCK - cnt % EXPERT_BLOCK) % EXPERT_BLOCK
            base = start_ref[e] + cnt
            zero_scr[:, half:] = ids_from(n_real + total)

            def one(r, carry):
                _row_copy(zero_scr.at[pl.ds(r, 1)], x_hbm.at[pl.ds(base + r, 1)], zsem).start()
                return carry

            def wait_one(r, carry):
                _row_copy(zero_scr.at[pl.ds(0, 1)], x_hbm.at[pl.ds(0, 1)], zsem).wait()
                return carry

            lax.fori_loop(0, n_pad, one, 0)
            lax.fori_loop(0, n_pad, wait_one, 0)
            return total + n_pad

        total = lax.fori_loop(0, N_EXPERTS, per_expert, 0)
        first_free = start_ref[N_EXPERTS] // EXPERT_BLOCK
        n_blocks = x_hbm.shape[0] // EXPERT_BLOCK

        def tail(i, carry):
            zero_scr[:, half:] = ids_from(n_real + total + (i - first_free) * EXPERT_BLOCK)
            cp = _row_copy(zero_scr, x_hbm.at[pl.ds(pl.multiple_of(i * EXPERT_BLOCK, EXPERT_BLOCK), EXPERT_BLOCK)],
                           zsem)
            cp.start()
            cp.wait()
            return carry

        lax.fori_loop(first_free, n_blocks, tail, 0)

    h = h_ref[...]
    for k in range(TOP_K):
        src_scr[k][:, :half] = h
        src_scr[k][:, half:] = ids_from(k * n_tok + pl.program_id(0) * rows)

    def issue(r, carry):
        for k in range(TOP_K):
            _row_copy(src_scr[k].at[pl.ds(r, 1)], x_hbm.at[pl.ds(dest_ref[0, r * TOP_K + k], 1)], sem).start()
        return carry

    lax.fori_loop(0, rows, issue, 0, unroll=4)
    for k in range(TOP_K):
        _row_copy(src_scr[k], x_hbm.at[pl.ds(0, rows)], sem).wait()


def _dispatch(h2p, dest_tiles, counts, pad_start, n_blocks):
    n_tok, half = h2p.shape
    assert TM == EXPERT_BLOCK
    width = half + LANES
    return pl.pallas_call(
        functools.partial(_scatter_kernel, n_tok=n_tok),
        grid_spec=pltpu.PrefetchScalarGridSpec(
            num_scalar_prefetch=2,
            grid=(n_tok // TM,),
            in_specs=[
                pl.BlockSpec((None, 1, TOP_K * TM), lambda i, c, s: (i, 0, 0), memory_space=pltpu.SMEM),
                pl.BlockSpec((TM, half), lambda i, c, s: (i, 0)),
            ],
            out_specs=pl.BlockSpec(memory_space=pl.ANY),
            scratch_shapes=[pltpu.VMEM((TM, width), jnp.uint32)] * TOP_K + [
                pltpu.VMEM((EXPERT_BLOCK, width), jnp.uint32), pltpu.SemaphoreType.DMA(()), pltpu.SemaphoreType.DMA(())],
        ),
        out_shape=jax.ShapeDtypeStruct((n_blocks * EXPERT_BLOCK, width), jnp.uint32),
        compiler_params=_cparams(("arbitrary",)),
        name="moe_dispatch",
    )(counts, pad_start, dest_tiles, h2p)


def _ffn_kernel(be_ref, ids_prev_ref, ids_last_ref, x_ref, w1_ref, b1_ref, w2_ref, b2_ref, y_hbm,
                buf0, buf1, w1_bf, w2_bf, sem0, sem1, *, d_ff, half):
    i = pl.program_id(0)
    n = pl.num_programs(0)
    rows = x_ref.shape[0]
    spare = y_hbm.shape[0] - 2 * rows

    def drain(src, sem, ids_ref):
        for r in range(rows):
            _row_copy(src.at[pl.ds(r, 1)], y_hbm.at[pl.ds(ids_ref[0, r], 1)], sem).start()

    def wait(src, sem):
        _row_copy(src, y_hbm.at[pl.ds(0, rows)], sem).wait()

    @pl.when(i == 0)
    def _():
        buf0[...] = jnp.zeros(buf0.shape, buf0.dtype)
        buf1[...] = jnp.zeros(buf1.shape, buf1.dtype)
        _row_copy(buf0, y_hbm.at[pl.ds(spare + rows, rows)], sem0).start()

    @pl.when(jnp.logical_or(i == 0, be_ref[i] != be_ref[jnp.maximum(i - 1, 0)]))
    def _():
        w1_bf[...] = w1_ref[...].astype(BF16)
        w2_bf[...] = w2_ref[...].astype(BF16)

    def step(src, ssem, dst, dsem):
        drain(src, ssem, ids_prev_ref)
        x = _unpack_bf16_pairs(x_ref[:, :half]).astype(BF16)
        hh = jnp.dot(x, w1_bf[...], preferred_element_type=F32) + b1_ref[...]
        glu = jnp.minimum(hh[:, :d_ff], SWIGLU_LIMIT)
        lin = jnp.clip(hh[:, d_ff:], -SWIGLU_LIMIT, SWIGLU_LIMIT)
        act = (glu * jax.nn.sigmoid(SWIGLU_ALPHA * glu) * (lin + 1.0)).astype(BF16)
        y = _pack_bf16_pairs(jnp.dot(act, w2_bf[...], preferred_element_type=F32) + b2_ref[...])
        wait(dst, dsem)
        dst[...] = y

    @pl.when(i % 2 == 0)
    def _():
        step(buf1, sem1, buf0, sem0)

    @pl.when(i % 2 == 1)
    def _():
        step(buf0, sem0, buf1, sem1)

    @pl.when(i == n - 1)
    def _():
        last_even = (y_hbm.shape[0] // rows - 3) % 2 == 0
        last, lsem, prev, psem = (buf0, sem0, buf1, sem1) if last_even else (buf1, sem1, buf0, sem0)
        drain(last, lsem, ids_last_ref)
        wait(prev, psem)
        wait(last, lsem)


def _expert_ffn(x_disp, ids, block_e, layer, w1, b1, w2, b2):
    rows, width = x_disp.shape
    half = width - LANES
    depth, n_e, d, ff2 = w1.shape
    d_ff = ff2 // 2
    n_blocks = rows // EXPERT_BLOCK
    ids_prev = jnp.concatenate([rows + jnp.arange(EXPERT_BLOCK, dtype=jnp.int32), ids])
    ids_prev = ids_prev.reshape(n_blocks + 1, 1, EXPERT_BLOCK)
    return pl.pallas_call(
        functools.partial(_ffn_kernel, d_ff=d_ff, half=half),
        grid_spec=pltpu.PrefetchScalarGridSpec(
            num_scalar_prefetch=1,
            grid=(n_blocks,),
            in_specs=[
                pl.BlockSpec((None, 1, EXPERT_BLOCK), lambda i, be: (i, 0, 0), memory_space=pltpu.SMEM),
                pl.BlockSpec((None, 1, EXPERT_BLOCK), lambda i, be: (n_blocks, 0, 0), memory_space=pltpu.SMEM),
                pl.BlockSpec((EXPERT_BLOCK, width), lambda i, be: (i, 0)),
                pl.BlockSpec((None, None, d, ff2), lambda i, be: (layer, be[i], 0, 0)),
                pl.BlockSpec((None, None, 1, ff2), lambda i, be: (layer, be[i], 0, 0)),
                pl.BlockSpec((None, None, d_ff, d), lambda i, be: (layer, be[i], 0, 0)),
                pl.BlockSpec((None, None, 1, d), lambda i, be: (layer, be[i], 0, 0)),
            ],
            out_specs=pl.BlockSpec(memory_space=pl.ANY),
            scratch_shapes=[pltpu.VMEM((EXPERT_BLOCK, half), jnp.uint32), pltpu.VMEM((EXPERT_BLOCK, half), jnp.uint32),
                            pltpu.VMEM((d, ff2), BF16), pltpu.VMEM((d_ff, d), BF16),
                            pltpu.SemaphoreType.DMA(()), pltpu.SemaphoreType.DMA(())],
        ),
        out_shape=jax.ShapeDtypeStruct((rows + 2 * EXPERT_BLOCK, half), jnp.uint32),
        compiler_params=_cparams(("arbitrary",)),
        name="moe_expert_ffn",
    )(block_e, ids_prev, ids_prev, x_disp, w1, b1.reshape(depth, n_e, 1, ff2), w2, b2.reshape(depth, n_e, 1, d))


def _combine_kernel(y0_ref, y1_ref, y2_ref, y3_ref, gate_ref, x_ref, mod_ref, fg_ref, o_ref, *, d, final):
    gate = gate_ref[...]
    out = _unpack_bf16_pairs(y0_ref[...]) * gate[:, 0:1]
    for k, y_ref in enumerate((y1_ref, y2_ref, y3_ref), start=1):
        out = out + _unpack_bf16_pairs(y_ref[...]) * gate[:, k:k + 1]
    xn = x_ref[...] + mod_ref[:, 5 * d:6 * d] * out
    if final:
        xn = xn * lax.rsqrt(jnp.mean(xn * xn, axis=-1, keepdims=True) + EPS) * fg_ref[...]
    o_ref[...] = xn


def _combine(y_rows, gates, x, mod, final_g, n_lat_tiles, final):
    b, t, d = x.shape
    nt = t // TM
    n_tiles = b * nt
    nt_out = n_lat_tiles if final else nt
    y_spec = lambda k: pl.BlockSpec((TM, d // 2), lambda i, j: (k * n_tiles + i * nt + j, 0))
    return pl.pallas_call(
        functools.partial(_combine_kernel, d=d, final=final),
        grid=(b, nt_out),
        in_specs=[
            y_spec(0), y_spec(1), y_spec(2), y_spec(3),
            pl.BlockSpec((TM, TOP_K), lambda i, j: (i * nt + j, 0)),
            pl.BlockSpec((None, TM, d), lambda i, j: (i, j, 0)),
            pl.BlockSpec((None, 1, 6 * d), lambda i, j: (jnp.where(j >= n_lat_tiles, b, i), 0, 0)),
            pl.BlockSpec((1, d), lambda i, j: (0, 0)),
        ],
        out_specs=pl.BlockSpec((None, TM, d), lambda i, j: (i, j, 0)),
        out_shape=jax.ShapeDtypeStruct((b, nt_out * TM, d), F32),
        compiler_params=_cparams(("parallel", "parallel")),
        name="moe_combine",
    )(y_rows, y_rows, y_rows, y_rows, gates, x, mod, final_g.reshape(1, d))


def _moe(h2p, idx_t, gates_t, rank_t, cnt, x, mod, layer, w1, b1, w2, b2, final_g, n_lat_tiles, final):
    b, t, d = x.shape
    n_tok = b * t
    n_assign = n_tok * TOP_K
    counts = cnt[:, 0]
    padded = (counts + EXPERT_BLOCK - 1) // EXPERT_BLOCK * EXPERT_BLOCK
    pad_end = jnp.cumsum(padded).astype(jnp.int32)
    pad_start = pad_end - padded
    experts = jnp.arange(N_EXPERTS, dtype=jnp.int32)
    dest_t = rank_t + jnp.sum(jnp.where(idx_t[..., None] == experts, pad_start, 0), axis=-1)
    n_blocks = -(-n_assign // EXPERT_BLOCK) + N_EXPERTS
    block_start = jnp.arange(n_blocks, dtype=jnp.int32) * EXPERT_BLOCK
    block_e = jnp.minimum(jnp.sum(pad_end[None, :] <= block_start[:, None], axis=1), N_EXPERTS - 1).astype(jnp.int32)
    dest_tiles = dest_t.reshape(TOP_K, n_tok // TM, TM).transpose(1, 2, 0).reshape(n_tok // TM, 1, TOP_K * TM)

    x_disp = _dispatch(h2p.reshape(n_tok, d // 2), dest_tiles, counts,
                       jnp.concatenate([pad_start, pad_end[-1:]]), n_blocks)
    ids = lax.bitcast_convert_type(x_disp[:, d // 2], jnp.int32)
    y_rows = _expert_ffn(x_disp, ids, block_e, layer, w1, b1, w2, b2)
    return _combine(y_rows, gates_t.T, x, mod, final_g, n_lat_tiles, final)


def _rope_tables(n_lat, n_ctx):
    rows = n_lat // GRID_W
    row = jnp.repeat(jnp.arange(rows, dtype=F32), GRID_W)
    col = jnp.tile(jnp.arange(GRID_W, dtype=F32), rows)
    axis_dim = HEAD_DIM // 2
    inv = ROPE_THETA ** (-jnp.arange(0, axis_dim, 2, dtype=F32) / axis_dim)
    ang_r = row[:, None] * inv
    ang_c = col[:, None] * inv
    ang = jnp.concatenate([ang_r, ang_r, ang_c, ang_c], axis=-1)
    cos = jnp.concatenate([jnp.cos(ang), jnp.ones((n_ctx, HEAD_DIM), F32)], axis=0)
    sin = jnp.concatenate([jnp.sin(ang), jnp.zeros((n_ctx, HEAD_DIM), F32)], axis=0)
    even = (jnp.arange(HEAD_DIM) // 32) % 2 == 0
    return cos, jnp.where(even, -sin, 0.0), jnp.where(even, 0.0, sin)


def kernel(x, c, ctx, c_ctx, ada_w, ada_b, norm1_g, norm2_g, mix_w_out, router_w, router_b, exp_w1, exp_b1, exp_w2, exp_b2, attn_w_in, attn_q_gain, attn_k_gain, gmlp_w_in, gmlp_v_gain, gmlp_w_s, gmlp_b_s, hyena_w_in, hyena_conv_w, hyena_conv_b, hyena_f_w1, hyena_f_b1, hyena_f_w2, hyena_f_b2, hyena_f_w3, hyena_f_b3, hyena_f_w4, hyena_f_b4, hyena_freq, hyena_d, final_g):
    b, s, d = x.shape
    n_ctx = ctx.shape[1]
    depth = ada_w.shape[0]
    assert s % TM == 0 and n_ctx == TM and b < MOD_ROWS and s % (FFT_N2 * 8) == 0
    n_lat_tiles = s // TM

    xs = jnp.concatenate([x, ctx], axis=1)
    cc = jnp.concatenate([c, c_ctx[None, :], jnp.zeros((MOD_ROWS - b - 1, d), F32)], axis=0)
    mods = _modulation(cc, ada_w, ada_b).reshape(depth, MOD_ROWS, 1, 6 * d)
    rope = _rope_tables(s, n_ctx)

    for i in range(depth):
        kind, j = i % 3, i // 3
        last = i == depth - 1
        mod = mods[i]
        w_out = mix_w_out[i].astype(BF16)
        if kind == 0:
            q, k, v = _attn_proj(xs, mod, norm1_g[i], attn_w_in[j].astype(BF16), attn_q_gain[j], attn_k_gain[j],
                                 rope, n_lat_tiles)
            y = _attention(q, k, v, s)
        elif kind == 2:
            p = _inproj(xs, mod, norm1_g[i], hyena_w_in[j].astype(BF16), n_lat_tiles)
            fargs = (hyena_f_w1[j], hyena_f_b1[j], hyena_f_w2[j], hyena_f_b2[j], hyena_f_w3[j], hyena_f_b3[j],
                     hyena_f_w4[j], hyena_f_b4[j], hyena_freq[j])
            tables = _fft_tables(s)
            kf = _filter_spectrum(*_hyena_filters(s, *fargs), tables)
            y_lat = _hyena_conv(p, hyena_conv_w[j], hyena_conv_b[j], hyena_d[j], kf, tables, s)
            dfwd, dinv = _dense_dft_tables(n_ctx)
            kf_c = _filter_spectrum_dense(*_hyena_filters(n_ctx, *fargs), dfwd)
            y_ctx = _hyena_ctx_conv(p, hyena_conv_w[j], hyena_conv_b[j], hyena_d[j], kf_c, dfwd, dinv, s, n_ctx)
            y = jnp.concatenate([y_lat, y_ctx], axis=1)
        if kind == 1:
            xs, *routed = _gmlp_layer(
                xs, mod, norm1_g[i], gmlp_w_in[j].astype(BF16), gmlp_v_gain[j], gmlp_w_s[j].astype(BF16),
                gmlp_b_s[j], w_out, norm2_g[i], router_w[i], router_b[i], n_lat_tiles)
        else:
            xs, *routed = _post(y, xs, mod, w_out, norm2_g[i], router_w[i], router_b[i], n_lat_tiles)
        xs = _moe(*routed, xs, mod, i, exp_w1, exp_b1, exp_w2, exp_b2, final_g, n_lat_tiles, last)
    return xs
```

```python
import functools
import math

import jax
import jax.numpy as jnp
import numpy as np
from jax import lax
from jax.experimental import pallas as pl
from jax.experimental.pallas import tpu as pltpu

F32 = jnp.float32
BF16 = jnp.bfloat16
HIGHEST = lax.Precision.HIGHEST

EPS = 1e-6
GRID_W = 64
ROPE_THETA = 10000.0
HEAD_DIM = 128
N_KV_HEADS = 2
Q_GROUP = 4
CHUNK = 128
GMLP_GROUPS = 8
N_EXPERTS = 32
TOP_K = 4
SWIGLU_ALPHA = 1.702
SWIGLU_LIMIT = 7.0
N_BANDS = 16
DECAY_TARGET = 1e-2
MAX_DECAY = math.log(DECAY_TARGET) / 0.3
MIN_DECAY = math.log(DECAY_TARGET) / 1.5

LANES = 128
TM = 256
ATTN_TK = 512
MOD_ROWS = 16
EXPERT_BLOCK = 256
FFT_N2 = 128
FFT_CB = 128
FFT_UNROLL = 16
VMEM_LIMIT = 56 * 1024 * 1024


def _cparams(sem):
    return pltpu.CompilerParams(dimension_semantics=sem, vmem_limit_bytes=VMEM_LIMIT)


def _norm_mod(x, g, scale, shift):
    y = x * lax.rsqrt(jnp.mean(x * x, axis=-1, keepdims=True) + EPS)
    return y * g * (1.0 + scale) + shift


def _resident(shape, index_map):
    return pl.BlockSpec(shape, index_map, pipeline_mode=pl.Buffered(1))


def _mod_kernel(c_ref, w_ref, b_ref, o_ref):
    c = c_ref[...]
    s = c * jax.nn.sigmoid(c)
    o_ref[...] = jnp.dot(s, w_ref[...], precision=HIGHEST, preferred_element_type=F32) + b_ref[...]


def _modulation(cc, ada_w, ada_b):
    depth, d, d6 = ada_w.shape
    tn = 1536
    return pl.pallas_call(
        _mod_kernel,
        grid=(depth, d6 // tn),
        in_specs=[
            pl.BlockSpec((MOD_ROWS, d), lambda l, j: (0, 0)),
            pl.BlockSpec((None, d, tn), lambda l, j: (l, 0, j)),
            pl.BlockSpec((None, 1, tn), lambda l, j: (l, 0, j)),
        ],
        out_specs=pl.BlockSpec((None, MOD_ROWS, tn), lambda l, j: (l, 0, j)),
        out_shape=jax.ShapeDtypeStruct((depth, MOD_ROWS, d6), F32),
        compiler_params=_cparams(("arbitrary", "arbitrary")),
        name="adaln_mod",
    )(cc, ada_w, ada_b.reshape(depth, 1, d6))


def _inproj_kernel(x_ref, mod_ref, g_ref, w_ref, o_ref, *, d):
    mod = mod_ref[...]
    h = _norm_mod(x_ref[...], g_ref[...], mod[:, d:2 * d], mod[:, 0:d]).astype(BF16)
    o_ref[...] = jnp.dot(h, w_ref[...], preferred_element_type=F32).astype(o_ref.dtype)


def _inproj(x, mod, g, w, n_lat_tiles):
    b, t, d = x.shape
    n_out = w.shape[1]
    return pl.pallas_call(
        functools.partial(_inproj_kernel, d=d),
        grid=(b, t // TM),
        in_specs=[
            pl.BlockSpec((None, TM, d), lambda i, j: (i, j, 0)),
            pl.BlockSpec((None, 1, 6 * d), lambda i, j: (jnp.where(j >= n_lat_tiles, b, i), 0, 0)),
            pl.BlockSpec((1, d), lambda i, j: (0, 0)),
            pl.BlockSpec((d, n_out), lambda i, j: (0, 0)),
        ],
        out_specs=pl.BlockSpec((None, TM, n_out), lambda i, j: (i, j, 0)),
        out_shape=jax.ShapeDtypeStruct((b, t, n_out), BF16),
        compiler_params=_cparams(("parallel", "parallel")),
        name="inproj",
    )(x, mod, g.reshape(1, d), w)


def _attn_proj_kernel(x_ref, mod_ref, g_ref, w_ref, qg_ref, kg_ref, cos_ref, sa_ref, sb_ref,
                      q_ref, k_ref, v_ref, *, d):
    mod = mod_ref[...]
    h = _norm_mod(x_ref[...], g_ref[...], mod[:, d:2 * d], mod[:, 0:d]).astype(BF16)
    p = jnp.dot(h, w_ref[...], preferred_element_type=F32)
    cos, sin_a, sin_b = cos_ref[...], sa_ref[...], sb_ref[...]
    n_q = Q_GROUP * N_KV_HEADS

    def head(ph, gain):
        y = ph * lax.rsqrt(jnp.mean(ph * ph, axis=-1, keepdims=True) + EPS) * gain
        return y * cos + pltpu.roll(y, LANES - 32, 1) * sin_a + pltpu.roll(y, 32, 1) * sin_b

    qg = qg_ref[...] * (HEAD_DIM ** -0.5 * math.log2(math.e))
    kg = kg_ref[...]
    for i in range(n_q):
        q_ref[:, i * HEAD_DIM:(i + 1) * HEAD_DIM] = head(p[:, i * HEAD_DIM:(i + 1) * HEAD_DIM], qg).astype(BF16)
    k0 = n_q * HEAD_DIM
    for i in range(N_KV_HEADS):
        k_ref[:, i * HEAD_DIM:(i + 1) * HEAD_DIM] = head(
            p[:, k0 + i * HEAD_DIM:k0 + (i + 1) * HEAD_DIM], kg).astype(BF16)
    v0 = k0 + N_KV_HEADS * HEAD_DIM
    for i in range(N_KV_HEADS):
        v_ref[:, 2 * i * HEAD_DIM:(2 * i + 1) * HEAD_DIM] = p[:, v0 + i * HEAD_DIM:v0 + (i + 1) * HEAD_DIM].astype(BF16)
        v_ref[:, (2 * i + 1) * HEAD_DIM:(2 * i + 2) * HEAD_DIM] = jnp.ones((p.shape[0], HEAD_DIM), BF16)


def _attn_proj(x, mod, g, w, q_gain, k_gain, rope, n_lat_tiles):
    b, t, d = x.shape
    n_out = w.shape[1]
    qw = Q_GROUP * N_KV_HEADS * HEAD_DIM
    kw = N_KV_HEADS * HEAD_DIM
    row = lambda i, j: (i, j, 0)
    tab = pl.BlockSpec((TM, HEAD_DIM), lambda i, j: (j, 0))
    return pl.pallas_call(
        functools.partial(_attn_proj_kernel, d=d),
        grid=(b, t // TM),
        in_specs=[
            pl.BlockSpec((None, TM, d), row),
            pl.BlockSpec((None, 1, 6 * d), lambda i, j: (jnp.where(j >= n_lat_tiles, b, i), 0, 0)),
            pl.BlockSpec((1, d), lambda i, j: (0, 0)),
            pl.BlockSpec((d, n_out), lambda i, j: (0, 0)),
            pl.BlockSpec((1, HEAD_DIM), lambda i, j: (0, 0)),
            pl.BlockSpec((1, HEAD_DIM), lambda i, j: (0, 0)),
            tab, tab, tab,
        ],
        out_specs=[
            pl.BlockSpec((None, TM, qw), row),
            pl.BlockSpec((None, TM, kw), row),
            pl.BlockSpec((None, TM, 2 * kw), row),
        ],
        out_shape=[
            jax.ShapeDtypeStruct((b, t, qw), BF16),
            jax.ShapeDtypeStruct((b, t, kw), BF16),
            jax.ShapeDtypeStruct((b, t, 2 * kw), BF16),
        ],
        compiler_params=_cparams(("parallel", "parallel")),
        name="attn_proj",
    )(x, mod, g.reshape(1, d), w, q_gain.reshape(1, HEAD_DIM), k_gain.reshape(1, HEAD_DIM), *rope)


def _attn_kernel(q_ref, k_ref, v_ref, o_ref, q_scr, s0, s1, p0, p1, a0, a1, m_scr, acc_scr, *, n_lat, n_ctx, tk):
    is_ctx = pl.program_id(2) * TM >= n_lat
    s_b, p_b, a_b = (s0, s1), (p0, p1), (a0, a1)
    for g in range(Q_GROUP):
        q_scr[g * TM:(g + 1) * TM, :] = q_ref[:, g * HEAD_DIM:(g + 1) * HEAD_DIM]
    m_scr[...] = jnp.full(m_scr.shape, -jnp.inf, F32)
    acc_scr[...] = jnp.zeros(acc_scr.shape, F32)

    def scores(start, size, slot):
        k = k_ref[pl.ds(start, size), :]
        s_b[slot][:, :size] = lax.dot_general(q_scr[...], k, (((1,), (1,)), ((), ())), preferred_element_type=F32)

    def softmax(size, slot):
        nb = size // LANES
        s = s_b[slot][:, :size]
        mx = s[:, 0:LANES]
        for j in range(1, nb):
            mx = jnp.maximum(mx, s[:, j * LANES:(j + 1) * LANES])
        m_prev = m_scr[...]
        m_new = jnp.maximum(m_prev, jnp.max(mx, axis=-1, keepdims=True))
        a_b[slot][...] = jnp.exp2(m_prev - m_new)
        m_scr[...] = m_new
        m_rep = jnp.concatenate([m_new] * nb, axis=1) if nb > 1 else m_new
        p_b[slot][:, :size] = jnp.exp2(s - m_rep).astype(BF16)

    def values(start, size, slot):
        a = a_b[slot][...]
        acc_scr[...] = jnp.concatenate([a, a], axis=1) * acc_scr[...] + jnp.dot(
            p_b[slot][:, :size], v_ref[pl.ds(start, size), :], preferred_element_type=F32)

    n = n_lat // tk

    @pl.when(is_ctx)
    def _():
        scores(n_lat, n_ctx, 0)
        softmax(n_ctx, 0)
        values(n_lat, n_ctx, 0)

    @pl.when(jnp.logical_not(is_ctx))
    def _():
        scores(0, tk, 0)
        softmax(tk, 0)
        scores(tk, tk, 1)

        def pair(i, carry):
            j = 2 + 2 * i
            values(pl.multiple_of((j - 2) * tk, tk), tk, 0)
            softmax(tk, 1)
            scores(pl.multiple_of(j * tk, tk), tk, 0)
            values(pl.multiple_of((j - 1) * tk, tk), tk, 1)
            softmax(tk, 0)
            scores(pl.multiple_of((j + 1) * tk, tk), tk, 1)
            return carry

        lax.fori_loop(0, (n - 2) // 2, pair, 0)
        values((n - 2) * tk, tk, 0)
        softmax(tk, 1)
        scores(n_lat, n_ctx, 0)
        values((n - 1) * tk, tk, 1)
        softmax(n_ctx, 0)
        values(n_lat, n_ctx, 0)

    out = acc_scr[:, :HEAD_DIM] / acc_scr[:, HEAD_DIM:]
    for g in range(Q_GROUP):
        o_ref[:, g * HEAD_DIM:(g + 1) * HEAD_DIM] = out[g * TM:(g + 1) * TM].astype(o_ref.dtype)


def _attention(q, k, v, n_lat):
    b, t, _ = q.shape
    n_ctx = t - n_lat
    gw = Q_GROUP * HEAD_DIM
    tk = ATTN_TK
    assert n_lat % (2 * tk) == 0 and n_ctx <= tk and n_ctx % LANES == 0
    m = Q_GROUP * TM
    return pl.pallas_call(
        functools.partial(_attn_kernel, n_lat=n_lat, n_ctx=n_ctx, tk=tk),
        grid=(b, N_KV_HEADS, t // TM),
        in_specs=[
            pl.BlockSpec((None, TM, gw), lambda i, h, j: (i, j, h)),
            pl.BlockSpec((None, t, HEAD_DIM), lambda i, h, j: (i, 0, h)),
            pl.BlockSpec((None, t, 2 * HEAD_DIM), lambda i, h, j: (i, 0, h)),
        ],
        out_specs=pl.BlockSpec((None, TM, gw), lambda i, h, j: (i, j, h)),
        out_shape=jax.ShapeDtypeStruct(q.shape, BF16),
        scratch_shapes=[
            pltpu.VMEM((m, HEAD_DIM), BF16),
            pltpu.VMEM((m, tk), F32), pltpu.VMEM((m, tk), F32),
            pltpu.VMEM((m, tk), BF16), pltpu.VMEM((m, tk), BF16),
            pltpu.VMEM((m, LANES), F32), pltpu.VMEM((m, LANES), F32),
            pltpu.VMEM((m, LANES), F32),
            pltpu.VMEM((m, 2 * HEAD_DIM), F32),
        ],
        compiler_params=_cparams(("parallel", "parallel", "parallel")),
        name="attention",
    )(q, k, v)


def _pack_bf16_pairs(a):
    n = a.shape[1] // 2
    bits = pltpu.bitcast(a.astype(BF16).astype(F32), jnp.uint32)
    return (bits[:, :n] >> 16) | (bits[:, n:] & jnp.uint32(0xFFFF0000))


def _unpack_bf16_pairs(w):
    lo = pltpu.bitcast(w << 16, F32)
    hi = pltpu.bitcast(w & jnp.uint32(0xFFFF0000), F32)
    return jnp.concatenate([lo, hi], axis=1)


def _residual_router(y, x, mod, w_out, g2n, rwt, rb, outs, cnt_scr, *, d):
    xo_ref, h2_ref, idx_ref, gate_ref, rank_ref, cnt_ref = outs
    first = jnp.logical_and(pl.program_id(0) == 0, pl.program_id(1) == 0)

    @pl.when(first)
    def _():
        cnt_scr[...] = jnp.zeros(cnt_scr.shape, F32)

    xn = x + mod[:, 2 * d:3 * d] * jnp.dot(y, w_out, preferred_element_type=F32)
    xo_ref[...] = xn
    h2 = _norm_mod(xn, g2n, mod[:, 4 * d:5 * d], mod[:, 3 * d:4 * d])
    h2_ref[...] = _pack_bf16_pairs(h2)
    lg = lax.dot_general(rwt, h2, (((1,), (1,)), ((), ())), precision=HIGHEST,
                         preferred_element_type=F32) + rb
    row = lax.broadcasted_iota(jnp.int32, lg.shape, 0)
    vals, idxs = [], []
    for _ in range(TOP_K):
        m = jnp.max(lg, axis=0, keepdims=True)
        i = jnp.min(jnp.where(lg == m, row, N_EXPERTS), axis=0, keepdims=True)
        vals.append(m)
        idxs.append(i)
        lg = jnp.where(row == i, -jnp.inf, lg)
    es = [jnp.exp(vv - vals[0]) for vv in vals]
    tot = es[0] + es[1] + es[2] + es[3]
    idx_ref[...] = jnp.concatenate(idxs, axis=0)
    gate_ref[...] = jnp.concatenate([e / tot for e in es], axis=0)

    tm = lg.shape[1]
    earlier = (lax.broadcasted_iota(jnp.int32, (tm, tm), 0) < lax.broadcasted_iota(jnp.int32, (tm, tm), 1)).astype(BF16)
    run = cnt_scr[...]
    ranks = []
    for i in idxs:
        hit = row == i
        before = jnp.dot(hit.astype(BF16), earlier, preferred_element_type=F32)
        ranks.append(jnp.sum(jnp.where(hit, run + before, 0.0), axis=0, keepdims=True))
        run = run + jnp.sum(hit.astype(F32), axis=1, keepdims=True)
    cnt_scr[...] = run
    rank_ref[...] = jnp.concatenate(ranks, axis=0).astype(jnp.int32)
    cnt_ref[...] = run.astype(jnp.int32)


def _post_kernel(y_ref, x_ref, mod_ref, w_ref, g_ref, rwt_ref, rb_ref, *rest, d):
    _residual_router(y_ref[...], x_ref[...], mod_ref[...], w_ref[...], g_ref[...], rwt_ref[...], rb_ref[...],
                     rest[:-1], rest[-1], d=d)


def _post_specs(b, t, d, n_lat_tiles):
    nt = t // TM
    row = lambda i, j: (i, j, 0)
    const2 = lambda i, j: (0, 0)
    in_tail = [
        pl.BlockSpec((d, d), const2),
        pl.BlockSpec((1, d), const2),
        pl.BlockSpec((N_EXPERTS, d), const2),
        pl.BlockSpec((N_EXPERTS, 1), const2),
    ]
    per_token = pl.BlockSpec((TOP_K, TM), lambda i, j: (0, i * nt + j))
    out_specs = [
        pl.BlockSpec((None, TM, d), row),
        pl.BlockSpec((None, TM, d // 2), row),
        per_token, per_token, per_token,
        pl.BlockSpec((N_EXPERTS, 1), const2),
    ]
    out_shape = [
        jax.ShapeDtypeStruct((b, t, d), F32),
        jax.ShapeDtypeStruct((b, t, d // 2), jnp.uint32),
        jax.ShapeDtypeStruct((TOP_K, b * t), jnp.int32),
        jax.ShapeDtypeStruct((TOP_K, b * t), F32),
        jax.ShapeDtypeStruct((TOP_K, b * t), jnp.int32),
        jax.ShapeDtypeStruct((N_EXPERTS, 1), jnp.int32),
    ]
    mod_spec = pl.BlockSpec((None, 1, 6 * d), lambda i, j: (jnp.where(j >= n_lat_tiles, b, i), 0, 0))
    return mod_spec, in_tail, out_specs, out_shape


def _post(y, x, mod, w_out, g2n, router_w, router_b, n_lat_tiles):
    b, t, d = x.shape
    mod_spec, in_tail, out_specs, out_shape = _post_specs(b, t, d, n_lat_tiles)
    row = lambda i, j: (i, j, 0)
    return pl.pallas_call(
        functools.partial(_post_kernel, d=d),
        grid=(b, t // TM),
        in_specs=[pl.BlockSpec((None, TM, d), row), pl.BlockSpec((None, TM, d), row), mod_spec] + in_tail,
        out_specs=out_specs,
        out_shape=out_shape,
        scratch_shapes=[pltpu.VMEM((N_EXPERTS, 1), F32)],
        compiler_params=_cparams(("arbitrary", "arbitrary")),
        name="outproj_router",
    )(y, x, mod, w_out, g2n.reshape(1, d), router_w.T, router_b.reshape(N_EXPERTS, 1))


def _gmlp_kernel(x_ref, mod_ref, g1_ref, win_ref, vg_ref, ws_ref, bs_ref, w_ref, g_ref, rwt_ref, rb_ref,
                 *rest, d):
    outs, cnt_scr, y_scr = rest[:-2], rest[-2], rest[-1]
    mod = mod_ref[...]
    x = x_ref[...]
    h = _norm_mod(x, g1_ref[...], mod[:, d:2 * d], mod[:, 0:d]).astype(BF16)
    z = jnp.dot(h, win_ref[...], preferred_element_type=F32)
    z = 0.5 * z * (1.0 + lax.erf(z * (2.0 ** -0.5)))
    width = z.shape[1] // 2
    u, v = z[:, :width], z[:, width:]
    v = (v * lax.rsqrt(jnp.mean(v * v, axis=-1, keepdims=True) + EPS) * vg_ref[...]).astype(BF16)
    gd = width // GMLP_GROUPS
    for n in range(TM // CHUNK):
        r = slice(n * CHUNK, (n + 1) * CHUNK)
        for g in range(GMLP_GROUPS):
            cs = slice(g * gd, (g + 1) * gd)
            mixed = jnp.dot(ws_ref[g], v[r, cs], preferred_element_type=F32) + bs_ref[:, g:g + 1]
            y_scr[r, cs] = (u[r, cs] * mixed).astype(BF16)
    _residual_router(y_scr[...], x, mod, w_ref[...], g_ref[...], rwt_ref[...], rb_ref[...], outs, cnt_scr, d=d)


def _gmlp_layer(x, mod, g1n, w_in, v_gain, w_s, b_s, w_out, g2n, router_w, router_b, n_lat_tiles):
    b, t, d = x.shape
    width = w_in.shape[1] // 2
    mod_spec, in_tail, out_specs, out_shape = _post_specs(b, t, d, n_lat_tiles)
    row = lambda i, j: (i, j, 0)
    const2 = lambda i, j: (0, 0)
    return pl.pallas_call(
        functools.partial(_gmlp_kernel, d=d),
        grid=(b, t // TM),
        in_specs=[
            pl.BlockSpec((None, TM, d), row),
            mod_spec,
            pl.BlockSpec((1, d), const2),
            pl.BlockSpec((d, 2 * width), const2),
            pl.BlockSpec((1, width), const2),
            pl.BlockSpec((GMLP_GROUPS, CHUNK, CHUNK), lambda i, j: (0, 0, 0)),
            pl.BlockSpec((CHUNK, GMLP_GROUPS), const2),
        ] + in_tail,
        out_specs=out_specs,
        out_shape=out_shape,
        scratch_shapes=[pltpu.VMEM((N_EXPERTS, 1), F32), pltpu.VMEM((TM, width), BF16)],
        compiler_params=_cparams(("arbitrary", "arbitrary")),
        name="gmlp_layer",
    )(x, mod, g1n.reshape(1, d), w_in, v_gain.reshape(1, width), w_s, b_s.T, w_out, g2n.reshape(1, d),
      router_w.T, router_b.reshape(N_EXPERTS, 1))


def _filter_mlp_kernel(z_ref, t_ref, w1, b1, w2, b2, w3, b3, w4, b4, fr, dl, kf_ref, kb_ref, *, width, tl):
    dot = functools.partial(jnp.dot, precision=HIGHEST, preferred_element_type=F32)
    f = fr[...]
    a = jnp.sin(f * (dot(z_ref[...], w1[...]) + b1[...]))
    a = jnp.sin(f * (dot(a, w2[...]) + b2[...]))
    a = jnp.sin(f * (dot(a, w3[...]) + b3[...]))
    k = dot(a, w4[...]) + b4[...]
    window = jnp.exp(-t_ref[...] * dl[...])
    kf_ref[...] = k[:, :width] * window
    pos = pl.program_id(0) * tl + lax.broadcasted_iota(jnp.int32, (tl, 1), 0)
    kb_ref[...] = jnp.where(pos == 0, 0.0, k[:, width:] * window)


def _hyena_filters(length, f_w1, f_b1, f_w2, f_b2, f_w3, f_b3, f_w4, f_b4, freq):
    width = f_w4.shape[1] // 2
    hid = f_w1.shape[1]
    emb = 2 * N_BANDS + 1
    t = jnp.linspace(0.0, 1.0, length, dtype=F32)[:, None]
    w = 2.0 * math.pi * jnp.arange(length, dtype=F32)[:, None] / length
    f = jnp.linspace(1e-4, N_BANDS - 1, N_BANDS, dtype=F32)[None, :]
    z = jnp.concatenate([t, jnp.cos(f * w), -jnp.sin(f * w), jnp.zeros((length, hid - emb), F32)], axis=-1)
    w1p = jnp.concatenate([f_w1, jnp.zeros((hid - emb, hid), F32)], axis=0)
    deltas = jnp.abs(jnp.linspace(MIN_DECAY, MAX_DECAY, width, dtype=F32))[None, :]
    tl = min(length, 512)
    full = lambda shape: pl.BlockSpec(shape, lambda i: (0, 0))
    return pl.pallas_call(
        functools.partial(_filter_mlp_kernel, width=width, tl=tl),
        grid=(length // tl,),
        in_specs=[
            pl.BlockSpec((tl, hid), lambda i: (i, 0)),
            pl.BlockSpec((tl, 1), lambda i: (i, 0)),
            full((hid, hid)), full((1, hid)), full((hid, hid)), full((1, hid)), full((hid, hid)), full((1, hid)),
            full((hid, 2 * width)), full((1, 2 * width)), full((1, hid)), full((1, width)),
        ],
        out_specs=[pl.BlockSpec((tl, width), lambda i: (i, 0)), pl.BlockSpec((tl, width), lambda i: (i, 0))],
        out_shape=[jax.ShapeDtypeStruct((length, width), F32), jax.ShapeDtypeStruct((length, width), F32)],
        compiler_params=_cparams(("parallel",)),
        name="hyena_filter_mlp",
    )(z, t, w1p, f_b1.reshape(1, hid), f_w2, f_b2.reshape(1, hid), f_w3, f_b3.reshape(1, hid),
      f_w4, f_b4.reshape(1, 2 * width), freq.reshape(1, hid), deltas)


def _fft_plan(length):
    n = 2 * length
    n1 = n // FFT_N2
    nz = n1 // 2
    ku = -(-(nz + 1) // 8) * 8
    return n, n1, nz, ku


def _fft_tables(length):
    n, n1, nz, ku = _fft_plan(length)
    i2 = np.arange(FFT_N2)[:, None, None]
    k1 = np.arange(ku)[None, :, None]
    i1 = np.arange(nz)[None, None, :]
    phi = 2.0 * np.pi * (((FFT_N2 * i1 + i2) * k1) % n) / n
    fwd = np.concatenate([np.cos(phi), -np.sin(phi)], axis=1)
    wgt = np.where((k1 == 0) | (k1 == nz), 1.0, np.where(k1 < nz, 2.0, 0.0)) / n
    inv = np.concatenate([np.cos(phi) * wgt, -np.sin(phi) * wgt], axis=1).transpose(0, 2, 1)
    th = 2.0 * np.pi * ((np.arange(FFT_N2)[:, None] * np.arange(FFT_N2)[None, :]) % FFT_N2) / FFT_N2
    c, s = np.cos(th), np.sin(th)
    f2 = np.block([[c, s], [-s, c]])
    f2i = np.block([[c, -s], [s, c]])
    as_bf = lambda a: jnp.asarray(a, F32).astype(BF16)
    return as_bf(fwd), as_bf(inv), as_bf(f2), as_bf(f2i)


def _fft_stage1(src_ref, fwd_ref, a_ref, *, nz, ku):
    slab = 2 * FFT_N2

    def body(i2, carry):
        rows = src_ref[pl.ds(i2, nz, stride=FFT_N2), :].astype(BF16)
        r = jnp.dot(fwd_ref[i2], rows, preferred_element_type=F32)
        a_ref[pl.ds(i2, ku, stride=slab), :] = r[:ku]
        a_ref[pl.ds(FFT_N2 + i2, ku, stride=slab), :] = r[ku:]
        return carry

    lax.fori_loop(0, FFT_N2, body, 0, unroll=FFT_UNROLL)


def _short_conv(p_ref, w_ref, b_ref, pad_ref, emit, *, length):
    step = min(length, 512)
    cb = p_ref.shape[-1]
    pad_ref[pl.ds(0, 8), :] = jnp.zeros((8, cb), F32)
    pad_ref[pl.ds(length + 8, 8), :] = jnp.zeros((8, cb), F32)
    for j in range(length // step):
        pad_ref[pl.ds(8 + j * step, step), :] = p_ref[pl.ds(j * step, step), :].astype(F32)
    w = w_ref[...]
    for j in range(length // step):
        r0 = j * step
        val = (pad_ref[pl.ds(r0 + 7, step), :] * w[0:1] + pad_ref[pl.ds(r0 + 8, step), :] * w[1:2]
               + pad_ref[pl.ds(r0 + 9, step), :] * w[2:3] + b_ref[...])
        emit(r0, step, val)


def _kf_kernel(kf_ref, kb_ref, fwd_ref, f2_ref, o_ref, a_ref, *, nz, ku):
    slab = 2 * FFT_N2
    for src, sign in ((kf_ref, 1.0), (kb_ref, -1.0)):
        _fft_stage1(src, fwd_ref, a_ref, nz=nz, ku=ku)

        def body(k1, carry, sign=sign, first=(src is kf_ref)):
            a = a_ref[pl.ds(pl.multiple_of(k1 * slab, slab), slab), :].astype(BF16)
            xk = jnp.dot(f2_ref[...], a, preferred_element_type=F32)
            if first:
                o_ref[k1] = xk
            else:
                o_ref[k1, :FFT_N2, :] = o_ref[k1, :FFT_N2, :] + xk[:FFT_N2]
                o_ref[k1, FFT_N2:, :] = o_ref[k1, FFT_N2:, :] - xk[FFT_N2:]
            return carry

        lax.fori_loop(0, ku, body, 0, unroll=math.gcd(ku, FFT_UNROLL))


def _filter_spectrum(k_fwd, k_bwd, tables):
    length, width = k_fwd.shape
    _, _, nz, ku = _fft_plan(length)
    fwd, _, f2, _ = tables
    blk = pl.BlockSpec((length, FFT_CB), lambda c: (0, c))
    return pl.pallas_call(
        functools.partial(_kf_kernel, nz=nz, ku=ku),
        grid=(width // FFT_CB,),
        in_specs=[
            blk, blk,
            pl.BlockSpec(fwd.shape, lambda c: (0, 0, 0)),
            pl.BlockSpec(f2.shape, lambda c: (0, 0)),
        ],
        out_specs=pl.BlockSpec((ku, 2 * FFT_N2, FFT_CB), lambda c: (0, 0, c)),
        out_shape=jax.ShapeDtypeStruct((ku, 2 * FFT_N2, width), F32),
        scratch_shapes=[pltpu.VMEM((ku * 2 * FFT_N2, FFT_CB), F32)],
        compiler_params=_cparams(("parallel",)),
        name="hyena_filter_spectrum",
    )(k_fwd, k_bwd, fwd, f2)


def _hyena_conv_kernel(px0_ref, px1_ref, pv_ref, w0_ref, w1_ref, wv_ref, b0_ref, b1_ref, bv_ref, dsk_ref,
                       kf_ref, fwd_ref, inv_ref, f2_ref, f2i_ref, o_ref, a_ref, vv_ref, *, length, nz, ku):
    slab = 2 * FFT_N2

    def set_vv(r0, rows, val):
        vv_ref[pl.ds(r0, rows), :] = val

    def mul_vv(r0, rows, val):
        vv_ref[pl.ds(r0, rows), :] = vv_ref[pl.ds(r0, rows), :] * val

    _short_conv(pv_ref, wv_ref, bv_ref, a_ref, set_vv, length=length)
    _short_conv(px1_ref, w1_ref, b1_ref, a_ref, mul_vv, length=length)
    _fft_stage1(vv_ref, fwd_ref, a_ref, nz=nz, ku=ku)

    def freq_body(k1, carry):
        rows = pl.ds(k1 * slab if isinstance(k1, int) else pl.multiple_of(k1 * slab, slab), slab)
        xk = jnp.dot(f2_ref[...], a_ref[rows, :].astype(BF16), preferred_element_type=F32)
        kf = kf_ref[k1]
        xr, xi = xk[:FFT_N2], xk[FFT_N2:]
        kr, ki = kf[:FFT_N2], kf[FFT_N2:]
        prod = jnp.concatenate([xr * kr - xi * ki, xr * ki + xi * kr], axis=0).astype(BF16)
        a_ref[rows, :] = jnp.dot(f2i_ref[...], prod, preferred_element_type=F32)
        return carry

    lax.fori_loop(0, nz, freq_body, 0, unroll=math.gcd(nz, FFT_UNROLL))
    freq_body(nz, 0)

    def time_body(i2, carry):
        re = a_ref[pl.ds(i2, ku, stride=slab), :]
        im = a_ref[pl.ds(FFT_N2 + i2, ku, stride=slab), :]
        q = jnp.concatenate([re, im], axis=0).astype(BF16)
        y = jnp.dot(inv_ref[i2], q, preferred_element_type=F32)
        rows = pl.ds(i2, nz, stride=FFT_N2)
        vv_ref[rows, :] = y + vv_ref[rows, :] * dsk_ref[...]
        return carry

    lax.fori_loop(0, FFT_N2, time_body, 0, unroll=FFT_UNROLL)

    def emit_out(r0, rows, val):
        o_ref[pl.ds(r0, rows), :] = (vv_ref[pl.ds(r0, rows), :] * val).astype(o_ref.dtype)

    _short_conv(px0_ref, w0_ref, b0_ref, a_ref, emit_out, length=length)


def _hyena_conv(p, conv_w, conv_b, d_skip, kf, tables, length):
    b, _, w3 = p.shape
    width = w3 // 3
    ncb = width // FFT_CB
    _, _, nz, ku = _fft_plan(length)
    fwd, inv, f2, f2i = tables
    pblk = lambda part: _resident((None, length, FFT_CB), lambda c, i: (i, 0, part * ncb + c))
    wblk = lambda part: pl.BlockSpec((3, FFT_CB), lambda c, i: (0, part * ncb + c))
    bblk = lambda part: pl.BlockSpec((1, FFT_CB), lambda c, i: (0, part * ncb + c))
    return pl.pallas_call(
        functools.partial(_hyena_conv_kernel, length=length, nz=nz, ku=ku),
        grid=(ncb, b),
        in_specs=[
            pblk(0), pblk(1), pblk(2), wblk(0), wblk(1), wblk(2), bblk(0), bblk(1), bblk(2),
            pl.BlockSpec((1, FFT_CB), lambda c, i: (0, c)),
            _resident((ku, 2 * FFT_N2, FFT_CB), lambda c, i: (0, 0, c)),
            _resident(fwd.shape, lambda c, i: (0, 0, 0)),
            _resident(inv.shape, lambda c, i: (0, 0, 0)),
            _resident(f2.shape, lambda c, i: (0, 0)),
            _resident(f2i.shape, lambda c, i: (0, 0)),
        ],
        out_specs=pl.BlockSpec((None, length, FFT_CB), lambda c, i: (i, 0, c)),
        out_shape=jax.ShapeDtypeStruct((b, length, width), BF16),
        scratch_shapes=[
            pltpu.VMEM((max(ku * 2 * FFT_N2, length + 16), FFT_CB), F32),
            pltpu.VMEM((length, FFT_CB), F32),
        ],
        compiler_params=_cparams(("arbitrary", "arbitrary")),
        name="hyena_long_conv",
    )(p, p, p, conv_w, conv_w, conv_w, conv_b.reshape(1, w3), conv_b.reshape(1, w3), conv_b.reshape(1, w3),
      d_skip.reshape(1, width), kf, fwd, inv, f2, f2i)


def _dense_dft_tables(length):
    n = 2 * length
    th = 2.0 * np.pi * ((np.arange(n)[:, None] * np.arange(length)[None, :]) % n) / n
    fwd = np.concatenate([np.cos(th), -np.sin(th)], axis=0)
    inv = np.concatenate([np.cos(th), -np.sin(th)], axis=0).T / n
    return jnp.asarray(fwd, F32).astype(BF16), jnp.asarray(inv, F32).astype(BF16)


def _kf_dense_kernel(kf_ref, kb_ref, fwd_ref, o_ref, *, n):
    xf = jnp.dot(fwd_ref[...], kf_ref[...].astype(BF16), preferred_element_type=F32)
    xb = jnp.dot(fwd_ref[...], kb_ref[...].astype(BF16), preferred_element_type=F32)
    o_ref[:n, :] = xf[:n] + xb[:n]
    o_ref[n:, :] = xf[n:] - xb[n:]


def _filter_spectrum_dense(k_fwd, k_bwd, fwd):
    length, width = k_fwd.shape
    n = 2 * length
    blk = pl.BlockSpec((length, FFT_CB), lambda c: (0, c))
    return pl.pallas_call(
        functools.partial(_kf_dense_kernel, n=n),
        grid=(width // FFT_CB,),
        in_specs=[blk, blk, pl.BlockSpec(fwd.shape, lambda c: (0, 0))],
        out_specs=pl.BlockSpec((2 * n, FFT_CB), lambda c: (0, c)),
        out_shape=jax.ShapeDtypeStruct((2 * n, width), F32),
        compiler_params=_cparams(("parallel",)),
        name="hyena_ctx_filter_spectrum",
    )(k_fwd, k_bwd, fwd)


def _hyena_ctx_kernel(px0_ref, px1_ref, pv_ref, w0_ref, w1_ref, wv_ref, b0_ref, b1_ref, bv_ref, dsk_ref,
                      kf_ref, fwd_ref, inv_ref, o_ref, pad_ref, vv_ref, *, length):
    n = 2 * length

    def set_vv(r0, rows, val):
        vv_ref[pl.ds(r0, rows), :] = val

    def mul_vv(r0, rows, val):
        vv_ref[pl.ds(r0, rows), :] = vv_ref[pl.ds(r0, rows), :] * val

    _short_conv(pv_ref, wv_ref, bv_ref, pad_ref, set_vv, length=length)
    _short_conv(px1_ref, w1_ref, b1_ref, pad_ref, mul_vv, length=length)
    vv = vv_ref[...]
    xk = jnp.dot(fwd_ref[...], vv.astype(BF16), preferred_element_type=F32)
    kf = kf_ref[...]
    xr, xi, kr, ki = xk[:n], xk[n:], kf[:n], kf[n:]
    prod = jnp.concatenate([xr * kr - xi * ki, xr * ki + xi * kr], axis=0).astype(BF16)
    vv_ref[...] = jnp.dot(inv_ref[...], prod, preferred_element_type=F32) + vv * dsk_ref[...]

    def emit_out(r0, rows, val):
        o_ref[pl.ds(r0, rows), :] = (vv_ref[pl.ds(r0, rows), :] * val).astype(o_ref.dtype)

    _short_conv(px0_ref, w0_ref, b0_ref, pad_ref, emit_out, length=length)


def _hyena_ctx_conv(p, conv_w, conv_b, d_skip, kf, fwd, inv, n_lat, length):
    b, _, w3 = p.shape
    width = w3 // 3
    ncb = width // FFT_CB
    rb = n_lat // length
    pblk = lambda part: pl.BlockSpec((None, length, FFT_CB), lambda c, i: (i, rb, part * ncb + c))
    wblk = lambda part: pl.BlockSpec((3, FFT_CB), lambda c, i: (0, part * ncb + c))
    bblk = lambda part: pl.BlockSpec((1, FFT_CB), lambda c, i: (0, part * ncb + c))
    return pl.pallas_call(
        functools.partial(_hyena_ctx_kernel, length=length),
        grid=(ncb, b),
        in_specs=[
            pblk(0), pblk(1), pblk(2), wblk(0), wblk(1), wblk(2), bblk(0), bblk(1), bblk(2),
            pl.BlockSpec((1, FFT_CB), lambda c, i: (0, c)),
            pl.BlockSpec((4 * length, FFT_CB), lambda c, i: (0, c)),
            pl.BlockSpec(fwd.shape, lambda c, i: (0, 0)),
            pl.BlockSpec(inv.shape, lambda c, i: (0, 0)),
        ],
        out_specs=pl.BlockSpec((None, length, FFT_CB), lambda c, i: (i, 0, c)),
        out_shape=jax.ShapeDtypeStruct((b, length, width), BF16),
        scratch_shapes=[pltpu.VMEM((length + 16, FFT_CB), F32), pltpu.VMEM((length, FFT_CB), F32)],
        compiler_params=_cparams(("arbitrary", "arbitrary")),
        name="hyena_ctx_conv",
    )(p, p, p, conv_w, conv_w, conv_w, conv_b.reshape(1, w3), conv_b.reshape(1, w3), conv_b.reshape(1, w3),
      d_skip.reshape(1, width), kf, fwd, inv)


def _row_copy(src, dst, sem):
    return pltpu.make_async_copy(src, dst, sem)


def _scatter_kernel(cnt_ref, start_ref, dest_ref, h_ref, x_hbm, src0, src1, src2, src3, zero_scr, sem, zsem, *, n_tok):
    rows, half = h_ref.shape
    src_scr = (src0, src1, src2, src3)
    n_real = TOP_K * n_tok
    row_iota = lax.broadcasted_iota(jnp.int32, (rows, LANES), 0)

    def ids_from(base):
        return pltpu.bitcast(base + row_iota, jnp.uint32)

    @pl.when(pl.program_id(0) == 0)
    def _():
        zero_scr[:, :half] = jnp.zeros((rows, half), jnp.uint32)

        def per_expert(e, total):
            cnt = cnt_ref[e]
            n_pad = (EXPERT_BLOCK - cnt % EXPERT_BLOCK) % EXPERT_BLOCK
            base = start_ref[e] + cnt
            zero_scr[:, half:] = ids_from(n_real + total)

            def one(r, carry):
                _row_copy(zero_scr.at[pl.ds(r, 1)], x_hbm.at[pl.ds(base + r, 1)], zsem).start()
                return carry

            def wait_one(r, carry):
                _row_copy(zero_scr.at[pl.ds(0, 1)], x_hbm.at[pl.ds(0, 1)], zsem).wait()
                return carry

            lax.fori_loop(0, n_pad, one, 0)
            lax.fori_loop(0, n_pad, wait_one, 0)
            return total + n_pad

        total = lax.fori_loop(0, N_EXPERTS, per_expert, 0)
        first_free = start_ref[N_EXPERTS] // EXPERT_BLOCK
        n_blocks = x_hbm.shape[0] // EXPERT_BLOCK

        def tail(i, carry):
            zero_scr[:, half:] = ids_from(n_real + total + (i - first_free) * EXPERT_BLOCK)
            cp = _row_copy(zero_scr, x_hbm.at[pl.ds(pl.multiple_of(i * EXPERT_BLOCK, EXPERT_BLOCK), EXPERT_BLOCK)],
                           zsem)
            cp.start()
            cp.wait()
            return carry

        lax.fori_loop(first_free, n_blocks, tail, 0)

    h = h_ref[...]
    for k in range(TOP_K):
        src_scr[k][:, :half] = h
        src_scr[k][:, half:] = ids_from(k * n_tok + pl.program_id(0) * rows)

    def issue(r, carry):
        for k in range(TOP_K):
            _row_copy(src_scr[k].at[pl.ds(r, 1)], x_hbm.at[pl.ds(dest_ref[0, r * TOP_K + k], 1)], sem).start()
        return carry

    lax.fori_loop(0, rows, issue, 0, unroll=4)
    for k in range(TOP_K):
        _row_copy(src_scr[k], x_hbm.at[pl.ds(0, rows)], sem).wait()


def _dispatch(h2p, dest_tiles, counts, pad_start, n_blocks):
    n_tok, half = h2p.shape
    assert TM == EXPERT_BLOCK
    width = half + LANES
    return pl.pallas_call(
        functools.partial(_scatter_kernel, n_tok=n_tok),
        grid_spec=pltpu.PrefetchScalarGridSpec(
            num_scalar_prefetch=2,
            grid=(n_tok // TM,),
            in_specs=[
                pl.BlockSpec((None, 1, TOP_K * TM), lambda i, c, s: (i, 0, 0), memory_space=pltpu.SMEM),
                pl.BlockSpec((TM, half), lambda i, c, s: (i, 0)),
            ],
            out_specs=pl.BlockSpec(memory_space=pl.ANY),
            scratch_shapes=[pltpu.VMEM((TM, width), jnp.uint32)] * TOP_K + [
                pltpu.VMEM((EXPERT_BLOCK, width), jnp.uint32), pltpu.SemaphoreType.DMA(()), pltpu.SemaphoreType.DMA(())],
        ),
        out_shape=jax.ShapeDtypeStruct((n_blocks * EXPERT_BLOCK, width), jnp.uint32),
        compiler_params=_cparams(("arbitrary",)),
        name="moe_dispatch",
    )(counts, pad_start, dest_tiles, h2p)


def _ffn_kernel(be_ref, ids_prev_ref, ids_last_ref, x_ref, w1_ref, b1_ref, w2_ref, b2_ref, y_hbm,
                buf0, buf1, w1_bf, w2_bf, sem0, sem1, *, d_ff, half):
    i = pl.program_id(0)
    n = pl.num_programs(0)
    rows = x_ref.shape[0]
    spare = y_hbm.shape[0] - 2 * rows

    def drain(src, sem, ids_ref):
        for r in range(rows):
            _row_copy(src.at[pl.ds(r, 1)], y_hbm.at[pl.ds(ids_ref[0, r], 1)], sem).start()

    def wait(src, sem):
        _row_copy(src, y_hbm.at[pl.ds(0, rows)], sem).wait()

    @pl.when(i == 0)
    def _():
        buf0[...] = jnp.zeros(buf0.shape, buf0.dtype)
        buf1[...] = jnp.zeros(buf1.shape, buf1.dtype)
        _row_copy(buf0, y_hbm.at[pl.ds(spare + rows, rows)], sem0).start()

    @pl.when(jnp.logical_or(i == 0, be_ref[i] != be_ref[jnp.maximum(i - 1, 0)]))
    def _():
        w1_bf[...] = w1_ref[...].astype(BF16)
        w2_bf[...] = w2_ref[...].astype(BF16)

    def step(src, ssem, dst, dsem):
        wait(dst, dsem)
        drain(src, ssem, ids_prev_ref)
        x = _unpack_bf16_pairs(x_ref[:, :half]).astype(BF16)
        hh = jnp.dot(x, w1_bf[...], preferred_element_type=F32) + b1_ref[...]
        glu = jnp.minimum(hh[:, :d_ff], SWIGLU_LIMIT)
        lin = jnp.clip(hh[:, d_ff:], -SWIGLU_LIMIT, SWIGLU_LIMIT)
        act = (glu * jax.nn.sigmoid(SWIGLU_ALPHA * glu) * (lin + 1.0)).astype(BF16)
        dst[...] = _pack_bf16_pairs(jnp.dot(act, w2_bf[...], preferred_element_type=F32) + b2_ref[...])

    @pl.when(i % 2 == 0)
    def _():
        step(buf1, sem1, buf0, sem0)

    @pl.when(i % 2 == 1)
    def _():
        step(buf0, sem0, buf1, sem1)

    @pl.when(i == n - 1)
    def _():
        last_even = (y_hbm.shape[0] // rows - 3) % 2 == 0
        last, lsem, prev, psem = (buf0, sem0, buf1, sem1) if last_even else (buf1, sem1, buf0, sem0)
        drain(last, lsem, ids_last_ref)
        wait(prev, psem)
        wait(last, lsem)


def _expert_ffn(x_disp, ids, block_e, layer, w1, b1, w2, b2):
    rows, width = x_disp.shape
    half = width - LANES
    depth, n_e, d, ff2 = w1.shape
    d_ff = ff2 // 2
    n_blocks = rows // EXPERT_BLOCK
    ids_prev = jnp.concatenate([rows + jnp.arange(EXPERT_BLOCK, dtype=jnp.int32), ids])
    ids_prev = ids_prev.reshape(n_blocks + 1, 1, EXPERT_BLOCK)
    return pl.pallas_call(
        functools.partial(_ffn_kernel, d_ff=d_ff, half=half),
        grid_spec=pltpu.PrefetchScalarGridSpec(
            num_scalar_prefetch=1,
            grid=(n_blocks,),
            in_specs=[
                pl.BlockSpec((None, 1, EXPERT_BLOCK), lambda i, be: (i, 0, 0), memory_space=pltpu.SMEM),
                pl.BlockSpec((None, 1, EXPERT_BLOCK), lambda i, be: (n_blocks, 0, 0), memory_space=pltpu.SMEM),
                pl.BlockSpec((EXPERT_BLOCK, width), lambda i, be: (i, 0)),
                pl.BlockSpec((None, None, d, ff2), lambda i, be: (layer, be[i], 0, 0)),
                pl.BlockSpec((None, None, 1, ff2), lambda i, be: (layer, be[i], 0, 0)),
                pl.BlockSpec((None, None, d_ff, d), lambda i, be: (layer, be[i], 0, 0)),
                pl.BlockSpec((None, None, 1, d), lambda i, be: (layer, be[i], 0, 0)),
            ],
            out_specs=pl.BlockSpec(memory_space=pl.ANY),
            scratch_shapes=[pltpu.VMEM((EXPERT_BLOCK, half), jnp.uint32), pltpu.VMEM((EXPERT_BLOCK, half), jnp.uint32),
                            pltpu.VMEM((d, ff2), BF16), pltpu.VMEM((d_ff, d), BF16),
                            pltpu.SemaphoreType.DMA(()), pltpu.SemaphoreType.DMA(())],
        ),
        out_shape=jax.ShapeDtypeStruct((rows + 2 * EXPERT_BLOCK, half), jnp.uint32),
        compiler_params=_cparams(("arbitrary",)),
        name="moe_expert_ffn",
    )(block_e, ids_prev, ids_prev, x_disp, w1, b1.reshape(depth, n_e, 1, ff2), w2, b2.reshape(depth, n_e, 1, d))


def _combine_kernel(y0_ref, y1_ref, y2_ref, y3_ref, gate_ref, x_ref, mod_ref, fg_ref, o_ref, *, d, final):
    gate = gate_ref[...]
    out = _unpack_bf16_pairs(y0_ref[...]) * gate[:, 0:1]
    for k, y_ref in enumerate((y1_ref, y2_ref, y3_ref), start=1):
        out = out + _unpack_bf16_pairs(y_ref[...]) * gate[:, k:k + 1]
    xn = x_ref[...] + mod_ref[:, 5 * d:6 * d] * out
    if final:
        xn = xn * lax.rsqrt(jnp.mean(xn * xn, axis=-1, keepdims=True) + EPS) * fg_ref[...]
    o_ref[...] = xn


def _combine(y_rows, gates, x, mod, final_g, n_lat_tiles, final):
    b, t, d = x.shape
    nt = t // TM
    n_tiles = b * nt
    nt_out = n_lat_tiles if final else nt
    y_spec = lambda k: pl.BlockSpec((TM, d // 2), lambda i, j: (k * n_tiles + i * nt + j, 0))
    return pl.pallas_call(
        functools.partial(_combine_kernel, d=d, final=final),
        grid=(b, nt_out),
        in_specs=[
            y_spec(0), y_spec(1), y_spec(2), y_spec(3),
            pl.BlockSpec((TM, TOP_K), lambda i, j: (i * nt + j, 0)),
            pl.BlockSpec((None, TM, d), lambda i, j: (i, j, 0)),
            pl.BlockSpec((None, 1, 6 * d), lambda i, j: (jnp.where(j >= n_lat_tiles, b, i), 0, 0)),
            pl.BlockSpec((1, d), lambda i, j: (0, 0)),
        ],
        out_specs=pl.BlockSpec((None, TM, d), lambda i, j: (i, j, 0)),
        out_shape=jax.ShapeDtypeStruct((b, nt_out * TM, d), F32),
        compiler_params=_cparams(("parallel", "parallel")),
        name="moe_combine",
    )(y_rows, y_rows, y_rows, y_rows, gates, x, mod, final_g.reshape(1, d))


def _moe(h2p, idx_t, gates_t, rank_t, cnt, x, mod, layer, w1, b1, w2, b2, final_g, n_lat_tiles, final):
    b, t, d = x.shape
    n_tok = b * t
    n_assign = n_tok * TOP_K
    counts = cnt[:, 0]
    padded = (counts + EXPERT_BLOCK - 1) // EXPERT_BLOCK * EXPERT_BLOCK
    pad_end = jnp.cumsum(padded).astype(jnp.int32)
    pad_start = pad_end - padded
    experts = jnp.arange(N_EXPERTS, dtype=jnp.int32)
    dest_t = rank_t + jnp.sum(jnp.where(idx_t[..., None] == experts, pad_start, 0), axis=-1)
    n_blocks = -(-n_assign // EXPERT_BLOCK) + N_EXPERTS
    block_start = jnp.arange(n_blocks, dtype=jnp.int32) * EXPERT_BLOCK
    block_e = jnp.minimum(jnp.sum(pad_end[None, :] <= block_start[:, None], axis=1), N_EXPERTS - 1).astype(jnp.int32)
    dest_tiles = dest_t.reshape(TOP_K, n_tok // TM, TM).transpose(1, 2, 0).reshape(n_tok // TM, 1, TOP_K * TM)

    x_disp = _dispatch(h2p.reshape(n_tok, d // 2), dest_tiles, counts,
                       jnp.concatenate([pad_start, pad_end[-1:]]), n_blocks)
    ids = lax.bitcast_convert_type(x_disp[:, d // 2], jnp.int32)
    y_rows = _expert_ffn(x_disp, ids, block_e, layer, w1, b1, w2, b2)
    return _combine(y_rows, gates_t.T, x, mod, final_g, n_lat_tiles, final)


def _rope_tables(n_lat, n_ctx):
    rows = n_lat // GRID_W
    row = jnp.repeat(jnp.arange(rows, dtype=F32), GRID_W)
    col = jnp.tile(jnp.arange(GRID_W, dtype=F32), rows)
    axis_dim = HEAD_DIM // 2
    inv = ROPE_THETA ** (-jnp.arange(0, axis_dim, 2, dtype=F32) / axis_dim)
    ang_r = row[:, None] * inv
    ang_c = col[:, None] * inv
    ang = jnp.concatenate([ang_r, ang_r, ang_c, ang_c], axis=-1)
    cos = jnp.concatenate([jnp.cos(ang), jnp.ones((n_ctx, HEAD_DIM), F32)], axis=0)
    sin = jnp.concatenate([jnp.sin(ang), jnp.zeros((n_ctx, HEAD_DIM), F32)], axis=0)
    even = (jnp.arange(HEAD_DIM) // 32) % 2 == 0
    return cos, jnp.where(even, -sin, 0.0), jnp.where(even, 0.0, sin)


def kernel(x, c, ctx, c_ctx, ada_w, ada_b, norm1_g, norm2_g, mix_w_out, router_w, router_b, exp_w1, exp_b1, exp_w2, exp_b2, attn_w_in, attn_q_gain, attn_k_gain, gmlp_w_in, gmlp_v_gain, gmlp_w_s, gmlp_b_s, hyena_w_in, hyena_conv_w, hyena_conv_b, hyena_f_w1, hyena_f_b1, hyena_f_w2, hyena_f_b2, hyena_f_w3, hyena_f_b3, hyena_f_w4, hyena_f_b4, hyena_freq, hyena_d, final_g):
    b, s, d = x.shape
    n_ctx = ctx.shape[1]
    depth = ada_w.shape[0]
    assert s % TM == 0 and n_ctx == TM and b < MOD_ROWS and s % (FFT_N2 * 8) == 0
    n_lat_tiles = s // TM

    xs = jnp.concatenate([x, ctx], axis=1)
    cc = jnp.concatenate([c, c_ctx[None, :], jnp.zeros((MOD_ROWS - b - 1, d), F32)], axis=0)
    mods = _modulation(cc, ada_w, ada_b).reshape(depth, MOD_ROWS, 1, 6 * d)
    rope = _rope_tables(s, n_ctx)

    for i in range(depth):
        kind, j = i % 3, i // 3
        last = i == depth - 1
        mod = mods[i]
        w_out = mix_w_out[i].astype(BF16)
        if kind == 0:
            q, k, v = _attn_proj(xs, mod, norm1_g[i], attn_w_in[j].astype(BF16), attn_q_gain[j], attn_k_gain[j],
                                 rope, n_lat_tiles)
            y = _attention(q, k, v, s)
        elif kind == 2:
            p = _inproj(xs, mod, norm1_g[i], hyena_w_in[j].astype(BF16), n_lat_tiles)
            fargs = (hyena_f_w1[j], hyena_f_b1[j], hyena_f_w2[j], hyena_f_b2[j], hyena_f_w3[j], hyena_f_b3[j],
                     hyena_f_w4[j], hyena_f_b4[j], hyena_freq[j])
            tables = _fft_tables(s)
            kf = _filter_spectrum(*_hyena_filters(s, *fargs), tables)
            y_lat = _hyena_conv(p, hyena_conv_w[j], hyena_conv_b[j], hyena_d[j], kf, tables, s)
            dfwd, dinv = _dense_dft_tables(n_ctx)
            kf_c = _filter_spectrum_dense(*_hyena_filters(n_ctx, *fargs), dfwd)
            y_ctx = _hyena_ctx_conv(p, hyena_conv_w[j], hyena_conv_b[j], hyena_d[j], kf_c, dfwd, dinv, s, n_ctx)
            y = jnp.concatenate([y_lat, y_ctx], axis=1)
        if kind == 1:
            xs, *routed = _gmlp_layer(
                xs, mod, norm1_g[i], gmlp_w_in[j].astype(BF16), gmlp_v_gain[j], gmlp_w_s[j].astype(BF16),
                gmlp_b_s[j], w_out, norm2_g[i], router_w[i], router_b[i], n_lat_tiles)
        else:
            xs, *routed = _post(y, xs, mod, w_out, norm2_g[i], router_w[i], router_b[i], n_lat_tiles)
        xs = _moe(*routed, xs, mod, i, exp_w1, exp_b1, exp_w2, exp_b2, final_g, n_lat_tiles, last)
    return xs
```

```python
import functools
import math

import jax
import jax.numpy as jnp
import numpy as np
from jax import lax
from jax.experimental import pallas as pl
from jax.experimental.pallas import tpu as pltpu

F32 = jnp.float32
BF16 = jnp.bfloat16
HIGHEST = lax.Precision.HIGHEST

EPS = 1e-6
GRID_W = 64
ROPE_THETA = 10000.0
HEAD_DIM = 128
N_KV_HEADS = 2
Q_GROUP = 4
CHUNK = 128
GMLP_GROUPS = 8
N_EXPERTS = 32
TOP_K = 4
SWIGLU_ALPHA = 1.702
SWIGLU_LIMIT = 7.0
N_BANDS = 16
DECAY_TARGET = 1e-2
MAX_DECAY = math.log(DECAY_TARGET) / 0.3
MIN_DECAY = math.log(DECAY_TARGET) / 1.5

LANES = 128
TM = 256
ATTN_TK = 512
MOD_ROWS = 16
EXPERT_BLOCK = 256
FFT_N2 = 128
FFT_CB = 128
FFT_UNROLL = 16
VMEM_LIMIT = 56 * 1024 * 1024


def _cparams(sem):
    return pltpu.CompilerParams(dimension_semantics=sem, vmem_limit_bytes=VMEM_LIMIT)


def _norm_mod(x, g, scale, shift):
    y = x * lax.rsqrt(jnp.mean(x * x, axis=-1, keepdims=True) + EPS)
    return y * g * (1.0 + scale) + shift


def _resident(shape, index_map):
    return pl.BlockSpec(shape, index_map, pipeline_mode=pl.Buffered(1))


def _mod_kernel(c_ref, w_ref, b_ref, o_ref):
    c = c_ref[...]
    s = c * jax.nn.sigmoid(c)
    o_ref[...] = jnp.dot(s, w_ref[...], precision=HIGHEST, preferred_element_type=F32) + b_ref[...]


def _modulation(cc, ada_w, ada_b):
    depth, d, d6 = ada_w.shape
    tn = 1536
    return pl.pallas_call(
        _mod_kernel,
        grid=(depth, d6 // tn),
        in_specs=[
            pl.BlockSpec((MOD_ROWS, d), lambda l, j: (0, 0)),
            pl.BlockSpec((None, d, tn), lambda l, j: (l, 0, j)),
            pl.BlockSpec((None, 1, tn), lambda l, j: (l, 0, j)),
        ],
        out_specs=pl.BlockSpec((None, MOD_ROWS, tn), lambda l, j: (l, 0, j)),
        out_shape=jax.ShapeDtypeStruct((depth, MOD_ROWS, d6), F32),
        compiler_params=_cparams(("arbitrary", "arbitrary")),
        name="adaln_mod",
    )(cc, ada_w, ada_b.reshape(depth, 1, d6))


def _inproj_kernel(x_ref, mod_ref, g_ref, w_ref, o_ref, *, d):
    mod = mod_ref[...]
    h = _norm_mod(x_ref[...], g_ref[...], mod[:, d:2 * d], mod[:, 0:d]).astype(BF16)
    o_ref[...] = jnp.dot(h, w_ref[...], preferred_element_type=F32).astype(o_ref.dtype)


def _inproj(x, mod, g, w, n_lat_tiles):
    b, t, d = x.shape
    n_out = w.shape[1]
    return pl.pallas_call(
        functools.partial(_inproj_kernel, d=d),
        grid=(b, t // TM),
        in_specs=[
            pl.BlockSpec((None, TM, d), lambda i, j: (i, j, 0)),
            pl.BlockSpec((None, 1, 6 * d), lambda i, j: (jnp.where(j >= n_lat_tiles, b, i), 0, 0)),
            pl.BlockSpec((1, d), lambda i, j: (0, 0)),
            pl.BlockSpec((d, n_out), lambda i, j: (0, 0)),
        ],
        out_specs=pl.BlockSpec((None, TM, n_out), lambda i, j: (i, j, 0)),
        out_shape=jax.ShapeDtypeStruct((b, t, n_out), BF16),
        compiler_params=_cparams(("parallel", "parallel")),
        name="inproj",
    )(x, mod, g.reshape(1, d), w)


def _attn_proj_kernel(x_ref, mod_ref, g_ref, w_ref, qg_ref, kg_ref, cos_ref, sa_ref, sb_ref,
                      q_ref, k_ref, v_ref, *, d):
    mod = mod_ref[...]
    h = _norm_mod(x_ref[...], g_ref[...], mod[:, d:2 * d], mod[:, 0:d]).astype(BF16)
    p = jnp.dot(h, w_ref[...], preferred_element_type=F32)
    cos, sin_a, sin_b = cos_ref[...], sa_ref[...], sb_ref[...]
    n_q = Q_GROUP * N_KV_HEADS

    def head(ph, gain):
        y = ph * lax.rsqrt(jnp.mean(ph * ph, axis=-1, keepdims=True) + EPS) * gain
        return y * cos + pltpu.roll(y, LANES - 32, 1) * sin_a + pltpu.roll(y, 32, 1) * sin_b

    qg = qg_ref[...] * (HEAD_DIM ** -0.5 * math.log2(math.e))
    kg = kg_ref[...]
    for i in range(n_q):
        q_ref[:, i * HEAD_DIM:(i + 1) * HEAD_DIM] = head(p[:, i * HEAD_DIM:(i + 1) * HEAD_DIM], qg).astype(BF16)
    k0 = n_q * HEAD_DIM
    for i in range(N_KV_HEADS):
        k_ref[:, i * HEAD_DIM:(i + 1) * HEAD_DIM] = head(
            p[:, k0 + i * HEAD_DIM:k0 + (i + 1) * HEAD_DIM], kg).astype(BF16)
    v0 = k0 + N_KV_HEADS * HEAD_DIM
    for i in range(N_KV_HEADS):
        v_ref[:, 2 * i * HEAD_DIM:(2 * i + 1) * HEAD_DIM] = p[:, v0 + i * HEAD_DIM:v0 + (i + 1) * HEAD_DIM].astype(BF16)
        v_ref[:, (2 * i + 1) * HEAD_DIM:(2 * i + 2) * HEAD_DIM] = jnp.ones((p.shape[0], HEAD_DIM), BF16)


def _attn_proj(x, mod, g, w, q_gain, k_gain, rope, n_lat_tiles):
    b, t, d = x.shape
    n_out = w.shape[1]
    qw = Q_GROUP * N_KV_HEADS * HEAD_DIM
    kw = N_KV_HEADS * HEAD_DIM
    row = lambda i, j: (i, j, 0)
    tab = pl.BlockSpec((TM, HEAD_DIM), lambda i, j: (j, 0))
    return pl.pallas_call(
        functools.partial(_attn_proj_kernel, d=d),
        grid=(b, t // TM),
        in_specs=[
            pl.BlockSpec((None, TM, d), row),
            pl.BlockSpec((None, 1, 6 * d), lambda i, j: (jnp.where(j >= n_lat_tiles, b, i), 0, 0)),
            pl.BlockSpec((1, d), lambda i, j: (0, 0)),
            pl.BlockSpec((d, n_out), lambda i, j: (0, 0)),
            pl.BlockSpec((1, HEAD_DIM), lambda i, j: (0, 0)),
            pl.BlockSpec((1, HEAD_DIM), lambda i, j: (0, 0)),
            tab, tab, tab,
        ],
        out_specs=[
            pl.BlockSpec((None, TM, qw), row),
            pl.BlockSpec((None, TM, kw), row),
            pl.BlockSpec((None, TM, 2 * kw), row),
        ],
        out_shape=[
            jax.ShapeDtypeStruct((b, t, qw), BF16),
            jax.ShapeDtypeStruct((b, t, kw), BF16),
            jax.ShapeDtypeStruct((b, t, 2 * kw), BF16),
        ],
        compiler_params=_cparams(("parallel", "parallel")),
        name="attn_proj",
    )(x, mod, g.reshape(1, d), w, q_gain.reshape(1, HEAD_DIM), k_gain.reshape(1, HEAD_DIM), *rope)


def _attn_kernel(q_ref, k_ref, v_ref, o_ref, q_scr, s0, s1, p0, p1, a0, a1, m_scr, acc_scr, *, n_lat, n_ctx, tk):
    is_ctx = pl.program_id(2) * TM >= n_lat
    s_b, p_b, a_b = (s0, s1), (p0, p1), (a0, a1)
    for g in range(Q_GROUP):
        q_scr[g * TM:(g + 1) * TM, :] = q_ref[:, g * HEAD_DIM:(g + 1) * HEAD_DIM]
    m_scr[...] = jnp.full(m_scr.shape, -jnp.inf, F32)
    acc_scr[...] = jnp.zeros(acc_scr.shape, F32)

    def scores(start, size, slot):
        k = k_ref[pl.ds(start, size), :]
        s_b[slot][:, :size] = lax.dot_general(q_scr[...], k, (((1,), (1,)), ((), ())), preferred_element_type=F32)

    def softmax(size, slot):
        nb = size // LANES
        s = s_b[slot][:, :size]
        mx = s[:, 0:LANES]
        for j in range(1, nb):
            mx = jnp.maximum(mx, s[:, j * LANES:(j + 1) * LANES])
        m_prev = m_scr[...]
        m_new = jnp.maximum(m_prev, jnp.max(mx, axis=-1, keepdims=True))
        a_b[slot][...] = jnp.exp2(m_prev - m_new)
        m_scr[...] = m_new
        m_rep = jnp.concatenate([m_new] * nb, axis=1) if nb > 1 else m_new
        p_b[slot][:, :size] = jnp.exp2(s - m_rep).astype(BF16)

    def values(start, size, slot):
        a = a_b[slot][...]
        acc_scr[...] = jnp.concatenate([a, a], axis=1) * acc_scr[...] + jnp.dot(
            p_b[slot][:, :size], v_ref[pl.ds(start, size), :], preferred_element_type=F32)

    n = n_lat // tk

    @pl.when(is_ctx)
    def _():
        scores(n_lat, n_ctx, 0)
        softmax(n_ctx, 0)
        values(n_lat, n_ctx, 0)

    @pl.when(jnp.logical_not(is_ctx))
    def _():
        scores(0, tk, 0)
        softmax(tk, 0)
        scores(tk, tk, 1)

        def pair(i, carry):
            j = 2 + 2 * i
            values(pl.multiple_of((j - 2) * tk, tk), tk, 0)
            softmax(tk, 1)
            scores(pl.multiple_of(j * tk, tk), tk, 0)
            values(pl.multiple_of((j - 1) * tk, tk), tk, 1)
            softmax(tk, 0)
            scores(pl.multiple_of((j + 1) * tk, tk), tk, 1)
            return carry

        lax.fori_loop(0, (n - 2) // 2, pair, 0)
        values((n - 2) * tk, tk, 0)
        softmax(tk, 1)
        scores(n_lat, n_ctx, 0)
        values((n - 1) * tk, tk, 1)
        softmax(n_ctx, 0)
        values(n_lat, n_ctx, 0)

    out = acc_scr[:, :HEAD_DIM] / acc_scr[:, HEAD_DIM:]
    for g in range(Q_GROUP):
        o_ref[:, g * HEAD_DIM:(g + 1) * HEAD_DIM] = out[g * TM:(g + 1) * TM].astype(o_ref.dtype)


def _attention(q, k, v, n_lat):
    b, t, _ = q.shape
    n_ctx = t - n_lat
    gw = Q_GROUP * HEAD_DIM
    tk = ATTN_TK
    assert n_lat % (2 * tk) == 0 and n_ctx <= tk and n_ctx % LANES == 0
    m = Q_GROUP * TM
    return pl.pallas_call(
        functools.partial(_attn_kernel, n_lat=n_lat, n_ctx=n_ctx, tk=tk),
        grid=(b, N_KV_HEADS, t // TM),
        in_specs=[
            pl.BlockSpec((None, TM, gw), lambda i, h, j: (i, j, h)),
            pl.BlockSpec((None, t, HEAD_DIM), lambda i, h, j: (i, 0, h)),
            pl.BlockSpec((None, t, 2 * HEAD_DIM), lambda i, h, j: (i, 0, h)),
        ],
        out_specs=pl.BlockSpec((None, TM, gw), lambda i, h, j: (i, j, h)),
        out_shape=jax.ShapeDtypeStruct(q.shape, BF16),
        scratch_shapes=[
            pltpu.VMEM((m, HEAD_DIM), BF16),
            pltpu.VMEM((m, tk), F32), pltpu.VMEM((m, tk), F32),
            pltpu.VMEM((m, tk), BF16), pltpu.VMEM((m, tk), BF16),
            pltpu.VMEM((m, LANES), F32), pltpu.VMEM((m, LANES), F32),
            pltpu.VMEM((m, LANES), F32),
            pltpu.VMEM((m, 2 * HEAD_DIM), F32),
        ],
        compiler_params=_cparams(("parallel", "parallel", "parallel")),
        name="attention",
    )(q, k, v)


def _pack_bf16_pairs(a):
    n = a.shape[1] // 2
    bits = pltpu.bitcast(a.astype(BF16).astype(F32), jnp.uint32)
    return (bits[:, :n] >> 16) | (bits[:, n:] & jnp.uint32(0xFFFF0000))


def _unpack_bf16_pairs(w):
    lo = pltpu.bitcast(w << 16, F32)
    hi = pltpu.bitcast(w & jnp.uint32(0xFFFF0000), F32)
    return jnp.concatenate([lo, hi], axis=1)


def _residual_router(y, x, mod, w_out, g2n, rwt, rb, outs, cnt_scr, *, d):
    xo_ref, h2_ref, idx_ref, gate_ref, rank_ref, cnt_ref = outs
    first = jnp.logical_and(pl.program_id(0) == 0, pl.program_id(1) == 0)

    @pl.when(first)
    def _():
        cnt_scr[...] = jnp.zeros(cnt_scr.shape, F32)

    xn = x + mod[:, 2 * d:3 * d] * jnp.dot(y, w_out, preferred_element_type=F32)
    xo_ref[...] = xn
    h2 = _norm_mod(xn, g2n, mod[:, 4 * d:5 * d], mod[:, 3 * d:4 * d])
    h2_ref[...] = _pack_bf16_pairs(h2)
    lg = lax.dot_general(rwt, h2, (((1,), (1,)), ((), ())), precision=HIGHEST,
                         preferred_element_type=F32) + rb
    row = lax.broadcasted_iota(jnp.int32, lg.shape, 0)
    vals, idxs = [], []
    for _ in range(TOP_K):
        m = jnp.max(lg, axis=0, keepdims=True)
        i = jnp.min(jnp.where(lg == m, row, N_EXPERTS), axis=0, keepdims=True)
        vals.append(m)
        idxs.append(i)
        lg = jnp.where(row == i, -jnp.inf, lg)
    es = [jnp.exp(vv - vals[0]) for vv in vals]
    tot = es[0] + es[1] + es[2] + es[3]
    idx_ref[...] = jnp.concatenate(idxs, axis=0)
    gate_ref[...] = jnp.concatenate([e / tot for e in es], axis=0)

    tm = lg.shape[1]
    earlier = (lax.broadcasted_iota(jnp.int32, (tm, tm), 0) < lax.broadcasted_iota(jnp.int32, (tm, tm), 1)).astype(BF16)
    run = cnt_scr[...]
    ranks = []
    for i in idxs:
        hit = row == i
        before = jnp.dot(hit.astype(BF16), earlier, preferred_element_type=F32)
        ranks.append(jnp.sum(jnp.where(hit, run + before, 0.0), axis=0, keepdims=True))
        run = run + jnp.sum(hit.astype(F32), axis=1, keepdims=True)
    cnt_scr[...] = run
    rank_ref[...] = jnp.concatenate(ranks, axis=0).astype(jnp.int32)
    cnt_ref[...] = run.astype(jnp.int32)


def _post_kernel(y_ref, x_ref, mod_ref, w_ref, g_ref, rwt_ref, rb_ref, *rest, d):
    _residual_router(y_ref[...], x_ref[...], mod_ref[...], w_ref[...], g_ref[...], rwt_ref[...], rb_ref[...],
                     rest[:-1], rest[-1], d=d)


def _post_specs(b, t, d, n_lat_tiles):
    nt = t // TM
    row = lambda i, j: (i, j, 0)
    const2 = lambda i, j: (0, 0)
    in_tail = [
        pl.BlockSpec((d, d), const2),
        pl.BlockSpec((1, d), const2),
        pl.BlockSpec((N_EXPERTS, d), const2),
        pl.BlockSpec((N_EXPERTS, 1), const2),
    ]
    per_token = pl.BlockSpec((TOP_K, TM), lambda i, j: (0, i * nt + j))
    out_specs = [
        pl.BlockSpec((None, TM, d), row),
        pl.BlockSpec((None, TM, d // 2), row),
        per_token, per_token, per_token,
        pl.BlockSpec((N_EXPERTS, 1), const2),
    ]
    out_shape = [
        jax.ShapeDtypeStruct((b, t, d), F32),
        jax.ShapeDtypeStruct((b, t, d // 2), jnp.uint32),
        jax.ShapeDtypeStruct((TOP_K, b * t), jnp.int32),
        jax.ShapeDtypeStruct((TOP_K, b * t), F32),
        jax.ShapeDtypeStruct((TOP_K, b * t), jnp.int32),
        jax.ShapeDtypeStruct((N_EXPERTS, 1), jnp.int32),
    ]
    mod_spec = pl.BlockSpec((None, 1, 6 * d), lambda i, j: (jnp.where(j >= n_lat_tiles, b, i), 0, 0))
    return mod_spec, in_tail, out_specs, out_shape


def _post(y, x, mod, w_out, g2n, router_w, router_b, n_lat_tiles):
    b, t, d = x.shape
    mod_spec, in_tail, out_specs, out_shape = _post_specs(b, t, d, n_lat_tiles)
    row = lambda i, j: (i, j, 0)
    return pl.pallas_call(
        functools.partial(_post_kernel, d=d),
        grid=(b, t // TM),
        in_specs=[pl.BlockSpec((None, TM, d), row), pl.BlockSpec((None, TM, d), row), mod_spec] + in_tail,
        out_specs=out_specs,
        out_shape=out_shape,
        scratch_shapes=[pltpu.VMEM((N_EXPERTS, 1), F32)],
        compiler_params=_cparams(("arbitrary", "arbitrary")),
        name="outproj_router",
    )(y, x, mod, w_out, g2n.reshape(1, d), router_w.T, router_b.reshape(N_EXPERTS, 1))


def _gmlp_kernel(x_ref, mod_ref, g1_ref, win_ref, vg_ref, ws_ref, bs_ref, w_ref, g_ref, rwt_ref, rb_ref,
                 *rest, d):
    outs, cnt_scr, y_scr = rest[:-2], rest[-2], rest[-1]
    mod = mod_ref[...]
    x = x_ref[...]
    h = _norm_mod(x, g1_ref[...], mod[:, d:2 * d], mod[:, 0:d]).astype(BF16)
    z = jnp.dot(h, win_ref[...], preferred_element_type=F32)
    z = 0.5 * z * (1.0 + lax.erf(z * (2.0 ** -0.5)))
    width = z.shape[1] // 2
    u, v = z[:, :width], z[:, width:]
    v = (v * lax.rsqrt(jnp.mean(v * v, axis=-1, keepdims=True) + EPS) * vg_ref[...]).astype(BF16)
    gd = width // GMLP_GROUPS
    for n in range(TM // CHUNK):
        r = slice(n * CHUNK, (n + 1) * CHUNK)
        for g in range(GMLP_GROUPS):
            cs = slice(g * gd, (g + 1) * gd)
            mixed = jnp.dot(ws_ref[g], v[r, cs], preferred_element_type=F32) + bs_ref[:, g:g + 1]
            y_scr[r, cs] = (u[r, cs] * mixed).astype(BF16)
    _residual_router(y_scr[...], x, mod, w_ref[...], g_ref[...], rwt_ref[...], rb_ref[...], outs, cnt_scr, d=d)


def _gmlp_layer(x, mod, g1n, w_in, v_gain, w_s, b_s, w_out, g2n, router_w, router_b, n_lat_tiles):
    b, t, d = x.shape
    width = w_in.shape[1] // 2
    mod_spec, in_tail, out_specs, out_shape = _post_specs(b, t, d, n_lat_tiles)
    row = lambda i, j: (i, j, 0)
    const2 = lambda i, j: (0, 0)
    return pl.pallas_call(
        functools.partial(_gmlp_kernel, d=d),
        grid=(b, t // TM),
        in_specs=[
            pl.BlockSpec((None, TM, d), row),
            mod_spec,
            pl.BlockSpec((1, d), const2),
            pl.BlockSpec((d, 2 * width), const2),
            pl.BlockSpec((1, width), const2),
            pl.BlockSpec((GMLP_GROUPS, CHUNK, CHUNK), lambda i, j: (0, 0, 0)),
            pl.BlockSpec((CHUNK, GMLP_GROUPS), const2),
        ] + in_tail,
        out_specs=out_specs,
        out_shape=out_shape,
        scratch_shapes=[pltpu.VMEM((N_EXPERTS, 1), F32), pltpu.VMEM((TM, width), BF16)],
        compiler_params=_cparams(("arbitrary", "arbitrary")),
        name="gmlp_layer",
    )(x, mod, g1n.reshape(1, d), w_in, v_gain.reshape(1, width), w_s, b_s.T, w_out, g2n.reshape(1, d),
      router_w.T, router_b.reshape(N_EXPERTS, 1))


def _filter_mlp_kernel(z_ref, t_ref, w1, b1, w2, b2, w3, b3, w4, b4, fr, dl, kf_ref, kb_ref, *, width, tl):
    dot = functools.partial(jnp.dot, precision=HIGHEST, preferred_element_type=F32)
    f = fr[...]
    a = jnp.sin(f * (dot(z_ref[...], w1[...]) + b1[...]))
    a = jnp.sin(f * (dot(a, w2[...]) + b2[...]))
    a = jnp.sin(f * (dot(a, w3[...]) + b3[...]))
    k = dot(a, w4[...]) + b4[...]
    window = jnp.exp(-t_ref[...] * dl[...])
    kf_ref[...] = k[:, :width] * window
    pos = pl.program_id(0) * tl + lax.broadcasted_iota(jnp.int32, (tl, 1), 0)
    kb_ref[...] = jnp.where(pos == 0, 0.0, k[:, width:] * window)


def _hyena_filters(length, f_w1, f_b1, f_w2, f_b2, f_w3, f_b3, f_w4, f_b4, freq):
    width = f_w4.shape[1] // 2
    hid = f_w1.shape[1]
    emb = 2 * N_BANDS + 1
    t = jnp.linspace(0.0, 1.0, length, dtype=F32)[:, None]
    w = 2.0 * math.pi * jnp.arange(length, dtype=F32)[:, None] / length
    f = jnp.linspace(1e-4, N_BANDS - 1, N_BANDS, dtype=F32)[None, :]
    z = jnp.concatenate([t, jnp.cos(f * w), -jnp.sin(f * w), jnp.zeros((length, hid - emb), F32)], axis=-1)
    w1p = jnp.concatenate([f_w1, jnp.zeros((hid - emb, hid), F32)], axis=0)
    deltas = jnp.abs(jnp.linspace(MIN_DECAY, MAX_DECAY, width, dtype=F32))[None, :]
    tl = min(length, 512)
    full = lambda shape: pl.BlockSpec(shape, lambda i: (0, 0))
    return pl.pallas_call(
        functools.partial(_filter_mlp_kernel, width=width, tl=tl),
        grid=(length // tl,),
        in_specs=[
            pl.BlockSpec((tl, hid), lambda i: (i, 0)),
            pl.BlockSpec((tl, 1), lambda i: (i, 0)),
            full((hid, hid)), full((1, hid)), full((hid, hid)), full((1, hid)), full((hid, hid)), full((1, hid)),
            full((hid, 2 * width)), full((1, 2 * width)), full((1, hid)), full((1, width)),
        ],
        out_specs=[pl.BlockSpec((tl, width), lambda i: (i, 0)), pl.BlockSpec((tl, width), lambda i: (i, 0))],
        out_shape=[jax.ShapeDtypeStruct((length, width), F32), jax.ShapeDtypeStruct((length, width), F32)],
        compiler_params=_cparams(("parallel",)),
        name="hyena_filter_mlp",
    )(z, t, w1p, f_b1.reshape(1, hid), f_w2, f_b2.reshape(1, hid), f_w3, f_b3.reshape(1, hid),
      f_w4, f_b4.reshape(1, 2 * width), freq.reshape(1, hid), deltas)


def _fft_plan(length):
    n = 2 * length
    n1 = n // FFT_N2
    nz = n1 // 2
    ku = -(-(nz + 1) // 8) * 8
    return n, n1, nz, ku


def _fft_tables(length):
    n, n1, nz, ku = _fft_plan(length)
    i2 = np.arange(FFT_N2)[:, None, None]
    k1 = np.arange(ku)[None, :, None]
    i1 = np.arange(nz)[None, None, :]
    phi = 2.0 * np.pi * (((FFT_N2 * i1 + i2) * k1) % n) / n
    fwd = np.concatenate([np.cos(phi), -np.sin(phi)], axis=1)
    wgt = np.where((k1 == 0) | (k1 == nz), 1.0, np.where(k1 < nz, 2.0, 0.0)) / n
    inv = np.concatenate([np.cos(phi) * wgt, -np.sin(phi) * wgt], axis=1).transpose(0, 2, 1)
    th = 2.0 * np.pi * ((np.arange(FFT_N2)[:, None] * np.arange(FFT_N2)[None, :]) % FFT_N2) / FFT_N2
    c, s = np.cos(th), np.sin(th)
    f2 = np.block([[c, s], [-s, c]])
    f2i = np.block([[c, -s], [s, c]])
    as_bf = lambda a: jnp.asarray(a, F32).astype(BF16)
    return as_bf(fwd), as_bf(inv), as_bf(f2), as_bf(f2i)


def _fft_stage1(src_ref, fwd_ref, a_ref, *, nz, ku):
    slab = 2 * FFT_N2

    def body(i2, carry):
        rows = src_ref[pl.ds(i2, nz, stride=FFT_N2), :].astype(BF16)
        r = jnp.dot(fwd_ref[i2], rows, preferred_element_type=F32)
        a_ref[pl.ds(i2, ku, stride=slab), :] = r[:ku]
        a_ref[pl.ds(FFT_N2 + i2, ku, stride=slab), :] = r[ku:]
        return carry

    lax.fori_loop(0, FFT_N2, body, 0, unroll=FFT_UNROLL)


def _short_conv(p_ref, w_ref, b_ref, pad_ref, emit, *, length):
    step = min(length, 512)
    cb = p_ref.shape[-1]
    pad_ref[pl.ds(0, 8), :] = jnp.zeros((8, cb), F32)
    pad_ref[pl.ds(length + 8, 8), :] = jnp.zeros((8, cb), F32)
    for j in range(length // step):
        pad_ref[pl.ds(8 + j * step, step), :] = p_ref[pl.ds(j * step, step), :].astype(F32)
    w = w_ref[...]
    for j in range(length // step):
        r0 = j * step
        val = (pad_ref[pl.ds(r0 + 7, step), :] * w[0:1] + pad_ref[pl.ds(r0 + 8, step), :] * w[1:2]
               + pad_ref[pl.ds(r0 + 9, step), :] * w[2:3] + b_ref[...])
        emit(r0, step, val)


def _kf_kernel(kf_ref, kb_ref, fwd_ref, f2_ref, o_ref, a_ref, *, nz, ku):
    slab = 2 * FFT_N2
    for src, sign in ((kf_ref, 1.0), (kb_ref, -1.0)):
        _fft_stage1(src, fwd_ref, a_ref, nz=nz, ku=ku)

        def body(k1, carry, sign=sign, first=(src is kf_ref)):
            a = a_ref[pl.ds(pl.multiple_of(k1 * slab, slab), slab), :].astype(BF16)
            xk = jnp.dot(f2_ref[...], a, preferred_element_type=F32)
            if first:
                o_ref[k1] = xk
            else:
                o_ref[k1, :FFT_N2, :] = o_ref[k1, :FFT_N2, :] + xk[:FFT_N2]
                o_ref[k1, FFT_N2:, :] = o_ref[k1, FFT_N2:, :] - xk[FFT_N2:]
            return carry

        lax.fori_loop(0, ku, body, 0, unroll=math.gcd(ku, FFT_UNROLL))


def _filter_spectrum(k_fwd, k_bwd, tables):
    length, width = k_fwd.shape
    _, _, nz, ku = _fft_plan(length)
    fwd, _, f2, _ = tables
    blk = pl.BlockSpec((length, FFT_CB), lambda c: (0, c))
    return pl.pallas_call(
        functools.partial(_kf_kernel, nz=nz, ku=ku),
        grid=(width // FFT_CB,),
        in_specs=[
            blk, blk,
            pl.BlockSpec(fwd.shape, lambda c: (0, 0, 0)),
            pl.BlockSpec(f2.shape, lambda c: (0, 0)),
        ],
        out_specs=pl.BlockSpec((ku, 2 * FFT_N2, FFT_CB), lambda c: (0, 0, c)),
        out_shape=jax.ShapeDtypeStruct((ku, 2 * FFT_N2, width), F32),
        scratch_shapes=[pltpu.VMEM((ku * 2 * FFT_N2, FFT_CB), F32)],
        compiler_params=_cparams(("parallel",)),
        name="hyena_filter_spectrum",
    )(k_fwd, k_bwd, fwd, f2)


def _hyena_conv_kernel(px0_ref, px1_ref, pv_ref, w0_ref, w1_ref, wv_ref, b0_ref, b1_ref, bv_ref, dsk_ref,
                       kf_ref, fwd_ref, inv_ref, f2_ref, f2i_ref, o_ref, a_ref, vv_ref, *, length, nz, ku):
    slab = 2 * FFT_N2

    def set_vv(r0, rows, val):
        vv_ref[pl.ds(r0, rows), :] = val

    def mul_vv(r0, rows, val):
        vv_ref[pl.ds(r0, rows), :] = vv_ref[pl.ds(r0, rows), :] * val

    _short_conv(pv_ref, wv_ref, bv_ref, a_ref, set_vv, length=length)
    _short_conv(px1_ref, w1_ref, b1_ref, a_ref, mul_vv, length=length)
    _fft_stage1(vv_ref, fwd_ref, a_ref, nz=nz, ku=ku)

    def freq_body(k1, carry):
        rows = pl.ds(k1 * slab if isinstance(k1, int) else pl.multiple_of(k1 * slab, slab), slab)
        xk = jnp.dot(f2_ref[...], a_ref[rows, :].astype(BF16), preferred_element_type=F32)
        kf = kf_ref[k1]
        xr, xi = xk[:FFT_N2], xk[FFT_N2:]
        kr, ki = kf[:FFT_N2], kf[FFT_N2:]
        prod = jnp.concatenate([xr * kr - xi * ki, xr * ki + xi * kr], axis=0).astype(BF16)
        a_ref[rows, :] = jnp.dot(f2i_ref[...], prod, preferred_element_type=F32)
        return carry

    lax.fori_loop(0, nz, freq_body, 0, unroll=math.gcd(nz, FFT_UNROLL))
    freq_body(nz, 0)

    def time_body(i2, carry):
        re = a_ref[pl.ds(i2, ku, stride=slab), :]
        im = a_ref[pl.ds(FFT_N2 + i2, ku, stride=slab), :]
        q = jnp.concatenate([re, im], axis=0).astype(BF16)
        y = jnp.dot(inv_ref[i2], q, preferred_element_type=F32)
        rows = pl.ds(i2, nz, stride=FFT_N2)
        vv_ref[rows, :] = y + vv_ref[rows, :] * dsk_ref[...]
        return carry

    lax.fori_loop(0, FFT_N2, time_body, 0, unroll=FFT_UNROLL)

    def emit_out(r0, rows, val):
        o_ref[pl.ds(r0, rows), :] = (vv_ref[pl.ds(r0, rows), :] * val).astype(o_ref.dtype)

    _short_conv(px0_ref, w0_ref, b0_ref, a_ref, emit_out, length=length)


def _hyena_conv(p, conv_w, conv_b, d_skip, kf, tables, length):
    b, _, w3 = p.shape
    width = w3 // 3
    ncb = width // FFT_CB
    _, _, nz, ku = _fft_plan(length)
    fwd, inv, f2, f2i = tables
    pblk = lambda part: _resident((None, length, FFT_CB), lambda c, i: (i, 0, part * ncb + c))
    wblk = lambda part: pl.BlockSpec((3, FFT_CB), lambda c, i: (0, part * ncb + c))
    bblk = lambda part: pl.BlockSpec((1, FFT_CB), lambda c, i: (0, part * ncb + c))
    return pl.pallas_call(
        functools.partial(_hyena_conv_kernel, length=length, nz=nz, ku=ku),
        grid=(ncb, b),
        in_specs=[
            pblk(0), pblk(1), pblk(2), wblk(0), wblk(1), wblk(2), bblk(0), bblk(1), bblk(2),
            pl.BlockSpec((1, FFT_CB), lambda c, i: (0, c)),
            _resident((ku, 2 * FFT_N2, FFT_CB), lambda c, i: (0, 0, c)),
            _resident(fwd.shape, lambda c, i: (0, 0, 0)),
            _resident(inv.shape, lambda c, i: (0, 0, 0)),
            _resident(f2.shape, lambda c, i: (0, 0)),
            _resident(f2i.shape, lambda c, i: (0, 0)),
        ],
        out_specs=pl.BlockSpec((None, length, FFT_CB), lambda c, i: (i, 0, c)),
        out_shape=jax.ShapeDtypeStruct((b, length, width), BF16),
        scratch_shapes=[
            pltpu.VMEM((max(ku * 2 * FFT_N2, length + 16), FFT_CB), F32),
            pltpu.VMEM((length, FFT_CB), F32),
        ],
        compiler_params=_cparams(("arbitrary", "arbitrary")),
        name="hyena_long_conv",
    )(p, p, p, conv_w, conv_w, conv_w, conv_b.reshape(1, w3), conv_b.reshape(1, w3), conv_b.reshape(1, w3),
      d_skip.reshape(1, width), kf, fwd, inv, f2, f2i)


def _dense_dft_tables(length):
    n = 2 * length
    th = 2.0 * np.pi * ((np.arange(n)[:, None] * np.arange(length)[None, :]) % n) / n
    fwd = np.concatenate([np.cos(th), -np.sin(th)], axis=0)
    inv = np.concatenate([np.cos(th), -np.sin(th)], axis=0).T / n
    return jnp.asarray(fwd, F32).astype(BF16), jnp.asarray(inv, F32).astype(BF16)


def _kf_dense_kernel(kf_ref, kb_ref, fwd_ref, o_ref, *, n):
    xf = jnp.dot(fwd_ref[...], kf_ref[...].astype(BF16), preferred_element_type=F32)
    xb = jnp.dot(fwd_ref[...], kb_ref[...].astype(BF16), preferred_element_type=F32)
    o_ref[:n, :] = xf[:n] + xb[:n]
    o_ref[n:, :] = xf[n:] - xb[n:]


def _filter_spectrum_dense(k_fwd, k_bwd, fwd):
    length, width = k_fwd.shape
    n = 2 * length
    blk = pl.BlockSpec((length, FFT_CB), lambda c: (0, c))
    return pl.pallas_call(
        functools.partial(_kf_dense_kernel, n=n),
        grid=(width // FFT_CB,),
        in_specs=[blk, blk, pl.BlockSpec(fwd.shape, lambda c: (0, 0))],
        out_specs=pl.BlockSpec((2 * n, FFT_CB), lambda c: (0, c)),
        out_shape=jax.ShapeDtypeStruct((2 * n, width), F32),
        compiler_params=_cparams(("parallel",)),
        name="hyena_ctx_filter_spectrum",
    )(k_fwd, k_bwd, fwd)


def _hyena_ctx_kernel(px0_ref, px1_ref, pv_ref, w0_ref, w1_ref, wv_ref, b0_ref, b1_ref, bv_ref, dsk_ref,
                      kf_ref, fwd_ref, inv_ref, o_ref, pad_ref, vv_ref, *, length):
    n = 2 * length

    def set_vv(r0, rows, val):
        vv_ref[pl.ds(r0, rows), :] = val

    def mul_vv(r0, rows, val):
        vv_ref[pl.ds(r0, rows), :] = vv_ref[pl.ds(r0, rows), :] * val

    _short_conv(pv_ref, wv_ref, bv_ref, pad_ref, set_vv, length=length)
    _short_conv(px1_ref, w1_ref, b1_ref, pad_ref, mul_vv, length=length)
    vv = vv_ref[...]
    xk = jnp.dot(fwd_ref[...], vv.astype(BF16), preferred_element_type=F32)
    kf = kf_ref[...]
    xr, xi, kr, ki = xk[:n], xk[n:], kf[:n], kf[n:]
    prod = jnp.concatenate([xr * kr - xi * ki, xr * ki + xi * kr], axis=0).astype(BF16)
    vv_ref[...] = jnp.dot(inv_ref[...], prod, preferred_element_type=F32) + vv * dsk_ref[...]

    def emit_out(r0, rows, val):
        o_ref[pl.ds(r0, rows), :] = (vv_ref[pl.ds(r0, rows), :] * val).astype(o_ref.dtype)

    _short_conv(px0_ref, w0_ref, b0_ref, pad_ref, emit_out, length=length)


def _hyena_ctx_conv(p, conv_w, conv_b, d_skip, kf, fwd, inv, n_lat, length):
    b, _, w3 = p.shape
    width = w3 // 3
    ncb = width // FFT_CB
    rb = n_lat // length
    pblk = lambda part: pl.BlockSpec((None, length, FFT_CB), lambda c, i: (i, rb, part * ncb + c))
    wblk = lambda part: pl.BlockSpec((3, FFT_CB), lambda c, i: (0, part * ncb + c))
    bblk = lambda part: pl.BlockSpec((1, FFT_CB), lambda c, i: (0, part * ncb + c))
    return pl.pallas_call(
        functools.partial(_hyena_ctx_kernel, length=length),
        grid=(ncb, b),
        in_specs=[
            pblk(0), pblk(1), pblk(2), wblk(0), wblk(1), wblk(2), bblk(0), bblk(1), bblk(2),
            pl.BlockSpec((1, FFT_CB), lambda c, i: (0, c)),
            pl.BlockSpec((4 * length, FFT_CB), lambda c, i: (0, c)),
            pl.BlockSpec(fwd.shape, lambda c, i: (0, 0)),
            pl.BlockSpec(inv.shape, lambda c, i: (0, 0)),
        ],
        out_specs=pl.BlockSpec((None, length, FFT_CB), lambda c, i: (i, 0, c)),
        out_shape=jax.ShapeDtypeStruct((b, length, width), BF16),
        scratch_shapes=[pltpu.VMEM((length + 16, FFT_CB), F32), pltpu.VMEM((length, FFT_CB), F32)],
        compiler_params=_cparams(("arbitrary", "arbitrary")),
        name="hyena_ctx_conv",
    )(p, p, p, conv_w, conv_w, conv_w, conv_b.reshape(1, w3), conv_b.reshape(1, w3), conv_b.reshape(1, w3),
      d_skip.reshape(1, width), kf, fwd, inv)


def _row_copy(src, dst, sem):
    return pltpu.make_async_copy(src, dst, sem)


def _scatter_kernel(cnt_ref, start_ref, dest_ref, h_ref, x_hbm, src0, src1, src2, src3, zero_scr, sem, zsem, *, n_tok):
    rows, half = h_ref.shape
    src_scr = (src0, src1, src2, src3)
    n_real = TOP_K * n_tok
    row_iota = lax.broadcasted_iota(jnp.int32, (rows, LANES), 0)

    def ids_from(base):
        return pltpu.bitcast(base + row_iota, jnp.uint32)

    @pl.when(pl.program_id(0) == 0)
    def _():
        zero_scr[:, :half] = jnp.zeros((rows, half), jnp.uint32)

        def per_expert(e, total):
            cnt = cnt_ref[e]
            n_pad = (EXPERT_BLOCK - cnt % EXPERT_BLOCK) % EXPERT_BLOCK
            base = start_ref[e] + cnt
            zero_scr[:, half:] = ids_from(n_real + total)

            def one(r, carry):
                _row_copy(zero_scr.at[pl.ds(r, 1)], x_hbm.at[pl.ds(base + r, 1)], zsem).start()
                return carry

            def wait_one(r, carry):
                _row_copy(zero_scr.at[pl.ds(0, 1)], x_hbm.at[pl.ds(0, 1)], zsem).wait()
                return carry

            lax.fori_loop(0, n_pad, one, 0)
            lax.fori_loop(0, n_pad, wait_one, 0)
            return total + n_pad

        total = lax.fori_loop(0, N_EXPERTS, per_expert, 0)
        first_free = start_ref[N_EXPERTS] // EXPERT_BLOCK
        n_blocks = x_hbm.shape[0] // EXPERT_BLOCK

        def tail(i, carry):
            zero_scr[:, half:] = ids_from(n_real + total + (i - first_free) * EXPERT_BLOCK)
            cp = _row_copy(zero_scr, x_hbm.at[pl.ds(pl.multiple_of(i * EXPERT_BLOCK, EXPERT_BLOCK), EXPERT_BLOCK)],
                           zsem)
            cp.start()
            cp.wait()
            return carry

        lax.fori_loop(first_free, n_blocks, tail, 0)

    h = h_ref[...]
    for k in range(TOP_K):
        src_scr[k][:, :half] = h
        src_scr[k][:, half:] = ids_from(k * n_tok + pl.program_id(0) * rows)

    def issue(r, carry):
        for k in range(TOP_K):
            _row_copy(src_scr[k].at[pl.ds(r, 1)], x_hbm.at[pl.ds(dest_ref[0, r * TOP_K + k], 1)], sem).start(
                priority=k % 2)
        return carry

    lax.fori_loop(0, rows, issue, 0, unroll=4)
    for k in range(TOP_K):
        _row_copy(src_scr[k], x_hbm.at[pl.ds(0, rows)], sem).wait()


def _dispatch(h2p, dest_tiles, counts, pad_start, n_blocks):
    n_tok, half = h2p.shape
    assert TM == EXPERT_BLOCK
    width = half + LANES
    return pl.pallas_call(
        functools.partial(_scatter_kernel, n_tok=n_tok),
        grid_spec=pltpu.PrefetchScalarGridSpec(
            num_scalar_prefetch=2,
            grid=(n_tok // TM,),
            in_specs=[
                pl.BlockSpec((None, 1, TOP_K * TM), lambda i, c, s: (i, 0, 0), memory_space=pltpu.SMEM),
                pl.BlockSpec((TM, half), lambda i, c, s: (i, 0)),
            ],
            out_specs=pl.BlockSpec(memory_space=pl.ANY),
            scratch_shapes=[pltpu.VMEM((TM, width), jnp.uint32)] * TOP_K + [
                pltpu.VMEM((EXPERT_BLOCK, width), jnp.uint32), pltpu.SemaphoreType.DMA(()), pltpu.SemaphoreType.DMA(())],
        ),
        out_shape=jax.ShapeDtypeStruct((n_blocks * EXPERT_BLOCK, width), jnp.uint32),
        compiler_params=_cparams(("arbitrary",)),
        name="moe_dispatch",
    )(counts, pad_start, dest_tiles, h2p)


def _ffn_kernel(be_ref, ids_prev_ref, ids_last_ref, x_ref, w1_ref, b1_ref, w2_ref, b2_ref, y_hbm,
                buf0, buf1, w1_bf, w2_bf, sem0, sem1, *, d_ff, half):
    i = pl.program_id(0)
    n = pl.num_programs(0)
    rows = x_ref.shape[0]
    spare = y_hbm.shape[0] - 2 * rows

    def drain(src, sem, ids_ref):
        for r in range(rows):
            _row_copy(src.at[pl.ds(r, 1)], y_hbm.at[pl.ds(ids_ref[0, r], 1)], sem).start(priority=r % 2)

    def wait(src, sem):
        _row_copy(src, y_hbm.at[pl.ds(0, rows)], sem).wait()

    @pl.when(i == 0)
    def _():
        buf0[...] = jnp.zeros(buf0.shape, buf0.dtype)
        buf1[...] = jnp.zeros(buf1.shape, buf1.dtype)
        _row_copy(buf0, y_hbm.at[pl.ds(spare + rows, rows)], sem0).start()

    @pl.when(jnp.logical_or(i == 0, be_ref[i] != be_ref[jnp.maximum(i - 1, 0)]))
    def _():
        w1_bf[...] = w1_ref[...].astype(BF16)
        w2_bf[...] = w2_ref[...].astype(BF16)

    def step(src, ssem, dst, dsem):
        drain(src, ssem, ids_prev_ref)
        x = _unpack_bf16_pairs(x_ref[:, :half]).astype(BF16)
        hh = jnp.dot(x, w1_bf[...], preferred_element_type=F32) + b1_ref[...]
        glu = jnp.minimum(hh[:, :d_ff], SWIGLU_LIMIT)
        lin = jnp.clip(hh[:, d_ff:], -SWIGLU_LIMIT, SWIGLU_LIMIT)
        act = (glu * jax.nn.sigmoid(SWIGLU_ALPHA * glu) * (lin + 1.0)).astype(BF16)
        y = _pack_bf16_pairs(jnp.dot(act, w2_bf[...], preferred_element_type=F32) + b2_ref[...])
        wait(dst, dsem)
        dst[...] = y

    @pl.when(i % 2 == 0)
    def _():
        step(buf1, sem1, buf0, sem0)

    @pl.when(i % 2 == 1)
    def _():
        step(buf0, sem0, buf1, sem1)

    @pl.when(i == n - 1)
    def _():
        last_even = (y_hbm.shape[0] // rows - 3) % 2 == 0
        last, lsem, prev, psem = (buf0, sem0, buf1, sem1) if last_even else (buf1, sem1, buf0, sem0)
        drain(last, lsem, ids_last_ref)
        wait(prev, psem)
        wait(last, lsem)


def _expert_ffn(x_disp, ids, block_e, layer, w1, b1, w2, b2):
    rows, width = x_disp.shape
    half = width - LANES
    depth, n_e, d, ff2 = w1.shape
    d_ff = ff2 // 2
    n_blocks = rows // EXPERT_BLOCK
    ids_prev = jnp.concatenate([rows + jnp.arange(EXPERT_BLOCK, dtype=jnp.int32), ids])
    ids_prev = ids_prev.reshape(n_blocks + 1, 1, EXPERT_BLOCK)
    return pl.pallas_call(
        functools.partial(_ffn_kernel, d_ff=d_ff, half=half),
        grid_spec=pltpu.PrefetchScalarGridSpec(
            num_scalar_prefetch=1,
            grid=(n_blocks,),
            in_specs=[
                pl.BlockSpec((None, 1, EXPERT_BLOCK), lambda i, be: (i, 0, 0), memory_space=pltpu.SMEM),
                pl.BlockSpec((None, 1, EXPERT_BLOCK), lambda i, be: (n_blocks, 0, 0), memory_space=pltpu.SMEM),
                pl.BlockSpec((EXPERT_BLOCK, width), lambda i, be: (i, 0)),
                pl.BlockSpec((None, None, d, ff2), lambda i, be: (layer, be[i], 0, 0)),
                pl.BlockSpec((None, None, 1, ff2), lambda i, be: (layer, be[i], 0, 0)),
                pl.BlockSpec((None, None, d_ff, d), lambda i, be: (layer, be[i], 0, 0)),
                pl.BlockSpec((None, None, 1, d), lambda i, be: (layer, be[i], 0, 0)),
            ],
            out_specs=pl.BlockSpec(memory_space=pl.ANY),
            scratch_shapes=[pltpu.VMEM((EXPERT_BLOCK, half), jnp.uint32), pltpu.VMEM((EXPERT_BLOCK, half), jnp.uint32),
                            pltpu.VMEM((d, ff2), BF16), pltpu.VMEM((d_ff, d), BF16),
                            pltpu.SemaphoreType.DMA(()), pltpu.SemaphoreType.DMA(())],
        ),
        out_shape=jax.ShapeDtypeStruct((rows + 2 * EXPERT_BLOCK, half), jnp.uint32),
        compiler_params=_cparams(("arbitrary",)),
        name="moe_expert_ffn",
    )(block_e, ids_prev, ids_prev, x_disp, w1, b1.reshape(depth, n_e, 1, ff2), w2, b2.reshape(depth, n_e, 1, d))


def _combine_kernel(y0_ref, y1_ref, y2_ref, y3_ref, gate_ref, x_ref, mod_ref, fg_ref, o_ref, *, d, final):
    gate = gate_ref[...]
    out = _unpack_bf16_pairs(y0_ref[...]) * gate[:, 0:1]
    for k, y_ref in enumerate((y1_ref, y2_ref, y3_ref), start=1):
        out = out + _unpack_bf16_pairs(y_ref[...]) * gate[:, k:k + 1]
    xn = x_ref[...] + mod_ref[:, 5 * d:6 * d] * out
    if final:
        xn = xn * lax.rsqrt(jnp.mean(xn * xn, axis=-1, keepdims=True) + EPS) * fg_ref[...]
    o_ref[...] = xn


def _combine(y_rows, gates, x, mod, final_g, n_lat_tiles, final):
    b, t, d = x.shape
    nt = t // TM
    n_tiles = b * nt
    nt_out = n_lat_tiles if final else nt
    y_spec = lambda k: pl.BlockSpec((TM, d // 2), lambda i, j: (k * n_tiles + i * nt + j, 0))
    return pl.pallas_call(
        functools.partial(_combine_kernel, d=d, final=final),
        grid=(b, nt_out),
        in_specs=[
            y_spec(0), y_spec(1), y_spec(2), y_spec(3),
            pl.BlockSpec((TM, TOP_K), lambda i, j: (i * nt + j, 0)),
            pl.BlockSpec((None, TM, d), lambda i, j: (i, j, 0)),
            pl.BlockSpec((None, 1, 6 * d), lambda i, j: (jnp.where(j >= n_lat_tiles, b, i), 0, 0)),
            pl.BlockSpec((1, d), lambda i, j: (0, 0)),
        ],
        out_specs=pl.BlockSpec((None, TM, d), lambda i, j: (i, j, 0)),
        out_shape=jax.ShapeDtypeStruct((b, nt_out * TM, d), F32),
        compiler_params=_cparams(("parallel", "parallel")),
        name="moe_combine",
    )(y_rows, y_rows, y_rows, y_rows, gates, x, mod, final_g.reshape(1, d))


def _moe(h2p, idx_t, gates_t, rank_t, cnt, x, mod, layer, w1, b1, w2, b2, final_g, n_lat_tiles, final):
    b, t, d = x.shape
    n_tok = b * t
    n_assign = n_tok * TOP_K
    counts = cnt[:, 0]
    padded = (counts + EXPERT_BLOCK - 1) // EXPERT_BLOCK * EXPERT_BLOCK
    pad_end = jnp.cumsum(padded).astype(jnp.int32)
    pad_start = pad_end - padded
    experts = jnp.arange(N_EXPERTS, dtype=jnp.int32)
    dest_t = rank_t + jnp.sum(jnp.where(idx_t[..., None] == experts, pad_start, 0), axis=-1)
    n_blocks = -(-n_assign // EXPERT_BLOCK) + N_EXPERTS
    block_start = jnp.arange(n_blocks, dtype=jnp.int32) * EXPERT_BLOCK
    block_e = jnp.minimum(jnp.sum(pad_end[None, :] <= block_start[:, None], axis=1), N_EXPERTS - 1).astype(jnp.int32)
    dest_tiles = dest_t.reshape(TOP_K, n_tok // TM, TM).transpose(1, 2, 0).reshape(n_tok // TM, 1, TOP_K * TM)

    x_disp = _dispatch(h2p.reshape(n_tok, d // 2), dest_tiles, counts,
                       jnp.concatenate([pad_start, pad_end[-1:]]), n_blocks)
    ids = lax.bitcast_convert_type(x_disp[:, d // 2], jnp.int32)
    y_rows = _expert_ffn(x_disp, ids, block_e, layer, w1, b1, w2, b2)
    return _combine(y_rows, gates_t.T, x, mod, final_g, n_lat_tiles, final)


def _rope_tables(n_lat, n_ctx):
    rows = n_lat // GRID_W
    row = jnp.repeat(jnp.arange(rows, dtype=F32), GRID_W)
    col = jnp.tile(jnp.arange(GRID_W, dtype=F32), rows)
    axis_dim = HEAD_DIM // 2
    inv = ROPE_THETA ** (-jnp.arange(0, axis_dim, 2, dtype=F32) / axis_dim)
    ang_r = row[:, None] * inv
    ang_c = col[:, None] * inv
    ang = jnp.concatenate([ang_r, ang_r, ang_c, ang_c], axis=-1)
    cos = jnp.concatenate([jnp.cos(ang), jnp.ones((n_ctx, HEAD_DIM), F32)], axis=0)
    sin = jnp.concatenate([jnp.sin(ang), jnp.zeros((n_ctx, HEAD_DIM), F32)], axis=0)
    even = (jnp.arange(HEAD_DIM) // 32) % 2 == 0
    return cos, jnp.where(even, -sin, 0.0), jnp.where(even, 0.0, sin)


def kernel(x, c, ctx, c_ctx, ada_w, ada_b, norm1_g, norm2_g, mix_w_out, router_w, router_b, exp_w1, exp_b1, exp_w2, exp_b2, attn_w_in, attn_q_gain, attn_k_gain, gmlp_w_in, gmlp_v_gain, gmlp_w_s, gmlp_b_s, hyena_w_in, hyena_conv_w, hyena_conv_b, hyena_f_w1, hyena_f_b1, hyena_f_w2, hyena_f_b2, hyena_f_w3, hyena_f_b3, hyena_f_w4, hyena_f_b4, hyena_freq, hyena_d, final_g):
    b, s, d = x.shape
    n_ctx = ctx.shape[1]
    depth = ada_w.shape[0]
    assert s % TM == 0 and n_ctx == TM and b < MOD_ROWS and s % (FFT_N2 * 8) == 0
    n_lat_tiles = s // TM

    xs = jnp.concatenate([x, ctx], axis=1)
    cc = jnp.concatenate([c, c_ctx[None, :], jnp.zeros((MOD_ROWS - b - 1, d), F32)], axis=0)
    mods = _modulation(cc, ada_w, ada_b).reshape(depth, MOD_ROWS, 1, 6 * d)
    rope = _rope_tables(s, n_ctx)

    for i in range(depth):
        kind, j = i % 3, i // 3
        last = i == depth - 1
        mod = mods[i]
        w_out = mix_w_out[i].astype(BF16)
        if kind == 0:
            q, k, v = _attn_proj(xs, mod, norm1_g[i], attn_w_in[j].astype(BF16), attn_q_gain[j], attn_k_gain[j],
                                 rope, n_lat_tiles)
            y = _attention(q, k, v, s)
        elif kind == 2:
            p = _inproj(xs, mod, norm1_g[i], hyena_w_in[j].astype(BF16), n_lat_tiles)
            fargs = (hyena_f_w1[j], hyena_f_b1[j], hyena_f_w2[j], hyena_f_b2[j], hyena_f_w3[j], hyena_f_b3[j],
                     hyena_f_w4[j], hyena_f_b4[j], hyena_freq[j])
            tables = _fft_tables(s)
            kf = _filter_spectrum(*_hyena_filters(s, *fargs), tables)
            y_lat = _hyena_conv(p, hyena_conv_w[j], hyena_conv_b[j], hyena_d[j], kf, tables, s)
            dfwd, dinv = _dense_dft_tables(n_ctx)
            kf_c = _filter_spectrum_dense(*_hyena_filters(n_ctx, *fargs), dfwd)
            y_ctx = _hyena_ctx_conv(p, hyena_conv_w[j], hyena_conv_b[j], hyena_d[j], kf_c, dfwd, dinv, s, n_ctx)
            y = jnp.concatenate([y_lat, y_ctx], axis=1)
        if kind == 1:
            xs, *routed = _gmlp_layer(
                xs, mod, norm1_g[i], gmlp_w_in[j].astype(BF16), gmlp_v_gain[j], gmlp_w_s[j].astype(BF16),
                gmlp_b_s[j], w_out, norm2_g[i], router_w[i], router_b[i], n_lat_tiles)
        else:
            xs, *routed = _post(y, xs, mod, w_out, norm2_g[i], router_w[i], router_b[i], n_lat_tiles)
        xs = _moe(*routed, xs, mod, i, exp_w1, exp_b1, exp_w2, exp_b2, final_g, n_lat_tiles, last)
    return xs
```
